```python
import math
import jax, jax.numpy as jnp
from jax import lax
import numpy as np

D_MODEL = 2048
BATCH = 8
SEQ = 2048
DEPTH = 1

PLE_DIM = 256
MIX_WIDTH = D_MODEL
N_HEADS = 8
QK_NOPE_DIM = 128
QK_ROPE_DIM = 64
V_HEAD_DIM = 128
QK_HEAD_DIM = QK_NOPE_DIM + QK_ROPE_DIM
Q_LORA = 512
KV_LORA = 256
ATTN_WIDTH = N_HEADS * V_HEAD_DIM
ROPE_THETA = 10000.0
Q_BLOCK = 128
SSM_WIDTH = MIX_WIDTH - ATTN_WIDTH
SSM_GROUP = 16
SSM_GROUPS = SSM_WIDTH // SSM_GROUP
SSM_STATE = 64
DT_MIN = 1e-3
DT_MAX = 1e-1
N_IN = Q_LORA + KV_LORA + QK_ROPE_DIM + SSM_WIDTH
D_FF = int(math.ceil((8 * D_MODEL / 3) / 256) * 256)
EPS = 1e-6

kernel_name = "hybrid_mla_s5_parallel_heads"


def rms_norm(t, g):
    tf = t.astype(jnp.float32)
    y = tf * lax.rsqrt(jnp.mean(tf * tf, axis=-1, keepdims=True) + EPS)
    return (y * g.astype(jnp.float32)).astype(t.dtype)


def rope_tables(positions):
    inv_freq = 1.0 / (ROPE_THETA ** (jnp.arange(0, QK_ROPE_DIM, 2, dtype=jnp.float32) / QK_ROPE_DIM))
    ang = positions.astype(jnp.float32)[..., None] * inv_freq
    return jnp.cos(ang)[:, None], jnp.sin(ang)[:, None]


def apply_rope(t, cos, sin):
    half = QK_ROPE_DIM // 2
    t1 = t[..., :half].astype(jnp.float32)
    t2 = t[..., half:].astype(jnp.float32)
    return jnp.concatenate([t1 * cos - t2 * sin, t2 * cos + t1 * sin], axis=-1).astype(t.dtype)


def causal_block_attention(q, k, v):
    L = q.shape[2]
    scale = QK_HEAD_DIM ** -0.5
    outs = []
    for i in range(L // Q_BLOCK):
        kv_len = (i + 1) * Q_BLOCK
        q_blk = q[:, :, i * Q_BLOCK:kv_len]
        s = jnp.einsum('bhqd,bhkd->bhqk', q_blk, k[:, :, :kv_len]).astype(jnp.float32) * scale
        q_idx = i * Q_BLOCK + jnp.arange(Q_BLOCK)[:, None]
        k_idx = jnp.arange(kv_len)[None, :]
        s = jnp.where(k_idx <= q_idx, s, -jnp.inf)
        pr = jax.nn.softmax(s, axis=-1).astype(v.dtype)
        outs.append(jnp.einsum('bhqk,bhkd->bhqd', pr, v[:, :, :kv_len]))
    return jnp.concatenate(outs, axis=2)


def mla_mixer(z_q, z_kv, z_kr, cos, sin, g_q_lora, w_uq, g_kv_lora, w_ukv, g_q_head, g_k_head):
    B_, L, _ = z_q.shape
    c_q = rms_norm(z_q, g_q_lora)
    q = (c_q @ w_uq).reshape(B_, L, N_HEADS, QK_HEAD_DIM)
    c_kv = rms_norm(z_kv, g_kv_lora)
    kv = (c_kv @ w_ukv).reshape(B_, L, N_HEADS, QK_NOPE_DIM + V_HEAD_DIM)
    k_nope, v = kv[..., :QK_NOPE_DIM], kv[..., QK_NOPE_DIM:]
    k_rope = jnp.broadcast_to(z_kr[:, :, None, :], (B_, L, N_HEADS, QK_ROPE_DIM))
    k = jnp.concatenate([k_nope, k_rope], axis=-1)
    q = rms_norm(q, g_q_head).transpose(0, 2, 1, 3)
    k = rms_norm(k, g_k_head).transpose(0, 2, 1, 3)
    v = v.transpose(0, 2, 1, 3)
    q = jnp.concatenate([q[..., :QK_NOPE_DIM], apply_rope(q[..., QK_NOPE_DIM:], cos, sin)], axis=-1)
    k = jnp.concatenate([k[..., :QK_NOPE_DIM], apply_rope(k[..., QK_NOPE_DIM:], cos, sin)], axis=-1)
    o = causal_block_attention(q, k, v)
    return o.transpose(0, 2, 1, 3).reshape(B_, L, ATTN_WIDTH)


def _ssm_combine(earlier, later):
    ar_i, ai_i, br_i, bi_i = earlier
    ar_j, ai_j, br_j, bi_j = later
    ar = ar_j * ar_i - ai_j * ai_i
    ai = ar_j * ai_i + ai_j * ar_i
    br = ar_j * br_i - ai_j * bi_i + br_j
    bi = ar_j * bi_i + ai_j * br_i + bi_j
    return ar, ai, br, bi


def s5_mixer(u, lam_re, lam_im, log_dt, b_re, b_im, c_re, c_im, d_skip, w_glu, b_glu):
    B_, L, _ = u.shape
    f32 = jnp.float32
    uf = u.astype(f32).reshape(B_, L, SSM_GROUPS, SSM_GROUP)
    lr = jnp.minimum(lam_re.astype(f32), -1e-4)
    li = lam_im.astype(f32)
    dt = jnp.exp(log_dt.astype(f32))[:, None]
    mag = jnp.exp(lr * dt)
    abar_re = mag * jnp.cos(li * dt)
    abar_im = mag * jnp.sin(li * dt)
    den = lr * lr + li * li
    num_re = abar_re - 1.0
    num_im = abar_im
    coef_re = (num_re * lr + num_im * li) / den
    coef_im = (num_im * lr - num_re * li) / den
    br = b_re.astype(f32)
    bim = b_im.astype(f32)
    bb_re = coef_re[..., None] * br - coef_im[..., None] * bim
    bb_im = coef_re[..., None] * bim + coef_im[..., None] * br
    bu_re = jnp.einsum('blgh,gph->blgp', uf, bb_re)
    bu_im = jnp.einsum('blgh,gph->blgp', uf, bb_im)
    a_re = jnp.broadcast_to(abar_re[None, None], (1, L, SSM_GROUPS, SSM_STATE))
    a_im = jnp.broadcast_to(abar_im[None, None], (1, L, SSM_GROUPS, SSM_STATE))
    _, _, s_re, s_im = lax.associative_scan(_ssm_combine, (a_re, a_im, bu_re, bu_im), axis=1)
    y = (jnp.einsum('blgp,ghp->blgh', s_re, c_re.astype(f32))
         - jnp.einsum('blgp,ghp->blgh', s_im, c_im.astype(f32))
         + d_skip.astype(f32) * uf)
    y = jax.nn.gelu(y.reshape(B_, L, SSM_WIDTH).astype(u.dtype))
    return y * jax.nn.sigmoid(y @ w_glu + b_glu)


def setup_inputs(seed: int = 0) -> dict:
    key = jax.random.key(seed)
    ks = jax.random.split(key, 32)
    f32 = jnp.float32

    def nrm(k, shape, scale):
        return jax.random.normal(k, shape, f32) * scale

    def gain(k, n):
        return 1.0 + 0.01 * jax.random.normal(k, (DEPTH, n), f32)

    x = jax.random.normal(ks[0], (BATCH, SEQ, D_MODEL), f32)
    p = jax.random.normal(ks[1], (DEPTH, BATCH, SEQ, PLE_DIM), f32)
    offs = jax.random.randint(ks[2], (BATCH, 1), 0, 1024, dtype=jnp.int32)
    positions = (jnp.arange(SEQ, dtype=jnp.int32)[None, :] + offs).astype(jnp.int32)

    G, P, H = SSM_GROUPS, SSM_STATE, SSM_GROUP
    lam_re = -0.5 + 0.01 * jax.random.normal(ks[3], (DEPTH, G, P), f32)
    lam_im = (math.pi * jnp.arange(P, dtype=f32))[None, None] + 0.01 * jax.random.normal(ks[4], (DEPTH, G, P), f32)
    log_dt = jax.random.uniform(ks[5], (DEPTH, G), f32, math.log(DT_MIN), math.log(DT_MAX))

    return {
        "x": x,
        "p": p,
        "positions": positions,
        "g_mix_norm": gain(ks[6], D_MODEL),
        "w_in": nrm(ks[7], (DEPTH, D_MODEL, N_IN), D_MODEL ** -0.5),
        "g_q_lora": gain(ks[8], Q_LORA),
        "w_uq": nrm(ks[9], (DEPTH, Q_LORA, N_HEADS * QK_HEAD_DIM), Q_LORA ** -0.5),
        "g_kv_lora": gain(ks[10], KV_LORA),
        "w_ukv": nrm(ks[11], (DEPTH, KV_LORA, N_HEADS * (QK_NOPE_DIM + V_HEAD_DIM)), KV_LORA ** -0.5),
        "g_q_head": gain(ks[12], QK_HEAD_DIM),
        "g_k_head": gain(ks[13], QK_HEAD_DIM),
        "lam_re": lam_re,
        "lam_im": lam_im,
        "log_dt": log_dt,
        "b_re": nrm(ks[14], (DEPTH, G, P, H), (2 * H) ** -0.5),
        "b_im": nrm(ks[15], (DEPTH, G, P, H), (2 * H) ** -0.5),
        "c_re": nrm(ks[16], (DEPTH, G, H, P), (2 * P) ** -0.5 * 4.0),
        "c_im": nrm(ks[17], (DEPTH, G, H, P), (2 * P) ** -0.5 * 4.0),
        "d_skip": nrm(ks[18], (DEPTH, G, H), 1.0),
        "w_glu": nrm(ks[19], (DEPTH, SSM_WIDTH, SSM_WIDTH), SSM_WIDTH ** -0.5),
        "b_glu": nrm(ks[20], (DEPTH, SSM_WIDTH), 0.01),
        "g_out_attn": gain(ks[21], ATTN_WIDTH),
        "g_out_ssm": gain(ks[22], SSM_WIDTH),
        "w_o": nrm(ks[23], (DEPTH, MIX_WIDTH, D_MODEL), MIX_WIDTH ** -0.5),
        "g_ffn_norm": gain(ks[24], D_MODEL),
        "w_gate": nrm(ks[25], (DEPTH, D_MODEL, D_FF), D_MODEL ** -0.5),
        "w_up": nrm(ks[26], (DEPTH, D_MODEL, D_FF), D_MODEL ** -0.5),
        "w_down": nrm(ks[27], (DEPTH, D_FF, D_MODEL), D_FF ** -0.5),
        "g_ple_norm": gain(ks[28], D_MODEL),
        "w_ple_gate": nrm(ks[29], (DEPTH, D_MODEL, D_MODEL), D_MODEL ** -0.5),
        "w_ple_proj": nrm(ks[30], (DEPTH, PLE_DIM, D_MODEL), PLE_DIM ** -0.5),
    }


def reference(x, p, positions, g_mix_norm, w_in, g_q_lora, w_uq, g_kv_lora, w_ukv,
              g_q_head, g_k_head, lam_re, lam_im, log_dt, b_re, b_im, c_re, c_im, d_skip,
              w_glu, b_glu, g_out_attn, g_out_ssm, w_o, g_ffn_norm, w_gate, w_up, w_down,
              g_ple_norm, w_ple_gate, w_ple_proj):
    cos, sin = rope_tables(positions)
    o1 = Q_LORA
    o2 = o1 + KV_LORA
    o3 = o2 + QK_ROPE_DIM
    for i in range(DEPTH):
        h = rms_norm(x, g_mix_norm[i])
        z = h @ w_in[i]
        o_attn = mla_mixer(z[..., :o1], z[..., o1:o2], z[..., o2:o3], cos, sin,
                           g_q_lora[i], w_uq[i], g_kv_lora[i], w_ukv[i],
                           g_q_head[i], g_k_head[i])
        o_ssm = s5_mixer(z[..., o3:], lam_re[i], lam_im[i], log_dt[i], b_re[i], b_im[i],
                         c_re[i], c_im[i], d_skip[i], w_glu[i], b_glu[i])
        mixed = jnp.concatenate([rms_norm(o_attn, g_out_attn[i]),
                                 rms_norm(o_ssm, g_out_ssm[i])], axis=-1)
        x = x + mixed @ w_o[i]
        hf = rms_norm(x, g_ffn_norm[i])
        x = x + (jax.nn.silu(hf @ w_gate[i]) * (hf @ w_up[i])) @ w_down[i]
        gate = jax.nn.sigmoid(rms_norm(x, g_ple_norm[i]) @ w_ple_gate[i])
        x = x + gate * (p[i] @ w_ple_proj[i])
    return x
```

```python
import functools
import math

import jax
import jax.numpy as jnp
from jax import lax
from jax.experimental import pallas as pl
from jax.experimental.pallas import tpu as pltpu

D_MODEL = 2048
PLE_DIM = 256
N_HEADS = 8
QK_NOPE_DIM = 128
QK_ROPE_DIM = 64
V_HEAD_DIM = 128
QK_HEAD_DIM = QK_NOPE_DIM + QK_ROPE_DIM
Q_LORA = 512
KV_LORA = 256
ATTN_WIDTH = N_HEADS * V_HEAD_DIM
ROPE_THETA = 10000.0
SSM_WIDTH = 1024
SSM_GROUP = 16
SSM_GROUPS = SSM_WIDTH // SSM_GROUP
SSM_STATE = 64
D_FF = 5632
EPS = 1e-6

LANE = 128
SUBLANE = 8
HEAD_PAD = 2 * LANE
ROPE_HALF = QK_ROPE_DIM // 2
KR_PAD = LANE
N_IN_PAD = Q_LORA + KV_LORA + KR_PAD + SSM_WIDTH
GROUP_BLOCK = 16
N_GROUP_BLOCKS = SSM_GROUPS // GROUP_BLOCK
BLOCK_CH = GROUP_BLOCK * SSM_GROUP
BLOCK_STATES = GROUP_BLOCK * SSM_STATE
VMEM_LIMIT = 56 * 1024 * 1024

BF16 = jnp.bfloat16
F32 = jnp.float32


def _rms(t, g, width=None):
    n = t.shape[-1] if width is None else width
    ss = jnp.sum(t * t, axis=-1, keepdims=True) * (1.0 / n)
    return t * lax.rsqrt(ss + EPS) * g


def _dot(a, b):
    return jnp.dot(a, b, preferred_element_type=F32)


def _sigmoid(t):
    return 1.0 / (1.0 + jnp.exp(-t))


def _inproj_kernel(x_ref, pos_ref, gmix_ref, win_ref, gql_ref, wuq_ref, gkvl_ref, wukv_ref,
                   gq_ref, gk_ref, freq_ref, sgn_ref,
                   q_ref, k_ref, v_ref, u_ref):
    x = x_ref[...]
    h = _rms(x, gmix_ref[...]).astype(BF16)
    z = _dot(h, win_ref[...])
    o1, o2, o3 = Q_LORA, Q_LORA + KV_LORA, Q_LORA + KV_LORA + KR_PAD
    u_ref[...] = z[:, o3:].astype(BF16)
    c_q = _rms(z[:, :o1], gql_ref[...]).astype(BF16)
    q = _dot(c_q, wuq_ref[...])
    c_kv = _rms(z[:, o1:o2], gkvl_ref[...]).astype(BF16)
    kv = _dot(c_kv, wukv_ref[...])
    kr = z[:, o2:o3]
    v_ref[...] = kv[:, ATTN_WIDTH:].astype(BF16)

    ang = pos_ref[...].astype(F32) * freq_ref[...]
    cos_t = jnp.cos(ang)
    sin_t = jnp.sin(ang) * sgn_ref[...]

    def rope(t):
        return t * cos_t + pltpu.roll(t, LANE // 2, 1) * sin_t

    gq = gq_ref[...]
    gk = gk_ref[...]
    scale = QK_HEAD_DIM ** -0.5
    kr_ss = jnp.sum(kr * kr, axis=-1, keepdims=True)
    for hd in range(N_HEADS):
        qh = q[:, hd * HEAD_PAD:(hd + 1) * HEAD_PAD]
        qn = _rms(qh, gq, width=QK_HEAD_DIM) * scale
        q_ref[:, hd * HEAD_PAD:hd * HEAD_PAD + LANE] = qn[:, :LANE].astype(BF16)
        q_ref[:, hd * HEAD_PAD + LANE:(hd + 1) * HEAD_PAD] = rope(qn[:, LANE:]).astype(BF16)
        kn = kv[:, hd * QK_NOPE_DIM:(hd + 1) * QK_NOPE_DIM]
        ss = (jnp.sum(kn * kn, axis=-1, keepdims=True) + kr_ss) * (1.0 / QK_HEAD_DIM)
        rinv = lax.rsqrt(ss + EPS)
        k_ref[:, hd * HEAD_PAD:hd * HEAD_PAD + LANE] = (kn * rinv * gk[:, :LANE]).astype(BF16)
        k_ref[:, hd * HEAD_PAD + LANE:(hd + 1) * HEAD_PAD] = rope(kr * rinv * gk[:, LANE:]).astype(BF16)


def _inproj(x2d, pos2d, gmix, win, gql, wuq, gkvl, wukv, gq, gk, freq, sgn, *, batch, seq, tl):
    nl = seq // tl
    tok = lambda w: pl.BlockSpec((tl, w), lambda b, i: (b * nl + i, 0))
    full = lambda a: pl.BlockSpec(a.shape, lambda b, i: (0,) * a.ndim)
    n_tok = batch * seq
    return pl.pallas_call(
        _inproj_kernel,
        grid=(batch, nl),
        in_specs=[tok(D_MODEL), tok(1), full(gmix), full(win), full(gql), full(wuq), full(gkvl),
                  full(wukv), full(gq), full(gk), full(freq), full(sgn)],
        out_specs=[tok(N_HEADS * HEAD_PAD), tok(N_HEADS * HEAD_PAD), tok(ATTN_WIDTH),
                   pl.BlockSpec((tl, SSM_WIDTH), lambda b, i: (i, b))],
        out_shape=[jax.ShapeDtypeStruct((n_tok, N_HEADS * HEAD_PAD), BF16),
                   jax.ShapeDtypeStruct((n_tok, N_HEADS * HEAD_PAD), BF16),
                   jax.ShapeDtypeStruct((n_tok, ATTN_WIDTH), BF16),
                   jax.ShapeDtypeStruct((seq, batch * SSM_WIDTH), BF16)],
        compiler_params=pltpu.CompilerParams(
            dimension_semantics=("arbitrary", "arbitrary"), vmem_limit_bytes=VMEM_LIMIT),
    )(x2d, pos2d, gmix, win, gql, wuq, gkvl, wukv, gq, gk, freq, sgn)


def _attn_kernel(q_ref, k_ref, v_ref, o_ref, *, tq):
    seq = q_ref.shape[0]
    for i in range(seq // tq):
        kv_len = (i + 1) * tq
        q = q_ref[i * tq:(i + 1) * tq, :]
        k = k_ref[:kv_len, :]
        s = lax.dot_general(q, k, (((1,), (1,)), ((), ())), preferred_element_type=F32)
        row = lax.broadcasted_iota(jnp.int32, s.shape, 0) + i * tq
        col = lax.broadcasted_iota(jnp.int32, s.shape, 1)
        s = jnp.where(col <= row, s, -jnp.inf)
        m = jnp.max(s, axis=-1, keepdims=True)
        p = jnp.exp(s - m)
        l = jnp.sum(p, axis=-1, keepdims=True)
        o = _dot(p.astype(BF16), v_ref[:kv_len, :])
        o_ref[i * tq:(i + 1) * tq, :] = (o / l).astype(o_ref.dtype)


def _attention(q, k, v, *, batch, seq, tq):
    return pl.pallas_call(
        functools.partial(_attn_kernel, tq=tq),
        grid=(batch, N_HEADS),
        in_specs=[pl.BlockSpec((seq, HEAD_PAD), lambda b, h: (b, h)),
                  pl.BlockSpec((seq, HEAD_PAD), lambda b, h: (b, h)),
                  pl.BlockSpec((seq, V_HEAD_DIM), lambda b, h: (b, h))],
        out_specs=pl.BlockSpec((seq, V_HEAD_DIM), lambda b, h: (b, h)),
        out_shape=jax.ShapeDtypeStruct((batch * seq, ATTN_WIDTH), BF16),
        compiler_params=pltpu.CompilerParams(
            dimension_semantics=("arbitrary", "arbitrary"), vmem_limit_bytes=VMEM_LIMIT),
    )(q, k, v)


def _s5_kernel(u_ref, bmat_ref, are_ref, aim_ref, cmat_ref, dskip_ref, wglu_ref, bglu_ref, gout_ref,
               o_ref, state_ref, buf_ref, y_ref, *, steps, batch):
    @pl.when(pl.program_id(0) == 0)
    def _():
        state_ref[...] = jnp.zeros_like(state_ref)

    for j in range(N_GROUP_BLOCKS):
        ch = slice(j * BLOCK_CH, (j + 1) * BLOCK_CH)
        uj = u_ref[:, ch]
        buf_ref[...] = _dot(uj, bmat_ref[j])
        a_re = are_ref[j]
        a_im = aim_ref[j]

        def step(t, carry):
            s_re, s_im = carry
            r0 = pl.multiple_of(t * batch, batch)
            b_re = buf_ref[pl.ds(r0, batch), :BLOCK_STATES]
            b_im = buf_ref[pl.ds(r0, batch), BLOCK_STATES:]
            n_re = a_re * s_re - a_im * s_im + b_re
            n_im = a_re * s_im + a_im * s_re + b_im
            buf_ref[pl.ds(r0, batch), :BLOCK_STATES] = n_re
            buf_ref[pl.ds(r0, batch), BLOCK_STATES:] = n_im
            return n_re, n_im

        s_re, s_im = lax.fori_loop(0, steps, step,
                                   (state_ref[j, :, :BLOCK_STATES], state_ref[j, :, BLOCK_STATES:]),
                                   unroll=4)
        state_ref[j, :, :BLOCK_STATES] = s_re
        state_ref[j, :, BLOCK_STATES:] = s_im
        y_ref[:, ch] = (_dot(buf_ref[...].astype(BF16), cmat_ref[j])
                        + dskip_ref[:, ch] * uj.astype(F32))

    y = jax.nn.gelu(y_ref[...], approximate=True)
    gate = _sigmoid(_dot(y.astype(BF16), wglu_ref[...]) + bglu_ref[...])
    o_ref[...] = _rms(y * gate, gout_ref[...]).astype(o_ref.dtype)


def _s5(u_tm, bmat, a_re, a_im, cmat, dskip, wglu, bglu, gout, *, batch, seq, steps):
    rows = steps * batch
    full = lambda a: pl.BlockSpec(a.shape, lambda i: (0,) * a.ndim)
    return pl.pallas_call(
        functools.partial(_s5_kernel, steps=steps, batch=batch),
        grid=(seq // steps,),
        in_specs=[pl.BlockSpec((rows, SSM_WIDTH), lambda i: (i, 0)),
                  full(bmat), full(a_re), full(a_im), full(cmat), full(dskip), full(wglu),
                  full(bglu), full(gout)],
        out_specs=pl.BlockSpec((rows, SSM_WIDTH), lambda i: (i, 0)),
        out_shape=jax.ShapeDtypeStruct((seq * batch, SSM_WIDTH), BF16),
        scratch_shapes=[pltpu.VMEM((N_GROUP_BLOCKS, batch, 2 * BLOCK_STATES), F32),
                        pltpu.VMEM((rows, 2 * BLOCK_STATES), F32),
                        pltpu.VMEM((rows, SSM_WIDTH), F32)],
        compiler_params=pltpu.CompilerParams(
            dimension_semantics=("arbitrary",), vmem_limit_bytes=VMEM_LIMIT),
    )(u_tm, bmat, a_re, a_im, cmat, dskip, wglu, bglu, gout)


def _oproj_kernel(x_ref, oa_ref, os_ref, ga_ref, woa_ref, wos_ref, out_ref):
    oa = _rms(oa_ref[...].astype(F32), ga_ref[...]).astype(BF16)
    out_ref[...] = x_ref[...] + _dot(oa, woa_ref[...]) + _dot(os_ref[...], wos_ref[...])


def _oproj(x2d, o_attn, o_ssm_tm, g_attn, wo_a, wo_s, *, batch, seq, tl):
    nl = seq // tl
    tok = lambda w: pl.BlockSpec((tl, w), lambda b, i: (b * nl + i, 0))
    full = lambda a: pl.BlockSpec(a.shape, lambda b, i: (0,) * a.ndim)
    return pl.pallas_call(
        _oproj_kernel,
        grid=(batch, nl),
        in_specs=[tok(D_MODEL), tok(ATTN_WIDTH),
                  pl.BlockSpec((tl, SSM_WIDTH), lambda b, i: (i, b)),
                  full(g_attn), full(wo_a), full(wo_s)],
        out_specs=tok(D_MODEL),
        out_shape=jax.ShapeDtypeStruct((batch * seq, D_MODEL), F32),
        compiler_params=pltpu.CompilerParams(
            dimension_semantics=("arbitrary", "arbitrary"), vmem_limit_bytes=VMEM_LIMIT),
    )(x2d, o_attn, o_ssm_tm, g_attn, wo_a, wo_s)


def _ffn_kernel(x_ref, g_ref, wg_ref, wu_ref, wd_ref, out_ref, h_ref):
    j = pl.program_id(1)

    @pl.when(j == 0)
    def _():
        x = x_ref[...]
        h_ref[...] = _rms(x, g_ref[...]).astype(BF16)
        out_ref[...] = x

    h = h_ref[...]
    gate = _dot(h, wg_ref[...])
    up = _dot(h, wu_ref[...])
    act = (gate * _sigmoid(gate) * up).astype(BF16)
    out_ref[...] += _dot(act, wd_ref[...])


def _ffn(x2d, g, wg, wu, wd, *, tm, tf):
    n_tok = x2d.shape[0]
    return pl.pallas_call(
        _ffn_kernel,
        grid=(n_tok // tm, D_FF // tf),
        in_specs=[pl.BlockSpec((tm, D_MODEL), lambda i, j: (i, 0)),
                  pl.BlockSpec((1, D_MODEL), lambda i, j: (0, 0)),
                  pl.BlockSpec((D_MODEL, tf), lambda i, j: (0, j)),
                  pl.BlockSpec((D_MODEL, tf), lambda i, j: (0, j)),
                  pl.BlockSpec((tf, D_MODEL), lambda i, j: (j, 0))],
        out_specs=pl.BlockSpec((tm, D_MODEL), lambda i, j: (i, 0)),
        out_shape=jax.ShapeDtypeStruct((n_tok, D_MODEL), F32),
        scratch_shapes=[pltpu.VMEM((tm, D_MODEL), BF16)],
        compiler_params=pltpu.CompilerParams(
            dimension_semantics=("arbitrary", "arbitrary"), vmem_limit_bytes=VMEM_LIMIT),
    )(x2d, g, wg, wu, wd)


def _ple_kernel(x_ref, p_ref, g_ref, wpg_ref, wpp_ref, out_ref):
    x = x_ref[...]
    h = _rms(x, g_ref[...]).astype(BF16)
    gate = _sigmoid(_dot(h, wpg_ref[...]))
    out_ref[...] = x + gate * _dot(p_ref[...].astype(BF16), wpp_ref[...])


def _ple(x2d, p2d, g, wpg, wpp, *, tm):
    n_tok = x2d.shape[0]
    full = lambda a: pl.BlockSpec(a.shape, lambda i: (0,) * a.ndim)
    return pl.pallas_call(
        _ple_kernel,
        grid=(n_tok // tm,),
        in_specs=[pl.BlockSpec((tm, D_MODEL), lambda i: (i, 0)),
                  pl.BlockSpec((tm, PLE_DIM), lambda i: (i, 0)),
                  full(g), full(wpg), full(wpp)],
        out_specs=pl.BlockSpec((tm, D_MODEL), lambda i: (i, 0)),
        out_shape=jax.ShapeDtypeStruct((n_tok, D_MODEL), F32),
        compiler_params=pltpu.CompilerParams(
            dimension_semantics=("arbitrary",), vmem_limit_bytes=VMEM_LIMIT),
    )(x2d, p2d, g, wpg, wpp)


def _rope_tile(t):
    z = jnp.zeros(t.shape[:-1] + (ROPE_HALF,), t.dtype)
    return jnp.concatenate([t[..., :ROPE_HALF], z, t[..., ROPE_HALF:], z], axis=-1)


def _head_gain(g):
    return jnp.concatenate([g[:QK_NOPE_DIM], _rope_tile(g[QK_NOPE_DIM:])])[None, :].astype(F32)


def _s5_params(lam_re, lam_im, log_dt, b_re, b_im, c_re, c_im):
    G, P, H, NB, GB = SSM_GROUPS, SSM_STATE, SSM_GROUP, N_GROUP_BLOCKS, GROUP_BLOCK
    lr = jnp.minimum(lam_re.astype(F32), -1e-4)
    li = lam_im.astype(F32)
    dt = jnp.exp(log_dt.astype(F32))[:, None]
    mag = jnp.exp(lr * dt)
    abar_re = mag * jnp.cos(li * dt)
    abar_im = mag * jnp.sin(li * dt)
    den = lr * lr + li * li
    num_re = abar_re - 1.0
    num_im = abar_im
    coef_re = (num_re * lr + num_im * li) / den
    coef_im = (num_im * lr - num_re * li) / den
    br = b_re.astype(F32)
    bim = b_im.astype(F32)
    bb_re = coef_re[..., None] * br - coef_im[..., None] * bim
    bb_im = coef_re[..., None] * bim + coef_im[..., None] * br
    eye = jnp.eye(GB, dtype=F32)

    def in_block(bb):
        t = bb.reshape(NB, GB, P, H)
        return jnp.einsum('ngph,gk->nghkp', t, eye).reshape(NB, GB * H, GB * P)

    def out_block(cc):
        t = cc.reshape(NB, GB, H, P)
        return jnp.einsum('nghp,gk->ngpkh', t, eye).reshape(NB, GB * P, GB * H)

    bmat = jnp.concatenate([in_block(bb_re), in_block(bb_im)], axis=2).astype(BF16)
    cmat = jnp.concatenate([out_block(c_re.astype(F32)), -out_block(c_im.astype(F32))], axis=1).astype(BF16)
    bcast = lambda a: jnp.broadcast_to(a.reshape(NB, 1, GB * P), (NB, SUBLANE, GB * P))
    return bmat, bcast(abar_re), bcast(abar_im), cmat


def kernel(x, p, positions, g_mix_norm, w_in, g_q_lora, w_uq, g_kv_lora, w_ukv, g_q_head, g_k_head,
           lam_re, lam_im, log_dt, b_re, b_im, c_re, c_im, d_skip, w_glu, b_glu, g_out_attn,
           g_out_ssm, w_o, g_ffn_norm, w_gate, w_up, w_down, g_ple_norm, w_ple_gate, w_ple_proj):
    batch, seq, _ = x.shape
    assert batch == SUBLANE, "the S5 scan keeps one batch row per sublane"
    depth = w_in.shape[0]
    n_tok = batch * seq
    row = lambda g: g[None, :].astype(F32)

    inv_freq = 1.0 / (ROPE_THETA ** (jnp.arange(0, QK_ROPE_DIM, 2, dtype=F32) / QK_ROPE_DIM))
    freq = _rope_tile(jnp.concatenate([inv_freq, inv_freq]))[None, :]
    ones = jnp.ones((ROPE_HALF,), F32)
    sgn = _rope_tile(jnp.concatenate([-ones, ones]))[None, :]
    pos2d = positions.reshape(n_tok, 1)

    x2d = x.reshape(n_tok, D_MODEL)
    for i in range(depth):
        o1, o2, o3 = Q_LORA, Q_LORA + KV_LORA, Q_LORA + KV_LORA + QK_ROPE_DIM
        wi = w_in[i]
        win = jnp.concatenate([wi[:, :o2], _rope_tile(wi[:, o2:o3]), wi[:, o3:]], axis=1).astype(BF16)
        wq = w_uq[i].reshape(Q_LORA, N_HEADS, QK_HEAD_DIM)
        wuq = jnp.concatenate([wq[..., :QK_NOPE_DIM], _rope_tile(wq[..., QK_NOPE_DIM:])], axis=-1)
        wuq = wuq.reshape(Q_LORA, N_HEADS * HEAD_PAD).astype(BF16)
        wkv = w_ukv[i].reshape(KV_LORA, N_HEADS, QK_NOPE_DIM + V_HEAD_DIM)
        wukv = jnp.concatenate([wkv[..., :QK_NOPE_DIM].reshape(KV_LORA, -1),
                                wkv[..., QK_NOPE_DIM:].reshape(KV_LORA, -1)], axis=1).astype(BF16)

        q, k, v, u_tm = _inproj(x2d, pos2d, row(g_mix_norm[i]), win, row(g_q_lora[i]), wuq,
                                row(g_kv_lora[i]), wukv, _head_gain(g_q_head[i]),
                                _head_gain(g_k_head[i]), freq, sgn, batch=batch, seq=seq, tl=512)
        o_attn = _attention(q, k, v, batch=batch, seq=seq, tq=256)

        bmat, a_re, a_im, cmat = _s5_params(lam_re[i], lam_im[i], log_dt[i], b_re[i], b_im[i],
                                            c_re[i], c_im[i])
        o_ssm_tm = _s5(u_tm.reshape(seq * batch, SSM_WIDTH), bmat, a_re, a_im, cmat,
                       d_skip[i].reshape(1, SSM_WIDTH).astype(F32), w_glu[i].astype(BF16),
                       row(b_glu[i]), row(g_out_ssm[i]), batch=batch, seq=seq, steps=64)

        wo = w_o[i].astype(BF16)
        x2d = _oproj(x2d, o_attn, o_ssm_tm.reshape(seq, batch * SSM_WIDTH), row(g_out_attn[i]),
                     wo[:ATTN_WIDTH], wo[ATTN_WIDTH:], batch=batch, seq=seq, tl=512)
        x2d = _ffn(x2d, row(g_ffn_norm[i]), w_gate[i].astype(BF16), w_up[i].astype(BF16),
                   w_down[i].astype(BF16), tm=512, tf=512)
        x2d = _ple(x2d, p[i].reshape(n_tok, PLE_DIM), row(g_ple_norm[i]),
                   w_ple_gate[i].astype(BF16), w_ple_proj[i].astype(BF16), tm=512)
    return x2d.reshape(batch, seq, D_MODEL)
```

```python
import functools

import jax
import jax.numpy as jnp
from jax import lax
from jax.experimental import pallas as pl
from jax.experimental.pallas import tpu as pltpu

D_MODEL = 2048
PLE_DIM = 256
N_HEADS = 8
QK_NOPE_DIM = 128
QK_ROPE_DIM = 64
V_HEAD_DIM = 128
QK_HEAD_DIM = QK_NOPE_DIM + QK_ROPE_DIM
Q_LORA = 512
KV_LORA = 256
ATTN_WIDTH = N_HEADS * V_HEAD_DIM
ROPE_THETA = 10000.0
SSM_WIDTH = 1024
SSM_GROUP = 16
SSM_GROUPS = SSM_WIDTH // SSM_GROUP
SSM_STATE = 64
D_FF = 5632
EPS = 1e-6

LANE = 128
SUBLANE = 8
HEAD_PAD = 2 * LANE
ROPE_HALF = QK_ROPE_DIM // 2
KR_PAD = LANE
N_IN_PAD = Q_LORA + KV_LORA + KR_PAD + SSM_WIDTH
CHUNK = SUBLANE
N_LANE_TILES = SSM_WIDTH // LANE
GROUPS_PER_TILE = LANE // SSM_GROUP
PAIR_W = 2 * LANE
N_PAIRS = SSM_GROUPS // 2
VMEM_LIMIT = 56 * 1024 * 1024

BF16 = jnp.bfloat16
F32 = jnp.float32


def _rms(t, g, width=None):
    n = t.shape[-1] if width is None else width
    ss = jnp.sum(t * t, axis=-1, keepdims=True) * (1.0 / n)
    return t * lax.rsqrt(ss + EPS) * g


def _dot(a, b):
    return jnp.dot(a, b, preferred_element_type=F32)


def _sigmoid(t):
    return 1.0 / (1.0 + jnp.exp(-t))


def _params(*sem):
    return pltpu.CompilerParams(dimension_semantics=sem, vmem_limit_bytes=VMEM_LIMIT)


def _inproj_kernel(x_ref, pos_ref, gmix_ref, win_ref, gql_ref, wuq_ref, gkvl_ref, wukv_ref,
                   gq_ref, gk_ref, freq_ref, sgn_ref,
                   q_ref, k_ref, v_ref, u_ref):
    x = x_ref[...]
    tl = x.shape[0]
    h = _rms(x, gmix_ref[...]).astype(BF16)
    z = _dot(h, win_ref[...])
    o1, o2, o3 = Q_LORA, Q_LORA + KV_LORA, Q_LORA + KV_LORA + KR_PAD
    for kt in range(N_LANE_TILES):
        u_ref[:, kt, :, :] = z[:, o3 + kt * LANE:o3 + (kt + 1) * LANE].reshape(tl // CHUNK, CHUNK, LANE)
    c_q = _rms(z[:, :o1], gql_ref[...]).astype(BF16)
    q = _dot(c_q, wuq_ref[...])
    c_kv = _rms(z[:, o1:o2], gkvl_ref[...]).astype(BF16)
    kv = _dot(c_kv, wukv_ref[...])
    kr = z[:, o2:o3]
    v_ref[...] = kv[:, ATTN_WIDTH:].astype(BF16)

    ang = pos_ref[...].astype(F32) * freq_ref[...]
    cos_t = jnp.cos(ang)
    sin_t = jnp.sin(ang) * sgn_ref[...]

    def rope(t):
        return t * cos_t + pltpu.roll(t, LANE // 2, 1) * sin_t

    gq = gq_ref[...]
    gk = gk_ref[...]
    scale = QK_HEAD_DIM ** -0.5
    kr_ss = jnp.sum(kr * kr, axis=-1, keepdims=True)
    for hd in range(N_HEADS):
        qh = q[:, hd * HEAD_PAD:(hd + 1) * HEAD_PAD]
        qn = _rms(qh, gq, width=QK_HEAD_DIM) * scale
        q_ref[:, hd * HEAD_PAD:hd * HEAD_PAD + LANE] = qn[:, :LANE].astype(BF16)
        q_ref[:, hd * HEAD_PAD + LANE:(hd + 1) * HEAD_PAD] = rope(qn[:, LANE:]).astype(BF16)
        kn = kv[:, hd * QK_NOPE_DIM:(hd + 1) * QK_NOPE_DIM]
        ss = (jnp.sum(kn * kn, axis=-1, keepdims=True) + kr_ss) * (1.0 / QK_HEAD_DIM)
        rinv = lax.rsqrt(ss + EPS)
        k_ref[:, hd * HEAD_PAD:hd * HEAD_PAD + LANE] = (kn * rinv * gk[:, :LANE]).astype(BF16)
        k_ref[:, hd * HEAD_PAD + LANE:(hd + 1) * HEAD_PAD] = rope(kr * rinv * gk[:, LANE:]).astype(BF16)


def _inproj(x2d, pos2d, gmix, win, gql, wuq, gkvl, wukv, gq, gk, freq, sgn, *, batch, seq, tl):
    nl = seq // tl
    tok = lambda w: pl.BlockSpec((tl, w), lambda b, i: (b * nl + i, 0))
    full = lambda a: pl.BlockSpec(a.shape, lambda b, i: (0,) * a.ndim)
    n_tok = batch * seq
    return pl.pallas_call(
        _inproj_kernel,
        grid=(batch, nl),
        in_specs=[tok(D_MODEL), tok(1), full(gmix), full(win), full(gql), full(wuq), full(gkvl),
                  full(wukv), full(gq), full(gk), full(freq), full(sgn)],
        out_specs=[tok(N_HEADS * HEAD_PAD), tok(N_HEADS * HEAD_PAD), tok(ATTN_WIDTH),
                   pl.BlockSpec((tl // CHUNK, N_LANE_TILES, CHUNK, LANE), lambda b, i: (i, 0, b, 0))],
        out_shape=[jax.ShapeDtypeStruct((n_tok, N_HEADS * HEAD_PAD), BF16),
                   jax.ShapeDtypeStruct((n_tok, N_HEADS * HEAD_PAD), BF16),
                   jax.ShapeDtypeStruct((n_tok, ATTN_WIDTH), BF16),
                   jax.ShapeDtypeStruct((seq // CHUNK, N_LANE_TILES, batch * CHUNK, LANE), F32)],
        compiler_params=_params("arbitrary", "arbitrary"),
    )(x2d, pos2d, gmix, win, gql, wuq, gkvl, wukv, gq, gk, freq, sgn)


def _attn_kernel(q_ref, k_ref, v_ref, o_ref, *, tq):
    seq = q_ref.shape[0]
    for i in range(seq // tq):
        kv_len = (i + 1) * tq
        q = q_ref[i * tq:(i + 1) * tq, :]
        k = k_ref[:kv_len, :]
        s = lax.dot_general(q, k, (((1,), (1,)), ((), ())), preferred_element_type=F32)
        row = lax.broadcasted_iota(jnp.int32, s.shape, 0) + i * tq
        col = lax.broadcasted_iota(jnp.int32, s.shape, 1)
        s = jnp.where(col <= row, s, -jnp.inf)
        m = jnp.max(s, axis=-1, keepdims=True)
        p = jnp.exp(s - m)
        l = jnp.sum(p, axis=-1, keepdims=True)
        o = _dot(p.astype(BF16), v_ref[:kv_len, :])
        o_ref[i * tq:(i + 1) * tq, :] = (o / l).astype(o_ref.dtype)


def _attention(q, k, v, *, batch, seq, tq):
    return pl.pallas_call(
        functools.partial(_attn_kernel, tq=tq),
        grid=(batch, N_HEADS),
        in_specs=[pl.BlockSpec((seq, HEAD_PAD), lambda b, h: (b, h)),
                  pl.BlockSpec((seq, HEAD_PAD), lambda b, h: (b, h)),
                  pl.BlockSpec((seq, V_HEAD_DIM), lambda b, h: (b, h))],
        out_specs=pl.BlockSpec((seq, V_HEAD_DIM), lambda b, h: (b, h)),
        out_shape=jax.ShapeDtypeStruct((batch * seq, ATTN_WIDTH), BF16),
        compiler_params=_params("arbitrary", "arbitrary"),
    )(q, k, v)


def _block_transpose(tiles):
    lane = lax.broadcasted_iota(jnp.int32, tiles[0].shape, 1)
    for d in (4, 2, 1):
        hi = (lane & (SSM_GROUP * d)) != 0
        new = list(tiles)
        for i in range(GROUPS_PER_TILE):
            if i & d == 0:
                a, b = tiles[i], tiles[i + d]
                new[i] = jnp.where(hi, pltpu.roll(b, SSM_GROUP * d, 1), a)
                new[i + d] = jnp.where(hi, b, pltpu.roll(a, LANE - SSM_GROUP * d, 1))
        tiles = new
    return tiles


def _s5_kernel(u_ref, wst_ref, are_ref, aim_ref, toep_ref, vmat_ref, dskip_ref, wglu_ref, bglu_ref,
               gout_ref, o_ref, state_ref, x_ref, xs_ref, ys_ref, yg_ref, *, chunks, batch):
    rows_x = chunks * batch
    rows = rows_x * CHUNK

    @pl.when(pl.program_id(0) == 0)
    def _():
        state_ref[...] = jnp.zeros_like(state_ref)

    for kt in range(N_LANE_TILES):
        tiles = [u_ref[:, pl.ds(kt, 1), pl.ds(t, batch, stride=CHUNK), :].reshape(rows_x, LANE)
                 for t in range(CHUNK)]
        outs = _block_transpose(tiles)
        for g in range(GROUPS_PER_TILE):
            c0 = (kt * GROUPS_PER_TILE + g) * LANE
            x_ref[:, c0:c0 + LANE] = outs[g].astype(BF16)

    for q in range(N_PAIRS):
        cs = slice(q * PAIR_W, (q + 1) * PAIR_W)
        xs_ref[:, cs] = _dot(x_ref[:, cs], wst_ref[q])

    def step(c, carry):
        r0 = pl.multiple_of(c * batch, batch)
        for q in range(N_PAIRS):
            re = slice(q * PAIR_W, q * PAIR_W + LANE)
            im = slice(q * PAIR_W + LANE, (q + 1) * PAIR_W)
            al = slice(q * LANE, (q + 1) * LANE)
            s_re = state_ref[:, re]
            s_im = state_ref[:, im]
            a_re = are_ref[:, al]
            a_im = aim_ref[:, al]
            x_re = xs_ref[pl.ds(r0, batch), re]
            x_im = xs_ref[pl.ds(r0, batch), im]
            xs_ref[pl.ds(r0, batch), re] = s_re
            xs_ref[pl.ds(r0, batch), im] = s_im
            state_ref[:, re] = a_re * s_re - a_im * s_im + x_re
            state_ref[:, im] = a_re * s_im + a_im * s_re + x_im
        return carry

    lax.fori_loop(0, chunks, step, 0)

    for q in range(N_PAIRS):
        cs = slice(q * PAIR_W, (q + 1) * PAIR_W)
        xs_ref[:, cs] = (_dot(x_ref[:, cs], toep_ref[q])
                         + _dot(xs_ref[:, cs].astype(BF16), vmat_ref[q]))

    for kt in range(N_LANE_TILES):
        tiles = [xs_ref[:, (kt * GROUPS_PER_TILE + g) * LANE:(kt * GROUPS_PER_TILE + g + 1) * LANE]
                 for g in range(GROUPS_PER_TILE)]
        outs = _block_transpose(tiles)
        for t in range(CHUNK):
            ys_ref[:, pl.ds(kt, 1), pl.ds(t, batch, stride=CHUNK), :] = outs[t].reshape(chunks, 1, batch, LANE)

    for kt in range(N_LANE_TILES):
        ls = slice(kt * LANE, (kt + 1) * LANE)
        y_k = (ys_ref[:, kt].reshape(rows, LANE)
               + dskip_ref[:, ls] * u_ref[:, kt].reshape(rows, LANE))
        yg_ref[:, ls] = jax.nn.gelu(y_k, approximate=True)
    y = yg_ref[...]
    gate = _sigmoid(_dot(y.astype(BF16), wglu_ref[...]) + bglu_ref[...])
    o_ref[...] = _rms(y * gate, gout_ref[...]).reshape(chunks, batch, CHUNK, SSM_WIDTH)


def _s5(u_slab, wst, a_re, a_im, toep, vmat, dskip, wglu, bglu, gout, *, batch, seq, chunks):
    n_chunks = seq // CHUNK
    rows_x = chunks * batch
    full = lambda a: pl.BlockSpec(a.shape, lambda i: (0,) * a.ndim, pipeline_mode=pl.Buffered(1))
    return pl.pallas_call(
        functools.partial(_s5_kernel, chunks=chunks, batch=batch),
        grid=(n_chunks // chunks,),
        in_specs=[pl.BlockSpec((chunks, N_LANE_TILES, batch * CHUNK, LANE), lambda i: (i, 0, 0, 0)),
                  full(wst), full(a_re), full(a_im), full(toep), full(vmat), full(dskip), full(wglu),
                  full(bglu), full(gout)],
        out_specs=pl.BlockSpec((chunks, batch, CHUNK, SSM_WIDTH), lambda i: (i, 0, 0, 0)),
        out_shape=jax.ShapeDtypeStruct((n_chunks, batch, CHUNK, SSM_WIDTH), F32),
        scratch_shapes=[pltpu.VMEM((batch, N_PAIRS * PAIR_W), F32),
                        pltpu.VMEM((rows_x, N_PAIRS * PAIR_W), BF16),
                        pltpu.VMEM((rows_x, N_PAIRS * PAIR_W), F32),
                        pltpu.VMEM((chunks, N_LANE_TILES, batch * CHUNK, LANE), F32),
                        pltpu.VMEM((rows_x * CHUNK, SSM_WIDTH), F32)],
        compiler_params=_params("arbitrary"),
    )(u_slab, wst, a_re, a_im, toep, vmat, dskip, wglu, bglu, gout)


def _oproj_kernel(x_ref, oa_ref, os_ref, ga_ref, woa_ref, wos_ref, out_ref):
    oa = _rms(oa_ref[...].astype(F32), ga_ref[...]).astype(BF16)
    os_ = os_ref[...].reshape(x_ref.shape[0], SSM_WIDTH).astype(BF16)
    out_ref[...] = x_ref[...] + _dot(oa, woa_ref[...]) + _dot(os_, wos_ref[...])


def _oproj(x2d, o_attn, o_ssm, g_attn, wo_a, wo_s, *, batch, seq, tl):
    nl = seq // tl
    tok = lambda w: pl.BlockSpec((tl, w), lambda b, i: (b * nl + i, 0))
    full = lambda a: pl.BlockSpec(a.shape, lambda b, i: (0,) * a.ndim)
    return pl.pallas_call(
        _oproj_kernel,
        grid=(batch, nl),
        in_specs=[tok(D_MODEL), tok(ATTN_WIDTH),
                  pl.BlockSpec((tl // CHUNK, None, CHUNK, SSM_WIDTH), lambda b, i: (i, b, 0, 0)),
                  full(g_attn), full(wo_a), full(wo_s)],
        out_specs=tok(D_MODEL),
        out_shape=jax.ShapeDtypeStruct((batch * seq, D_MODEL), F32),
        compiler_params=_params("arbitrary", "arbitrary"),
    )(x2d, o_attn, o_ssm, g_attn, wo_a, wo_s)


def _ffn_kernel(x_ref, g_ref, wg_ref, wu_ref, wd_ref, out_ref, h_ref):
    j = pl.program_id(1)

    @pl.when(j == 0)
    def _():
        x = x_ref[...]
        h_ref[...] = _rms(x, g_ref[...]).astype(BF16)
        out_ref[...] = x

    h = h_ref[...]
    gate = _dot(h, wg_ref[...])
    up = _dot(h, wu_ref[...])
    act = (gate * _sigmoid(gate) * up).astype(BF16)
    out_ref[...] += _dot(act, wd_ref[...])


def _ffn(x2d, g, wg, wu, wd, *, tm, tf):
    n_tok = x2d.shape[0]
    return pl.pallas_call(
        _ffn_kernel,
        grid=(n_tok // tm, D_FF // tf),
        in_specs=[pl.BlockSpec((tm, D_MODEL), lambda i, j: (i, 0)),
                  pl.BlockSpec((1, D_MODEL), lambda i, j: (0, 0)),
                  pl.BlockSpec((D_MODEL, tf), lambda i, j: (0, j)),
                  pl.BlockSpec((D_MODEL, tf), lambda i, j: (0, j)),
                  pl.BlockSpec((tf, D_MODEL), lambda i, j: (j, 0))],
        out_specs=pl.BlockSpec((tm, D_MODEL), lambda i, j: (i, 0)),
        out_shape=jax.ShapeDtypeStruct((n_tok, D_MODEL), F32),
        scratch_shapes=[pltpu.VMEM((tm, D_MODEL), BF16)],
        compiler_params=_params("arbitrary", "arbitrary"),
    )(x2d, g, wg, wu, wd)


def _ple_kernel(x_ref, p_ref, g_ref, wpg_ref, wpp_ref, out_ref):
    x = x_ref[...]
    h = _rms(x, g_ref[...]).astype(BF16)
    gate = _sigmoid(_dot(h, wpg_ref[...]))
    out_ref[...] = x + gate * _dot(p_ref[...].astype(BF16), wpp_ref[...])


def _ple(x2d, p2d, g, wpg, wpp, *, tm):
    n_tok = x2d.shape[0]
    full = lambda a: pl.BlockSpec(a.shape, lambda i: (0,) * a.ndim)
    return pl.pallas_call(
        _ple_kernel,
        grid=(n_tok // tm,),
        in_specs=[pl.BlockSpec((tm, D_MODEL), lambda i: (i, 0)),
                  pl.BlockSpec((tm, PLE_DIM), lambda i: (i, 0)),
                  full(g), full(wpg), full(wpp)],
        out_specs=pl.BlockSpec((tm, D_MODEL), lambda i: (i, 0)),
        out_shape=jax.ShapeDtypeStruct((n_tok, D_MODEL), F32),
        compiler_params=_params("arbitrary"),
    )(x2d, p2d, g, wpg, wpp)


def _rope_tile(t):
    z = jnp.zeros(t.shape[:-1] + (ROPE_HALF,), t.dtype)
    return jnp.concatenate([t[..., :ROPE_HALF], z, t[..., ROPE_HALF:], z], axis=-1)


def _head_gain(g):
    return jnp.concatenate([g[:QK_NOPE_DIM], _rope_tile(g[QK_NOPE_DIM:])])[None, :].astype(F32)


def _s5_params(lam_re, lam_im, log_dt, b_re, b_im, c_re, c_im):
    G, P, H, T, NQ = SSM_GROUPS, SSM_STATE, SSM_GROUP, CHUNK, N_PAIRS
    lr = jnp.minimum(lam_re.astype(F32), -1e-4)
    li = lam_im.astype(F32)
    dt = jnp.exp(log_dt.astype(F32))[:, None]
    mag = jnp.exp(lr * dt)
    abar_re = mag * jnp.cos(li * dt)
    abar_im = mag * jnp.sin(li * dt)
    den = lr * lr + li * li
    num_re = abar_re - 1.0
    num_im = abar_im
    coef_re = (num_re * lr + num_im * li) / den
    coef_im = (num_im * lr - num_re * li) / den
    br = b_re.astype(F32)
    bim = b_im.astype(F32)
    bb_re = coef_re[..., None] * br - coef_im[..., None] * bim
    bb_im = coef_re[..., None] * bim + coef_im[..., None] * br
    pw_re, pw_im = [jnp.ones_like(abar_re)], [jnp.zeros_like(abar_im)]
    for _ in range(T):
        r, i = pw_re[-1], pw_im[-1]
        pw_re.append(r * abar_re - i * abar_im)
        pw_im.append(r * abar_im + i * abar_re)
    pw_re = jnp.stack(pw_re)
    pw_im = jnp.stack(pw_im)
    cr = c_re.astype(F32)
    ci = c_im.astype(F32)
    ca_re = cr[None] * pw_re[:, :, None, :] - ci[None] * pw_im[:, :, None, :]
    ca_im = cr[None] * pw_im[:, :, None, :] + ci[None] * pw_re[:, :, None, :]
    hp = lax.Precision.HIGHEST
    kj = (jnp.einsum('jghp,gpk->jghk', ca_re[:T], bb_re, precision=hp)
          - jnp.einsum('jghp,gpk->jghk', ca_im[:T], bb_im, precision=hp))
    tt = jnp.arange(T)
    lag = tt[None, :] - tt[:, None]
    kfull = jnp.where((lag >= 0)[:, :, None, None, None], kj[jnp.clip(lag, 0, T - 1)], 0.0)
    toep = kfull.transpose(2, 0, 4, 1, 3).reshape(G, T * H, T * H)
    rev_re = pw_re[:T][::-1]
    rev_im = pw_im[:T][::-1]
    w_re = rev_re[..., None] * bb_re[None] - rev_im[..., None] * bb_im[None]
    w_im = rev_re[..., None] * bb_im[None] + rev_im[..., None] * bb_re[None]
    w_re = w_re.transpose(1, 0, 3, 2).reshape(G, T * H, P)
    w_im = w_im.transpose(1, 0, 3, 2).reshape(G, T * H, P)
    v_re = ca_re[1:].transpose(1, 3, 0, 2).reshape(G, P, T * H)
    v_im = (-ca_im[1:]).transpose(1, 3, 0, 2).reshape(G, P, T * H)
    eye = jnp.eye(2, dtype=F32)
    toep_p = jnp.einsum('qaij,ab->qaibj', toep.reshape(NQ, 2, T * H, T * H), eye).reshape(NQ, PAIR_W, PAIR_W)
    w_ri = jnp.stack([w_re, w_im], axis=1).reshape(NQ, 2, 2, T * H, P)
    wst_p = jnp.einsum('qacip,ab->qaicbp', w_ri, eye).reshape(NQ, PAIR_W, PAIR_W)
    v_ri = jnp.stack([v_re, v_im], axis=1).reshape(NQ, 2, 2, P, T * H)
    vmat_p = jnp.einsum('qacpj,ab->qcapbj', v_ri, eye).reshape(NQ, PAIR_W, PAIR_W)
    a_re = jnp.broadcast_to(pw_re[T].reshape(1, G * P), (SUBLANE, G * P))
    a_im = jnp.broadcast_to(pw_im[T].reshape(1, G * P), (SUBLANE, G * P))
    return wst_p.astype(BF16), a_re, a_im, toep_p.astype(BF16), vmat_p.astype(BF16)


def kernel(x, p, positions, g_mix_norm, w_in, g_q_lora, w_uq, g_kv_lora, w_ukv, g_q_head, g_k_head,
           lam_re, lam_im, log_dt, b_re, b_im, c_re, c_im, d_skip, w_glu, b_glu, g_out_attn,
           g_out_ssm, w_o, g_ffn_norm, w_gate, w_up, w_down, g_ple_norm, w_ple_gate, w_ple_proj):
    batch, seq, _ = x.shape
    assert batch == SUBLANE, "the S5 chunk recurrence keeps one batch row per sublane"
    depth = w_in.shape[0]
    n_tok = batch * seq
    row = lambda g: g[None, :].astype(F32)

    inv_freq = 1.0 / (ROPE_THETA ** (jnp.arange(0, QK_ROPE_DIM, 2, dtype=F32) / QK_ROPE_DIM))
    freq = _rope_tile(jnp.concatenate([inv_freq, inv_freq]))[None, :]
    ones = jnp.ones((ROPE_HALF,), F32)
    sgn = _rope_tile(jnp.concatenate([-ones, ones]))[None, :]
    pos2d = positions.reshape(n_tok, 1)

    x2d = x.reshape(n_tok, D_MODEL)
    for i in range(depth):
        o1, o2, o3 = Q_LORA, Q_LORA + KV_LORA, Q_LORA + KV_LORA + QK_ROPE_DIM
        wi = w_in[i]
        win = jnp.concatenate([wi[:, :o2], _rope_tile(wi[:, o2:o3]), wi[:, o3:]], axis=1).astype(BF16)
        wq = w_uq[i].reshape(Q_LORA, N_HEADS, QK_HEAD_DIM)
        wuq = jnp.concatenate([wq[..., :QK_NOPE_DIM], _rope_tile(wq[..., QK_NOPE_DIM:])], axis=-1)
        wuq = wuq.reshape(Q_LORA, N_HEADS * HEAD_PAD).astype(BF16)
        wkv = w_ukv[i].reshape(KV_LORA, N_HEADS, QK_NOPE_DIM + V_HEAD_DIM)
        wukv = jnp.concatenate([wkv[..., :QK_NOPE_DIM].reshape(KV_LORA, -1),
                                wkv[..., QK_NOPE_DIM:].reshape(KV_LORA, -1)], axis=1).astype(BF16)

        q, k, v, u_slab = _inproj(x2d, pos2d, row(g_mix_norm[i]), win, row(g_q_lora[i]), wuq,
                                  row(g_kv_lora[i]), wukv, _head_gain(g_q_head[i]),
                                  _head_gain(g_k_head[i]), freq, sgn, batch=batch, seq=seq, tl=512)
        o_attn = _attention(q, k, v, batch=batch, seq=seq, tq=256)

        wst, a_re, a_im, toep, vmat = _s5_params(lam_re[i], lam_im[i], log_dt[i], b_re[i], b_im[i],
                                                 c_re[i], c_im[i])
        o_ssm = _s5(u_slab, wst, a_re, a_im, toep, vmat, d_skip[i].reshape(1, SSM_WIDTH).astype(F32),
                    w_glu[i].astype(BF16), row(b_glu[i]), row(g_out_ssm[i]),
                    batch=batch, seq=seq, chunks=16)

        wo = w_o[i].astype(BF16)
        x2d = _oproj(x2d, o_attn, o_ssm, row(g_out_attn[i]), wo[:ATTN_WIDTH], wo[ATTN_WIDTH:],
                     batch=batch, seq=seq, tl=512)
        x2d = _ffn(x2d, row(g_ffn_norm[i]), w_gate[i].astype(BF16), w_up[i].astype(BF16),
                   w_down[i].astype(BF16), tm=512, tf=512)
        x2d = _ple(x2d, p[i].reshape(n_tok, PLE_DIM), row(g_ple_norm[i]),
                   w_ple_gate[i].astype(BF16), w_ple_proj[i].astype(BF16), tm=512)
    return x2d.reshape(batch, seq, D_MODEL)
```

```python
import functools

import jax
import jax.numpy as jnp
from jax import lax
from jax.experimental import pallas as pl
from jax.experimental.pallas import tpu as pltpu

D_MODEL = 2048
PLE_DIM = 256
N_HEADS = 8
QK_NOPE_DIM = 128
QK_ROPE_DIM = 64
V_HEAD_DIM = 128
QK_HEAD_DIM = QK_NOPE_DIM + QK_ROPE_DIM
Q_LORA = 512
KV_LORA = 256
ATTN_WIDTH = N_HEADS * V_HEAD_DIM
ROPE_THETA = 10000.0
SSM_WIDTH = 1024
SSM_GROUP = 16
SSM_GROUPS = SSM_WIDTH // SSM_GROUP
SSM_STATE = 64
D_FF = 5632
EPS = 1e-6

LANE = 128
SUBLANE = 8
HEAD_PAD = 2 * LANE
ROPE_HALF = QK_ROPE_DIM // 2
KR_PAD = LANE
N_IN_PAD = Q_LORA + KV_LORA + KR_PAD + SSM_WIDTH
CHUNK = SUBLANE
N_LANE_TILES = SSM_WIDTH // LANE
GROUPS_PER_TILE = LANE // SSM_GROUP
PAIR_W = 2 * LANE
N_PAIRS = SSM_GROUPS // 2
VMEM_LIMIT = 56 * 1024 * 1024
LOG2_E = 1.4426950408889634
INPROJ_SUBTILES = 2
QK_AHEAD = 3

BF16 = jnp.bfloat16
F32 = jnp.float32


def _rms(t, g, width=None):
    n = t.shape[-1] if width is None else width
    ss = jnp.sum(t * t, axis=-1, keepdims=True) * (1.0 / n)
    return t * lax.rsqrt(ss + EPS) * g


def _dot(a, b):
    return jnp.dot(a, b, preferred_element_type=F32)


def _sigmoid(t):
    return 1.0 / (1.0 + jnp.exp(-t))


def _params(*sem):
    return pltpu.CompilerParams(dimension_semantics=sem, vmem_limit_bytes=VMEM_LIMIT)


def _inproj_kernel(x_ref, pos_ref, gmix_ref, win_ref, gql_ref, wuq_ref, gkvl_ref, wukv_ref,
                   gq_ref, gk_ref, freq_ref, sgn_ref,
                   q_ref, k_ref, v_ref, u_ref):
    sub = x_ref.shape[0] // INPROJ_SUBTILES
    o1, o2, o3 = Q_LORA, Q_LORA + KV_LORA, Q_LORA + KV_LORA + KR_PAD
    gq = gq_ref[...]
    gk = gk_ref[...]
    scale = QK_HEAD_DIM ** -0.5 * LOG2_E

    def project(s):
        rs = slice(s * sub, (s + 1) * sub)
        h = _rms(x_ref[rs, :], gmix_ref[...]).astype(BF16)
        z = _dot(h, win_ref[...])
        cs = slice(s * sub // CHUNK, (s + 1) * sub // CHUNK)
        for kt in range(N_LANE_TILES):
            u_ref[cs, kt, :, :] = z[:, o3 + kt * LANE:o3 + (kt + 1) * LANE].reshape(sub // CHUNK, CHUNK, LANE)
        c_q = _rms(z[:, :o1], gql_ref[...]).astype(BF16)
        q = _dot(c_q, wuq_ref[...])
        c_kv = _rms(z[:, o1:o2], gkvl_ref[...]).astype(BF16)
        kv = _dot(c_kv, wukv_ref[...])
        return q, kv, z[:, o2:o3]

    def finish_heads(s, q, kv, kr):
        rs = slice(s * sub, (s + 1) * sub)
        v_ref[rs, :] = kv[:, ATTN_WIDTH:].astype(BF16)
        ang = pos_ref[rs, :].astype(F32) * freq_ref[...]
        cos_t = jnp.cos(ang)
        sin_t = jnp.sin(ang) * sgn_ref[...]

        def rope(t):
            return t * cos_t + pltpu.roll(t, LANE // 2, 1) * sin_t

        kr_ss = jnp.sum(kr * kr, axis=-1, keepdims=True)
        for hd in range(N_HEADS):
            qh = q[:, hd * HEAD_PAD:(hd + 1) * HEAD_PAD]
            qn = _rms(qh, gq, width=QK_HEAD_DIM) * scale
            q_ref[rs, hd * HEAD_PAD:hd * HEAD_PAD + LANE] = qn[:, :LANE].astype(BF16)
            q_ref[rs, hd * HEAD_PAD + LANE:(hd + 1) * HEAD_PAD] = rope(qn[:, LANE:]).astype(BF16)
            kn = kv[:, hd * QK_NOPE_DIM:(hd + 1) * QK_NOPE_DIM]
            ss = (jnp.sum(kn * kn, axis=-1, keepdims=True) + kr_ss) * (1.0 / QK_HEAD_DIM)
            rinv = lax.rsqrt(ss + EPS)
            k_ref[rs, hd * HEAD_PAD:hd * HEAD_PAD + LANE] = (kn * rinv * gk[:, :LANE]).astype(BF16)
            k_ref[rs, hd * HEAD_PAD + LANE:(hd + 1) * HEAD_PAD] = rope(kr * rinv * gk[:, LANE:]).astype(BF16)

    for s in range(INPROJ_SUBTILES):
        finish_heads(s, *project(s))


def _inproj(x2d, pos2d, gmix, win, gql, wuq, gkvl, wukv, gq, gk, freq, sgn, *, batch, seq, tl):
    nl = seq // tl
    tok = lambda w: pl.BlockSpec((tl, w), lambda b, i: (b * nl + i, 0))
    full = lambda a: pl.BlockSpec(a.shape, lambda b, i: (0,) * a.ndim)
    n_tok = batch * seq
    return pl.pallas_call(
        _inproj_kernel,
        grid=(batch, nl),
        in_specs=[tok(D_MODEL), tok(1), full(gmix), full(win), full(gql), full(wuq), full(gkvl),
                  full(wukv), full(gq), full(gk), full(freq), full(sgn)],
        out_specs=[tok(N_HEADS * HEAD_PAD), tok(N_HEADS * HEAD_PAD), tok(ATTN_WIDTH),
                   pl.BlockSpec((tl // CHUNK, N_LANE_TILES, CHUNK, LANE), lambda b, i: (i, 0, b, 0))],
        out_shape=[jax.ShapeDtypeStruct((n_tok, N_HEADS * HEAD_PAD), BF16),
                   jax.ShapeDtypeStruct((n_tok, N_HEADS * HEAD_PAD), BF16),
                   jax.ShapeDtypeStruct((n_tok, ATTN_WIDTH), BF16),
                   jax.ShapeDtypeStruct((seq // CHUNK, N_LANE_TILES, batch * CHUNK, LANE), F32)],
        compiler_params=_params("arbitrary", "arbitrary"),
    )(x2d, pos2d, gmix, win, gql, wuq, gkvl, wukv, gq, gk, freq, sgn)


def _attn_kernel(q_ref, k_ref, v_ref, o_ref, *, tq):
    seq = q_ref.shape[0]
    n = seq // tq
    vt = v_ref[...].T
    diag_mask = (lax.broadcasted_iota(jnp.int32, (tq, tq), 0)
                 <= lax.broadcasted_iota(jnp.int32, (tq, tq), 1))

    def scores_t(i):
        return lax.dot_general(k_ref[:(i + 1) * tq, :], q_ref[i * tq:(i + 1) * tq, :],
                               (((1,), (1,)), ((), ())), preferred_element_type=F32)

    sts = [scores_t(j) for j in range(min(QK_AHEAD, n))]
    for i in range(n):
        kv_len = (i + 1) * tq
        st = sts[i]
        if i + QK_AHEAD < n:
            sts.append(scores_t(i + QK_AHEAD))
        sd = jnp.where(diag_mask, st[kv_len - tq:], -jnp.inf)
        m = jnp.max(sd, axis=0, keepdims=True)
        if i:
            m = jnp.maximum(m, jnp.max(st[:kv_len - tq], axis=0, keepdims=True))
        pd = jnp.exp2(sd - m)
        l = jnp.sum(pd, axis=0, keepdims=True)
        if i:
            pt = jnp.exp2(st[:kv_len - tq] - m)
            l = l + jnp.sum(pt, axis=0, keepdims=True)
            p = jnp.concatenate([pt.astype(BF16), pd.astype(BF16)], axis=0)
        else:
            p = pd.astype(BF16)
        ot = _dot(vt[:, :kv_len], p)
        o_ref[i * tq:(i + 1) * tq, :] = (ot / l).T.astype(o_ref.dtype)


def _attention(q, k, v, *, batch, seq, tq):
    return pl.pallas_call(
        functools.partial(_attn_kernel, tq=tq),
        grid=(batch, N_HEADS),
        in_specs=[pl.BlockSpec((seq, HEAD_PAD), lambda b, h: (b, h)),
                  pl.BlockSpec((seq, HEAD_PAD), lambda b, h: (b, h)),
                  pl.BlockSpec((seq, V_HEAD_DIM), lambda b, h: (b, h))],
        out_specs=pl.BlockSpec((seq, V_HEAD_DIM), lambda b, h: (b, h)),
        out_shape=jax.ShapeDtypeStruct((batch * seq, ATTN_WIDTH), BF16),
        compiler_params=_params("arbitrary", "arbitrary"),
    )(q, k, v)


def _block_transpose(tiles):
    lane = lax.broadcasted_iota(jnp.int32, tiles[0].shape, 1)
    for d in (4, 2, 1):
        hi = (lane & (SSM_GROUP * d)) != 0
        new = list(tiles)
        for i in range(GROUPS_PER_TILE):
            if i & d == 0:
                a, b = tiles[i], tiles[i + d]
                new[i] = jnp.where(hi, pltpu.roll(b, SSM_GROUP * d, 1), a)
                new[i + d] = jnp.where(hi, b, pltpu.roll(a, LANE - SSM_GROUP * d, 1))
        tiles = new
    return tiles


def _s5_kernel(u_ref, wst_ref, are_ref, aim_ref, toep_ref, vmat_ref, dskip_ref, wglu_ref, bglu_ref,
               gout_ref, o_ref, state_ref, x_ref, xs_ref, ys_ref, yg_ref, *, chunks, batch):
    rows_x = chunks * batch
    rows = rows_x * CHUNK

    @pl.when(pl.program_id(0) == 0)
    def _():
        state_ref[...] = jnp.zeros_like(state_ref)

    for kt in range(N_LANE_TILES):
        tiles = [u_ref[:, pl.ds(kt, 1), pl.ds(t, batch, stride=CHUNK), :].reshape(rows_x, LANE)
                 for t in range(CHUNK)]
        outs = _block_transpose(tiles)
        for g in range(GROUPS_PER_TILE):
            c0 = (kt * GROUPS_PER_TILE + g) * LANE
            x_ref[:, c0:c0 + LANE] = outs[g].astype(BF16)

    for q in range(N_PAIRS):
        cs = slice(q * PAIR_W, (q + 1) * PAIR_W)
        xs_ref[:, cs] = _dot(x_ref[:, cs], wst_ref[q])

    def step(c, carry):
        r0 = pl.multiple_of(c * batch, batch)
        for q in range(N_PAIRS):
            re = slice(q * PAIR_W, q * PAIR_W + LANE)
            im = slice(q * PAIR_W + LANE, (q + 1) * PAIR_W)
            al = slice(q * LANE, (q + 1) * LANE)
            s_re = state_ref[:, re]
            s_im = state_ref[:, im]
            a_re = are_ref[:, al]
            a_im = aim_ref[:, al]
            x_re = xs_ref[pl.ds(r0, batch), re]
            x_im = xs_ref[pl.ds(r0, batch), im]
            xs_ref[pl.ds(r0, batch), re] = s_re
            xs_ref[pl.ds(r0, batch), im] = s_im
            state_ref[:, re] = a_re * s_re - a_im * s_im + x_re
            state_ref[:, im] = a_re * s_im + a_im * s_re + x_im
        return carry

    lax.fori_loop(0, chunks, step, 0)

    for q in range(N_PAIRS):
        cs = slice(q * PAIR_W, (q + 1) * PAIR_W)
        xs_ref[:, cs] = (_dot(x_ref[:, cs], toep_ref[q])
                         + _dot(xs_ref[:, cs].astype(BF16), vmat_ref[q]))

    for kt in range(N_LANE_TILES):
        tiles = [xs_ref[:, (kt * GROUPS_PER_TILE + g) * LANE:(kt * GROUPS_PER_TILE + g + 1) * LANE]
                 for g in range(GROUPS_PER_TILE)]
        outs = _block_transpose(tiles)
        for t in range(CHUNK):
            ys_ref[:, pl.ds(kt, 1), pl.ds(t, batch, stride=CHUNK), :] = outs[t].reshape(chunks, 1, batch, LANE)

    for kt in range(N_LANE_TILES):
        ls = slice(kt * LANE, (kt + 1) * LANE)
        y_k = (ys_ref[:, kt].reshape(rows, LANE)
               + dskip_ref[:, ls] * u_ref[:, kt].reshape(rows, LANE))
        yg_ref[:, ls] = jax.nn.gelu(y_k, approximate=True)
    y = yg_ref[...]
    gate = _sigmoid(_dot(y.astype(BF16), wglu_ref[...]) + bglu_ref[...])
    o_ref[...] = _rms(y * gate, gout_ref[...]).reshape(chunks, batch, CHUNK, SSM_WIDTH)


def _s5(u_slab, wst, a_re, a_im, toep, vmat, dskip, wglu, bglu, gout, *, batch, seq, chunks):
    n_chunks = seq // CHUNK
    rows_x = chunks * batch
    full = lambda a: pl.BlockSpec(a.shape, lambda i: (0,) * a.ndim, pipeline_mode=pl.Buffered(1))
    return pl.pallas_call(
        functools.partial(_s5_kernel, chunks=chunks, batch=batch),
        grid=(n_chunks // chunks,),
        in_specs=[pl.BlockSpec((chunks, N_LANE_TILES, batch * CHUNK, LANE), lambda i: (i, 0, 0, 0)),
                  full(wst), full(a_re), full(a_im), full(toep), full(vmat), full(dskip), full(wglu),
                  full(bglu), full(gout)],
        out_specs=pl.BlockSpec((chunks, batch, CHUNK, SSM_WIDTH), lambda i: (i, 0, 0, 0)),
        out_shape=jax.ShapeDtypeStruct((n_chunks, batch, CHUNK, SSM_WIDTH), F32),
        scratch_shapes=[pltpu.VMEM((batch, N_PAIRS * PAIR_W), F32),
                        pltpu.VMEM((rows_x, N_PAIRS * PAIR_W), BF16),
                        pltpu.VMEM((rows_x, N_PAIRS * PAIR_W), F32),
                        pltpu.VMEM((chunks, N_LANE_TILES, batch * CHUNK, LANE), F32),
                        pltpu.VMEM((rows_x * CHUNK, SSM_WIDTH), F32)],
        compiler_params=_params("arbitrary"),
    )(u_slab, wst, a_re, a_im, toep, vmat, dskip, wglu, bglu, gout)


def _oproj_kernel(x_ref, oa_ref, os_ref, ga_ref, woa_ref, wos_ref, out_ref):
    oa = _rms(oa_ref[...].astype(F32), ga_ref[...]).astype(BF16)
    os_ = os_ref[...].reshape(x_ref.shape[0], SSM_WIDTH).astype(BF16)
    out_ref[...] = x_ref[...] + _dot(oa, woa_ref[...]) + _dot(os_, wos_ref[...])


def _oproj(x2d, o_attn, o_ssm, g_attn, wo_a, wo_s, *, batch, seq, tl):
    nl = seq // tl
    tok = lambda w: pl.BlockSpec((tl, w), lambda b, i: (b * nl + i, 0))
    full = lambda a: pl.BlockSpec(a.shape, lambda b, i: (0,) * a.ndim)
    return pl.pallas_call(
        _oproj_kernel,
        grid=(batch, nl),
        in_specs=[tok(D_MODEL), tok(ATTN_WIDTH),
                  pl.BlockSpec((tl // CHUNK, None, CHUNK, SSM_WIDTH), lambda b, i: (i, b, 0, 0)),
                  full(g_attn), full(wo_a), full(wo_s)],
        out_specs=tok(D_MODEL),
        out_shape=jax.ShapeDtypeStruct((batch * seq, D_MODEL), F32),
        compiler_params=_params("arbitrary", "arbitrary"),
    )(x2d, o_attn, o_ssm, g_attn, wo_a, wo_s)


def _ffn_kernel(x_ref, g_ref, wg_ref, wu_ref, wd_ref, out_ref, h_ref):
    j = pl.program_id(1)

    @pl.when(j == 0)
    def _():
        x = x_ref[...]
        h_ref[...] = _rms(x, g_ref[...]).astype(BF16)
        out_ref[...] = x

    h = h_ref[...]
    gate = _dot(h, wg_ref[...])
    up = _dot(h, wu_ref[...])
    act = (gate * _sigmoid(gate) * up).astype(BF16)
    out_ref[...] += _dot(act, wd_ref[...])


def _ffn(x2d, g, wg, wu, wd, *, tm, tf):
    n_tok = x2d.shape[0]
    return pl.pallas_call(
        _ffn_kernel,
        grid=(n_tok // tm, D_FF // tf),
        in_specs=[pl.BlockSpec((tm, D_MODEL), lambda i, j: (i, 0)),
                  pl.BlockSpec((1, D_MODEL), lambda i, j: (0, 0)),
                  pl.BlockSpec((D_MODEL, tf), lambda i, j: (0, j)),
                  pl.BlockSpec((D_MODEL, tf), lambda i, j: (0, j)),
                  pl.BlockSpec((tf, D_MODEL), lambda i, j: (j, 0))],
        out_specs=pl.BlockSpec((tm, D_MODEL), lambda i, j: (i, 0)),
        out_shape=jax.ShapeDtypeStruct((n_tok, D_MODEL), F32),
        scratch_shapes=[pltpu.VMEM((tm, D_MODEL), BF16)],
        compiler_params=_params("arbitrary", "arbitrary"),
    )(x2d, g, wg, wu, wd)


def _ple_kernel(x_ref, p_ref, g_ref, wpg_ref, wpp_ref, out_ref):
    x = x_ref[...]
    h = _rms(x, g_ref[...]).astype(BF16)
    gate = _sigmoid(_dot(h, wpg_ref[...]))
    out_ref[...] = x + gate * _dot(p_ref[...].astype(BF16), wpp_ref[...])


def _ple(x2d, p2d, g, wpg, wpp, *, tm):
    n_tok = x2d.shape[0]
    full = lambda a: pl.BlockSpec(a.shape, lambda i: (0,) * a.ndim)
    return pl.pallas_call(
        _ple_kernel,
        grid=(n_tok // tm,),
        in_specs=[pl.BlockSpec((tm, D_MODEL), lambda i: (i, 0)),
                  pl.BlockSpec((tm, PLE_DIM), lambda i: (i, 0)),
                  full(g), full(wpg), full(wpp)],
        out_specs=pl.BlockSpec((tm, D_MODEL), lambda i: (i, 0)),
        out_shape=jax.ShapeDtypeStruct((n_tok, D_MODEL), F32),
        compiler_params=_params("arbitrary"),
    )(x2d, p2d, g, wpg, wpp)


def _rope_tile(t):
    z = jnp.zeros(t.shape[:-1] + (ROPE_HALF,), t.dtype)
    return jnp.concatenate([t[..., :ROPE_HALF], z, t[..., ROPE_HALF:], z], axis=-1)


def _head_gain(g):
    return jnp.concatenate([g[:QK_NOPE_DIM], _rope_tile(g[QK_NOPE_DIM:])])[None, :].astype(F32)


def _s5_params(lam_re, lam_im, log_dt, b_re, b_im, c_re, c_im):
    G, P, H, T, NQ = SSM_GROUPS, SSM_STATE, SSM_GROUP, CHUNK, N_PAIRS
    TH = T * H
    lr = jnp.minimum(lam_re.astype(F32), -1e-4)
    li = lam_im.astype(F32)
    dt = jnp.exp(log_dt.astype(F32))[:, None]
    mag = jnp.exp(lr * dt)
    abar_re = mag * jnp.cos(li * dt)
    abar_im = mag * jnp.sin(li * dt)
    den = lr * lr + li * li
    num_re = abar_re - 1.0
    num_im = abar_im
    coef_re = ((num_re * lr + num_im * li) / den)[:, None, :]
    coef_im = ((num_im * lr - num_re * li) / den)[:, None, :]
    br = b_re.astype(F32).transpose(0, 2, 1)
    bim = b_im.astype(F32).transpose(0, 2, 1)
    bb_re = coef_re * br - coef_im * bim
    bb_im = coef_re * bim + coef_im * br
    pw_re, pw_im = [jnp.ones_like(abar_re)], [jnp.zeros_like(abar_im)]
    for _ in range(T):
        r, i = pw_re[-1], pw_im[-1]
        pw_re.append(r * abar_re - i * abar_im)
        pw_im.append(r * abar_im + i * abar_re)
    pw_re = jnp.stack(pw_re, axis=1)[:, :, None, :]
    pw_im = jnp.stack(pw_im, axis=1)[:, :, None, :]
    cr = c_re.astype(F32)[:, None]
    ci = c_im.astype(F32)[:, None]
    ca_re = cr * pw_re - ci * pw_im
    ca_im = cr * pw_im + ci * pw_re
    cat = lambda re, im: jnp.concatenate([re.reshape(G, -1, P), im.reshape(G, -1, P)], axis=-1)
    m1 = jnp.einsum('gik,gjk->gij', cat(bb_re, bb_im), cat(ca_re[:, :T], -ca_im[:, :T]),
                    precision=lax.Precision.HIGHEST)
    toep = jnp.stack([jnp.pad(m1[:, :, :TH - H * t], ((0, 0), (0, 0), (H * t, 0))) for t in range(T)],
                     axis=1).reshape(G, TH, TH)
    rev_re = pw_re[:, :T][:, ::-1]
    rev_im = pw_im[:, :T][:, ::-1]
    w_re = (rev_re * bb_re[:, None] - rev_im * bb_im[:, None]).reshape(NQ, 2, TH, P)
    w_im = (rev_re * bb_im[:, None] + rev_im * bb_re[:, None]).reshape(NQ, 2, TH, P)
    zw = jnp.zeros((NQ, TH, P), F32)
    wst_p = jnp.concatenate([
        jnp.concatenate([w_re[:, 0], zw, w_im[:, 0], zw], axis=-1),
        jnp.concatenate([zw, w_re[:, 1], zw, w_im[:, 1]], axis=-1)], axis=1)
    tp = toep.reshape(NQ, 2, TH, TH)
    zt = jnp.zeros((NQ, TH, TH), F32)
    toep_p = jnp.concatenate([jnp.concatenate([tp[:, 0], zt], axis=-1),
                              jnp.concatenate([zt, tp[:, 1]], axis=-1)], axis=1)
    v_t = cat(ca_re[:, 1:], -ca_im[:, 1:]).transpose(0, 2, 1).reshape(NQ, 2, 2, P, TH)
    zv = jnp.zeros((NQ, P, TH), F32)
    vmat_p = jnp.concatenate([
        jnp.concatenate([v_t[:, 0, 0], zv], axis=-1), jnp.concatenate([zv, v_t[:, 1, 0]], axis=-1),
        jnp.concatenate([v_t[:, 0, 1], zv], axis=-1), jnp.concatenate([zv, v_t[:, 1, 1]], axis=-1)], axis=1)
    a_re = jnp.broadcast_to(pw_re[:, T].reshape(1, G * P), (SUBLANE, G * P))
    a_im = jnp.broadcast_to(pw_im[:, T].reshape(1, G * P), (SUBLANE, G * P))
    return wst_p.astype(BF16), a_re, a_im, toep_p.astype(BF16), vmat_p.astype(BF16)


def kernel(x, p, positions, g_mix_norm, w_in, g_q_lora, w_uq, g_kv_lora, w_ukv, g_q_head, g_k_head,
           lam_re, lam_im, log_dt, b_re, b_im, c_re, c_im, d_skip, w_glu, b_glu, g_out_attn,
           g_out_ssm, w_o, g_ffn_norm, w_gate, w_up, w_down, g_ple_norm, w_ple_gate, w_ple_proj):
    batch, seq, _ = x.shape
    assert batch == SUBLANE, "the S5 chunk recurrence keeps one batch row per sublane"
    depth = w_in.shape[0]
    n_tok = batch * seq
    row = lambda g: g[None, :].astype(F32)

    inv_freq = 1.0 / (ROPE_THETA ** (jnp.arange(0, QK_ROPE_DIM, 2, dtype=F32) / QK_ROPE_DIM))
    freq = _rope_tile(jnp.concatenate([inv_freq, inv_freq]))[None, :]
    ones = jnp.ones((ROPE_HALF,), F32)
    sgn = _rope_tile(jnp.concatenate([-ones, ones]))[None, :]
    pos2d = positions.reshape(n_tok, 1)

    x2d = x.reshape(n_tok, D_MODEL)
    for i in range(depth):
        o1, o2, o3 = Q_LORA, Q_LORA + KV_LORA, Q_LORA + KV_LORA + QK_ROPE_DIM
        wi = w_in[i]
        win = jnp.concatenate([wi[:, :o2], _rope_tile(wi[:, o2:o3]), wi[:, o3:]], axis=1).astype(BF16)
        wq = w_uq[i].reshape(Q_LORA, N_HEADS, QK_HEAD_DIM)
        wuq = jnp.concatenate([wq[..., :QK_NOPE_DIM], _rope_tile(wq[..., QK_NOPE_DIM:])], axis=-1)
        wuq = wuq.reshape(Q_LORA, N_HEADS * HEAD_PAD).astype(BF16)
        wkv = w_ukv[i].reshape(KV_LORA, N_HEADS, QK_NOPE_DIM + V_HEAD_DIM)
        wukv = jnp.concatenate([wkv[..., :QK_NOPE_DIM].reshape(KV_LORA, -1),
                                wkv[..., QK_NOPE_DIM:].reshape(KV_LORA, -1)], axis=1).astype(BF16)

        q, k, v, u_slab = _inproj(x2d, pos2d, row(g_mix_norm[i]), win, row(g_q_lora[i]), wuq,
                                  row(g_kv_lora[i]), wukv, _head_gain(g_q_head[i]),
                                  _head_gain(g_k_head[i]), freq, sgn, batch=batch, seq=seq, tl=512)
        o_attn = _attention(q, k, v, batch=batch, seq=seq, tq=256)

        wst, a_re, a_im, toep, vmat = _s5_params(lam_re[i], lam_im[i], log_dt[i], b_re[i], b_im[i],
                                                 c_re[i], c_im[i])
        o_ssm = _s5(u_slab, wst, a_re, a_im, toep, vmat, d_skip[i].reshape(1, SSM_WIDTH).astype(F32),
                    w_glu[i].astype(BF16), row(b_glu[i]), row(g_out_ssm[i]),
                    batch=batch, seq=seq, chunks=16)

        wo = w_o[i].astype(BF16)
        x2d = _oproj(x2d, o_attn, o_ssm, row(g_out_attn[i]), wo[:ATTN_WIDTH], wo[ATTN_WIDTH:],
                     batch=batch, seq=seq, tl=512)
        x2d = _ffn(x2d, row(g_ffn_norm[i]), w_gate[i].astype(BF16), w_up[i].astype(BF16),
                   w_down[i].astype(BF16), tm=512, tf=512)
        x2d = _ple(x2d, p[i].reshape(n_tok, PLE_DIM), row(g_ple_norm[i]),
                   w_ple_gate[i].astype(BF16), w_ple_proj[i].astype(BF16), tm=512)
    return x2d.reshape(batch, seq, D_MODEL)
```

```python
import functools

import jax
import jax.numpy as jnp
from jax import lax
from jax.experimental import pallas as pl
from jax.experimental.pallas import tpu as pltpu

D_MODEL = 2048
PLE_DIM = 256
N_HEADS = 8
QK_NOPE_DIM = 128
QK_ROPE_DIM = 64
V_HEAD_DIM = 128
QK_HEAD_DIM = QK_NOPE_DIM + QK_ROPE_DIM
Q_LORA = 512
KV_LORA = 256
ATTN_WIDTH = N_HEADS * V_HEAD_DIM
ROPE_THETA = 10000.0
SSM_WIDTH = 1024
SSM_GROUP = 16
SSM_GROUPS = SSM_WIDTH // SSM_GROUP
SSM_STATE = 64
D_FF = 5632
EPS = 1e-6

LANE = 128
SUBLANE = 8
HEAD_PAD = 2 * LANE
ROPE_HALF = QK_ROPE_DIM // 2
KR_PAD = LANE
N_IN_PAD = Q_LORA + KV_LORA + KR_PAD + SSM_WIDTH
CHUNK = SUBLANE
N_LANE_TILES = SSM_WIDTH // LANE
GROUPS_PER_TILE = LANE // SSM_GROUP
PAIR_W = 2 * LANE
N_PAIRS = SSM_GROUPS // 2
VMEM_LIMIT = 56 * 1024 * 1024
LOG2_E = 1.4426950408889634
FFN_SUBTILES = 2
INPROJ_SUBTILES = 2
PLE_SUBTILES = 2
OPROJ_SUBTILES = 4
QK_AHEAD = 4

BF16 = jnp.bfloat16
F32 = jnp.float32


def _rms(t, g, width=None):
    n = t.shape[-1] if width is None else width
    ss = jnp.sum(t * t, axis=-1, keepdims=True) * (1.0 / n)
    return t * lax.rsqrt(ss + EPS) * g


def _dot(a, b):
    return jnp.dot(a, b, preferred_element_type=F32)


def _sigmoid(t):
    return 1.0 / (1.0 + jnp.exp(-t))


def _params(*sem):
    return pltpu.CompilerParams(dimension_semantics=sem, vmem_limit_bytes=VMEM_LIMIT)


def _inproj_kernel(x_ref, pos_ref, gmix_ref, win_ref, gql_ref, wuq_ref, gkvl_ref, wukv_ref,
                   gq_ref, gk_ref, freq_ref, sgn_ref,
                   q_ref, k_ref, v_ref, u_ref):
    sub = x_ref.shape[0] // INPROJ_SUBTILES
    o1, o2, o3 = Q_LORA, Q_LORA + KV_LORA, Q_LORA + KV_LORA + KR_PAD
    gq = gq_ref[...]
    gk = gk_ref[...]
    scale = QK_HEAD_DIM ** -0.5 * LOG2_E

    def project(s):
        rs = slice(s * sub, (s + 1) * sub)
        h = _rms(x_ref[rs, :], gmix_ref[...]).astype(BF16)
        z = _dot(h, win_ref[...])
        cs = slice(s * sub // CHUNK, (s + 1) * sub // CHUNK)
        for kt in range(N_LANE_TILES):
            u_ref[cs, kt, :, :] = z[:, o3 + kt * LANE:o3 + (kt + 1) * LANE].reshape(sub // CHUNK, CHUNK, LANE)
        c_q = _rms(z[:, :o1], gql_ref[...]).astype(BF16)
        q = _dot(c_q, wuq_ref[...])
        c_kv = _rms(z[:, o1:o2], gkvl_ref[...]).astype(BF16)
        kv = _dot(c_kv, wukv_ref[...])
        return q, kv, z[:, o2:o3]

    def finish_heads(s, q, kv, kr):
        rs = slice(s * sub, (s + 1) * sub)
        v_ref[rs, :] = kv[:, ATTN_WIDTH:].astype(BF16)
        ang = pos_ref[rs, :].astype(F32) * freq_ref[...]
        cos_t = jnp.cos(ang)
        sin_t = jnp.sin(ang) * sgn_ref[...]

        def rope(t):
            return t * cos_t + pltpu.roll(t, LANE // 2, 1) * sin_t

        kr_ss = jnp.sum(kr * kr, axis=-1, keepdims=True)
        for hd in range(N_HEADS):
            qh = q[:, hd * HEAD_PAD:(hd + 1) * HEAD_PAD]
            qn = _rms(qh, gq, width=QK_HEAD_DIM) * scale
            q_ref[rs, hd * HEAD_PAD:hd * HEAD_PAD + LANE] = qn[:, :LANE].astype(BF16)
            q_ref[rs, hd * HEAD_PAD + LANE:(hd + 1) * HEAD_PAD] = rope(qn[:, LANE:]).astype(BF16)
            kn = kv[:, hd * QK_NOPE_DIM:(hd + 1) * QK_NOPE_DIM]
            ss = (jnp.sum(kn * kn, axis=-1, keepdims=True) + kr_ss) * (1.0 / QK_HEAD_DIM)
            rinv = lax.rsqrt(ss + EPS)
            k_ref[rs, hd * HEAD_PAD:hd * HEAD_PAD + LANE] = (kn * rinv * gk[:, :LANE]).astype(BF16)
            k_ref[rs, hd * HEAD_PAD + LANE:(hd + 1) * HEAD_PAD] = rope(kr * rinv * gk[:, LANE:]).astype(BF16)

    for s in range(INPROJ_SUBTILES):
        finish_heads(s, *project(s))


def _inproj(x2d, pos2d, gmix, win, gql, wuq, gkvl, wukv, gq, gk, freq, sgn, *, batch, seq, tl):
    nl = seq // tl
    tok = lambda w: pl.BlockSpec((tl, w), lambda b, i: (b * nl + i, 0))
    full = lambda a: pl.BlockSpec(a.shape, lambda b, i: (0,) * a.ndim)
    n_tok = batch * seq
    return pl.pallas_call(
        _inproj_kernel,
        grid=(batch, nl),
        in_specs=[tok(D_MODEL), tok(1), full(gmix), full(win), full(gql), full(wuq), full(gkvl),
                  full(wukv), full(gq), full(gk), full(freq), full(sgn)],
        out_specs=[tok(N_HEADS * HEAD_PAD), tok(N_HEADS * HEAD_PAD), tok(ATTN_WIDTH),
                   pl.BlockSpec((tl // CHUNK, N_LANE_TILES, CHUNK, LANE), lambda b, i: (i, 0, b, 0))],
        out_shape=[jax.ShapeDtypeStruct((n_tok, N_HEADS * HEAD_PAD), BF16),
                   jax.ShapeDtypeStruct((n_tok, N_HEADS * HEAD_PAD), BF16),
                   jax.ShapeDtypeStruct((n_tok, ATTN_WIDTH), BF16),
                   jax.ShapeDtypeStruct((seq // CHUNK, N_LANE_TILES, batch * CHUNK, LANE), F32)],
        compiler_params=_params("arbitrary", "arbitrary"),
    )(x2d, pos2d, gmix, win, gql, wuq, gkvl, wukv, gq, gk, freq, sgn)


def _attn_kernel(q_ref, k_ref, v_ref, o_ref, *, tq):
    seq = q_ref.shape[0]
    n = seq // tq
    vt = v_ref[...].T
    diag_mask = (lax.broadcasted_iota(jnp.int32, (tq, tq), 0)
                 <= lax.broadcasted_iota(jnp.int32, (tq, tq), 1))

    def scores_t(i):
        return lax.dot_general(k_ref[:(i + 1) * tq, :], q_ref[i * tq:(i + 1) * tq, :],
                               (((1,), (1,)), ((), ())), preferred_element_type=F32)

    sts = [scores_t(j) for j in range(min(QK_AHEAD, n))]
    for i in range(n):
        kv_len = (i + 1) * tq
        st = sts[i]
        if i + QK_AHEAD < n:
            sts.append(scores_t(i + QK_AHEAD))
        sd = jnp.where(diag_mask, st[kv_len - tq:], -jnp.inf)
        m = jnp.max(sd, axis=0, keepdims=True)
        if i:
            m = jnp.maximum(m, jnp.max(st[:kv_len - tq], axis=0, keepdims=True))
        pd = jnp.exp2(sd - m)
        l = jnp.sum(pd, axis=0, keepdims=True)
        if i:
            pt = jnp.exp2(st[:kv_len - tq] - m)
            l = l + jnp.sum(pt, axis=0, keepdims=True)
            p = jnp.concatenate([pt.astype(BF16), pd.astype(BF16)], axis=0)
        else:
            p = pd.astype(BF16)
        ot = _dot(vt[:, :kv_len], p)
        o_ref[i * tq:(i + 1) * tq, :] = (ot / l).T.astype(o_ref.dtype)


def _attention(q, k, v, *, batch, seq, tq):
    return pl.pallas_call(
        functools.partial(_attn_kernel, tq=tq),
        grid=(batch, N_HEADS),
        in_specs=[pl.BlockSpec((seq, HEAD_PAD), lambda b, h: (b, h)),
                  pl.BlockSpec((seq, HEAD_PAD), lambda b, h: (b, h)),
                  pl.BlockSpec((seq, V_HEAD_DIM), lambda b, h: (b, h))],
        out_specs=pl.BlockSpec((seq, V_HEAD_DIM), lambda b, h: (b, h)),
        out_shape=jax.ShapeDtypeStruct((batch * seq, ATTN_WIDTH), BF16),
        compiler_params=_params("arbitrary", "arbitrary"),
    )(q, k, v)


def _block_transpose(tiles):
    lane = lax.broadcasted_iota(jnp.int32, tiles[0].shape, 1)
    for d in (4, 2, 1):
        hi = (lane & (SSM_GROUP * d)) != 0
        new = list(tiles)
        for i in range(GROUPS_PER_TILE):
            if i & d == 0:
                a, b = tiles[i], tiles[i + d]
                new[i] = jnp.where(hi, pltpu.roll(b, SSM_GROUP * d, 1), a)
                new[i + d] = jnp.where(hi, b, pltpu.roll(a, LANE - SSM_GROUP * d, 1))
        tiles = new
    return tiles


def _s5_kernel(u_ref, wst_ref, are_ref, aim_ref, toep_ref, vmat_ref, dskip_ref, wglu_ref, bglu_ref,
               gout_ref, o_ref, state_ref, x_ref, xs_ref, ys_ref, yg_ref, *, chunks, batch):
    rows_x = chunks * batch
    rows = rows_x * CHUNK

    @pl.when(pl.program_id(0) == 0)
    def _():
        state_ref[...] = jnp.zeros_like(state_ref)

    for kt in range(N_LANE_TILES):
        tiles = [u_ref[:, pl.ds(kt, 1), pl.ds(t, batch, stride=CHUNK), :].reshape(rows_x, LANE)
                 for t in range(CHUNK)]
        outs = _block_transpose(tiles)
        for g in range(GROUPS_PER_TILE):
            c0 = (kt * GROUPS_PER_TILE + g) * LANE
            x_ref[:, c0:c0 + LANE] = outs[g].astype(BF16)

    for q in range(N_PAIRS):
        cs = slice(q * PAIR_W, (q + 1) * PAIR_W)
        xs_ref[:, cs] = _dot(x_ref[:, cs], wst_ref[q])

    def step(c, carry):
        r0 = pl.multiple_of(c * batch, batch)
        for q in range(N_PAIRS):
            re = slice(q * PAIR_W, q * PAIR_W + LANE)
            im = slice(q * PAIR_W + LANE, (q + 1) * PAIR_W)
            al = slice(q * LANE, (q + 1) * LANE)
            s_re = state_ref[:, re]
            s_im = state_ref[:, im]
            a_re = are_ref[:, al]
            a_im = aim_ref[:, al]
            x_re = xs_ref[pl.ds(r0, batch), re]
            x_im = xs_ref[pl.ds(r0, batch), im]
            xs_ref[pl.ds(r0, batch), re] = s_re
            xs_ref[pl.ds(r0, batch), im] = s_im
            state_ref[:, re] = a_re * s_re - a_im * s_im + x_re
            state_ref[:, im] = a_re * s_im + a_im * s_re + x_im
        return carry

    lax.fori_loop(0, chunks, step, 0)

    for q in range(N_PAIRS):
        cs = slice(q * PAIR_W, (q + 1) * PAIR_W)
        xs_ref[:, cs] = (_dot(x_ref[:, cs], toep_ref[q])
                         + _dot(xs_ref[:, cs].astype(BF16), vmat_ref[q]))

    for kt in range(N_LANE_TILES):
        tiles = [xs_ref[:, (kt * GROUPS_PER_TILE + g) * LANE:(kt * GROUPS_PER_TILE + g + 1) * LANE]
                 for g in range(GROUPS_PER_TILE)]
        outs = _block_transpose(tiles)
        for t in range(CHUNK):
            ys_ref[:, pl.ds(kt, 1), pl.ds(t, batch, stride=CHUNK), :] = outs[t].reshape(chunks, 1, batch, LANE)

    for kt in range(N_LANE_TILES):
        ls = slice(kt * LANE, (kt + 1) * LANE)
        y_k = (ys_ref[:, kt].reshape(rows, LANE)
               + dskip_ref[:, ls] * u_ref[:, kt].reshape(rows, LANE))
        yg_ref[:, ls] = jax.nn.gelu(y_k, approximate=True)
    y = yg_ref[...]
    gate = _sigmoid(_dot(y.astype(BF16), wglu_ref[...]) + bglu_ref[...])
    o_ref[...] = _rms(y * gate, gout_ref[...]).reshape(chunks, batch, CHUNK, SSM_WIDTH)


def _s5(u_slab, wst, a_re, a_im, toep, vmat, dskip, wglu, bglu, gout, *, batch, seq, chunks):
    n_chunks = seq // CHUNK
    rows_x = chunks * batch
    full = lambda a: pl.BlockSpec(a.shape, lambda i: (0,) * a.ndim, pipeline_mode=pl.Buffered(1))
    return pl.pallas_call(
        functools.partial(_s5_kernel, chunks=chunks, batch=batch),
        grid=(n_chunks // chunks,),
        in_specs=[pl.BlockSpec((chunks, N_LANE_TILES, batch * CHUNK, LANE), lambda i: (i, 0, 0, 0)),
                  full(wst), full(a_re), full(a_im), full(toep), full(vmat), full(dskip), full(wglu),
                  full(bglu), full(gout)],
        out_specs=pl.BlockSpec((chunks, batch, CHUNK, SSM_WIDTH), lambda i: (i, 0, 0, 0)),
        out_shape=jax.ShapeDtypeStruct((n_chunks, batch, CHUNK, SSM_WIDTH), F32),
        scratch_shapes=[pltpu.VMEM((batch, N_PAIRS * PAIR_W), F32),
                        pltpu.VMEM((rows_x, N_PAIRS * PAIR_W), BF16),
                        pltpu.VMEM((rows_x, N_PAIRS * PAIR_W), F32),
                        pltpu.VMEM((chunks, N_LANE_TILES, batch * CHUNK, LANE), F32),
                        pltpu.VMEM((rows_x * CHUNK, SSM_WIDTH), F32)],
        compiler_params=_params("arbitrary"),
    )(u_slab, wst, a_re, a_im, toep, vmat, dskip, wglu, bglu, gout)


def _oproj_kernel(x_ref, oa_ref, os_ref, ga_ref, woa_ref, wos_ref, out_ref):
    sub = x_ref.shape[0] // OPROJ_SUBTILES
    for s in range(OPROJ_SUBTILES):
        rs = slice(s * sub, (s + 1) * sub)
        oa = _rms(oa_ref[rs, :].astype(F32), ga_ref[...]).astype(BF16)
        os_ = os_ref[s * sub // CHUNK:(s + 1) * sub // CHUNK].reshape(sub, SSM_WIDTH).astype(BF16)
        out_ref[rs, :] = x_ref[rs, :] + _dot(oa, woa_ref[...]) + _dot(os_, wos_ref[...])


def _oproj(x2d, o_attn, o_ssm, g_attn, wo_a, wo_s, *, batch, seq, tl):
    nl = seq // tl
    tok = lambda w: pl.BlockSpec((tl, w), lambda b, i: (b * nl + i, 0))
    full = lambda a: pl.BlockSpec(a.shape, lambda b, i: (0,) * a.ndim)
    return pl.pallas_call(
        _oproj_kernel,
        grid=(batch, nl),
        in_specs=[tok(D_MODEL), tok(ATTN_WIDTH),
                  pl.BlockSpec((tl // CHUNK, None, CHUNK, SSM_WIDTH), lambda b, i: (i, b, 0, 0)),
                  full(g_attn), full(wo_a), full(wo_s)],
        out_specs=tok(D_MODEL),
        out_shape=jax.ShapeDtypeStruct((batch * seq, D_MODEL), F32),
        compiler_params=_params("arbitrary", "arbitrary"),
    )(x2d, o_attn, o_ssm, g_attn, wo_a, wo_s)


def _ffn_kernel(x_ref, g_ref, wg_ref, wu_ref, wd_ref, out_ref, h_ref):
    j = pl.program_id(1)

    def ff_block(h):
        gate = _dot(h, wg_ref[...])
        up = _dot(h, wu_ref[...])
        act = (gate * _sigmoid(gate) * up).astype(BF16)
        return _dot(act, wd_ref[...])

    @pl.when(j == 0)
    def _():
        sub = x_ref.shape[0] // FFN_SUBTILES
        for s in range(FFN_SUBTILES):
            rs = slice(s * sub, (s + 1) * sub)
            x = x_ref[rs, :]
            h = _rms(x, g_ref[...]).astype(BF16)
            h_ref[rs, :] = h
            out_ref[rs, :] = x + ff_block(h)

    @pl.when(j > 0)
    def _():
        out_ref[...] += ff_block(h_ref[...])


def _ffn(x2d, g, wg, wu, wd, *, tm, tf):
    n_tok = x2d.shape[0]
    return pl.pallas_call(
        _ffn_kernel,
        grid=(n_tok // tm, D_FF // tf),
        in_specs=[pl.BlockSpec((tm, D_MODEL), lambda i, j: (i, 0)),
                  pl.BlockSpec((1, D_MODEL), lambda i, j: (0, 0)),
                  pl.BlockSpec((D_MODEL, tf), lambda i, j: (0, j)),
                  pl.BlockSpec((D_MODEL, tf), lambda i, j: (0, j)),
                  pl.BlockSpec((tf, D_MODEL), lambda i, j: (j, 0))],
        out_specs=pl.BlockSpec((tm, D_MODEL), lambda i, j: (i, 0)),
        out_shape=jax.ShapeDtypeStruct((n_tok, D_MODEL), F32),
        scratch_shapes=[pltpu.VMEM((tm, D_MODEL), BF16)],
        compiler_params=_params("arbitrary", "arbitrary"),
    )(x2d, g, wg, wu, wd)


def _ple_kernel(x_ref, p_ref, g_ref, wpg_ref, wpp_ref, out_ref):
    sub = x_ref.shape[0] // PLE_SUBTILES
    for s in range(PLE_SUBTILES):
        rs = slice(s * sub, (s + 1) * sub)
        x = x_ref[rs, :]
        h = _rms(x, g_ref[...]).astype(BF16)
        gate = _sigmoid(_dot(h, wpg_ref[...]))
        out_ref[rs, :] = x + gate * _dot(p_ref[rs, :].astype(BF16), wpp_ref[...])


def _ple(x2d, p2d, g, wpg, wpp, *, tm):
    n_tok = x2d.shape[0]
    full = lambda a: pl.BlockSpec(a.shape, lambda i: (0,) * a.ndim)
    return pl.pallas_call(
        _ple_kernel,
        grid=(n_tok // tm,),
        in_specs=[pl.BlockSpec((tm, D_MODEL), lambda i: (i, 0)),
                  pl.BlockSpec((tm, PLE_DIM), lambda i: (i, 0)),
                  full(g), full(wpg), full(wpp)],
        out_specs=pl.BlockSpec((tm, D_MODEL), lambda i: (i, 0)),
        out_shape=jax.ShapeDtypeStruct((n_tok, D_MODEL), F32),
        compiler_params=_params("arbitrary"),
    )(x2d, p2d, g, wpg, wpp)


def _rope_tile(t):
    z = jnp.zeros(t.shape[:-1] + (ROPE_HALF,), t.dtype)
    return jnp.concatenate([t[..., :ROPE_HALF], z, t[..., ROPE_HALF:], z], axis=-1)


def _head_gain(g):
    return jnp.concatenate([g[:QK_NOPE_DIM], _rope_tile(g[QK_NOPE_DIM:])])[None, :].astype(F32)


def _s5_params(lam_re, lam_im, log_dt, b_re, b_im, c_re, c_im):
    G, P, H, T, NQ = SSM_GROUPS, SSM_STATE, SSM_GROUP, CHUNK, N_PAIRS
    TH = T * H
    lr = jnp.minimum(lam_re.astype(F32), -1e-4)
    li = lam_im.astype(F32)
    dt = jnp.exp(log_dt.astype(F32))[:, None]
    mag = jnp.exp(lr * dt)
    abar_re = mag * jnp.cos(li * dt)
    abar_im = mag * jnp.sin(li * dt)
    den = lr * lr + li * li
    num_re = abar_re - 1.0
    num_im = abar_im
    coef_re = ((num_re * lr + num_im * li) / den)[:, None, :]
    coef_im = ((num_im * lr - num_re * li) / den)[:, None, :]
    br = b_re.astype(F32).transpose(0, 2, 1)
    bim = b_im.astype(F32).transpose(0, 2, 1)
    bb_re = coef_re * br - coef_im * bim
    bb_im = coef_re * bim + coef_im * br
    pw_re, pw_im = [jnp.ones_like(abar_re)], [jnp.zeros_like(abar_im)]
    for _ in range(T):
        r, i = pw_re[-1], pw_im[-1]
        pw_re.append(r * abar_re - i * abar_im)
        pw_im.append(r * abar_im + i * abar_re)
    pw_re = jnp.stack(pw_re, axis=1)[:, :, None, :]
    pw_im = jnp.stack(pw_im, axis=1)[:, :, None, :]
    cr = c_re.astype(F32)[:, None]
    ci = c_im.astype(F32)[:, None]
    ca_re = cr * pw_re - ci * pw_im
    ca_im = cr * pw_im + ci * pw_re
    cat = lambda re, im: jnp.concatenate([re.reshape(G, -1, P), im.reshape(G, -1, P)], axis=-1)
    m1 = jnp.einsum('gik,gjk->gij', cat(bb_re, bb_im), cat(ca_re[:, :T], -ca_im[:, :T]),
                    precision=lax.Precision.HIGHEST)
    toep = jnp.stack([jnp.pad(m1[:, :, :TH - H * t], ((0, 0), (0, 0), (H * t, 0))) for t in range(T)],
                     axis=1).reshape(G, TH, TH)
    rev_re = pw_re[:, :T][:, ::-1]
    rev_im = pw_im[:, :T][:, ::-1]
    w_re = (rev_re * bb_re[:, None] - rev_im * bb_im[:, None]).reshape(NQ, 2, TH, P)
    w_im = (rev_re * bb_im[:, None] + rev_im * bb_re[:, None]).reshape(NQ, 2, TH, P)
    zw = jnp.zeros((NQ, TH, P), F32)
    wst_p = jnp.concatenate([
        jnp.concatenate([w_re[:, 0], zw, w_im[:, 0], zw], axis=-1),
        jnp.concatenate([zw, w_re[:, 1], zw, w_im[:, 1]], axis=-1)], axis=1)
    tp = toep.reshape(NQ, 2, TH, TH)
    zt = jnp.zeros((NQ, TH, TH), F32)
    toep_p = jnp.concatenate([jnp.concatenate([tp[:, 0], zt], axis=-1),
                              jnp.concatenate([zt, tp[:, 1]], axis=-1)], axis=1)
    v_t = cat(ca_re[:, 1:], -ca_im[:, 1:]).transpose(0, 2, 1).reshape(NQ, 2, 2, P, TH)
    zv = jnp.zeros((NQ, P, TH), F32)
    vmat_p = jnp.concatenate([
        jnp.concatenate([v_t[:, 0, 0], zv], axis=-1), jnp.concatenate([zv, v_t[:, 1, 0]], axis=-1),
        jnp.concatenate([v_t[:, 0, 1], zv], axis=-1), jnp.concatenate([zv, v_t[:, 1, 1]], axis=-1)], axis=1)
    a_re = jnp.broadcast_to(pw_re[:, T].reshape(1, G * P), (SUBLANE, G * P))
    a_im = jnp.broadcast_to(pw_im[:, T].reshape(1, G * P), (SUBLANE, G * P))
    return wst_p.astype(BF16), a_re, a_im, toep_p.astype(BF16), vmat_p.astype(BF16)


def kernel(x, p, positions, g_mix_norm, w_in, g_q_lora, w_uq, g_kv_lora, w_ukv, g_q_head, g_k_head,
           lam_re, lam_im, log_dt, b_re, b_im, c_re, c_im, d_skip, w_glu, b_glu, g_out_attn,
           g_out_ssm, w_o, g_ffn_norm, w_gate, w_up, w_down, g_ple_norm, w_ple_gate, w_ple_proj):
    batch, seq, _ = x.shape
    assert batch == SUBLANE, "the S5 chunk recurrence keeps one batch row per sublane"
    depth = w_in.shape[0]
    n_tok = batch * seq
    row = lambda g: g[None, :].astype(F32)

    inv_freq = 1.0 / (ROPE_THETA ** (jnp.arange(0, QK_ROPE_DIM, 2, dtype=F32) / QK_ROPE_DIM))
    freq = _rope_tile(jnp.concatenate([inv_freq, inv_freq]))[None, :]
    ones = jnp.ones((ROPE_HALF,), F32)
    sgn = _rope_tile(jnp.concatenate([-ones, ones]))[None, :]
    pos2d = positions.reshape(n_tok, 1)

    x2d = x.reshape(n_tok, D_MODEL)
    for i in range(depth):
        o1, o2, o3 = Q_LORA, Q_LORA + KV_LORA, Q_LORA + KV_LORA + QK_ROPE_DIM
        wi = w_in[i].astype(BF16)
        win = jnp.concatenate([wi[:, :o2], _rope_tile(wi[:, o2:o3]), wi[:, o3:]], axis=1)
        wq = w_uq[i].reshape(Q_LORA, N_HEADS, QK_HEAD_DIM)
        wuq = jnp.concatenate([wq[..., :QK_NOPE_DIM], _rope_tile(wq[..., QK_NOPE_DIM:])], axis=-1)
        wuq = wuq.reshape(Q_LORA, N_HEADS * HEAD_PAD).astype(BF16)
        wkv = w_ukv[i].reshape(KV_LORA, N_HEADS, QK_NOPE_DIM + V_HEAD_DIM)
        wukv = jnp.concatenate([wkv[..., :QK_NOPE_DIM].reshape(KV_LORA, -1),
                                wkv[..., QK_NOPE_DIM:].reshape(KV_LORA, -1)], axis=1).astype(BF16)

        q, k, v, u_slab = _inproj(x2d, pos2d, row(g_mix_norm[i]), win, row(g_q_lora[i]), wuq,
                                  row(g_kv_lora[i]), wukv, _head_gain(g_q_head[i]),
                                  _head_gain(g_k_head[i]), freq, sgn, batch=batch, seq=seq, tl=512)
        o_attn = _attention(q, k, v, batch=batch, seq=seq, tq=256)

        wst, a_re, a_im, toep, vmat = _s5_params(lam_re[i], lam_im[i], log_dt[i], b_re[i], b_im[i],
                                                 c_re[i], c_im[i])
        o_ssm = _s5(u_slab, wst, a_re, a_im, toep, vmat, d_skip[i].reshape(1, SSM_WIDTH).astype(F32),
                    w_glu[i].astype(BF16), row(b_glu[i]), row(g_out_ssm[i]),
                    batch=batch, seq=seq, chunks=16)

        wo = w_o[i].astype(BF16)
        x2d = _oproj(x2d, o_attn, o_ssm, row(g_out_attn[i]), wo[:ATTN_WIDTH], wo[ATTN_WIDTH:],
                     batch=batch, seq=seq, tl=1024)
        x2d = _ffn(x2d, row(g_ffn_norm[i]), w_gate[i].astype(BF16), w_up[i].astype(BF16),
                   w_down[i].astype(BF16), tm=1024, tf=512)
        x2d = _ple(x2d, p[i].reshape(n_tok, PLE_DIM), row(g_ple_norm[i]),
                   w_ple_gate[i].astype(BF16), w_ple_proj[i].astype(BF16), tm=1024)
    return x2d.reshape(batch, seq, D_MODEL)
```

```python
import functools

import jax
import jax.numpy as jnp
from jax import lax
from jax.experimental import pallas as pl
from jax.experimental.pallas import tpu as pltpu

D_MODEL = 2048
PLE_DIM = 256
N_HEADS = 8
QK_NOPE_DIM = 128
QK_ROPE_DIM = 64
V_HEAD_DIM = 128
QK_HEAD_DIM = QK_NOPE_DIM + QK_ROPE_DIM
Q_LORA = 512
KV_LORA = 256
ATTN_WIDTH = N_HEADS * V_HEAD_DIM
ROPE_THETA = 10000.0
SSM_WIDTH = 1024
SSM_GROUP = 16
SSM_GROUPS = SSM_WIDTH // SSM_GROUP
SSM_STATE = 64
D_FF = 5632
EPS = 1e-6

LANE = 128
SUBLANE = 8
HEAD_PAD = 2 * LANE
ROPE_HALF = QK_ROPE_DIM // 2
KR_PAD = LANE
CHUNK = SUBLANE
N_LANE_TILES = SSM_WIDTH // LANE
GROUPS_PER_TILE = LANE // SSM_GROUP
PAIR_W = 2 * LANE
N_PAIRS = SSM_GROUPS // 2
VMEM_LIMIT = 56 * 1024 * 1024
LOG2_E = 1.4426950408889634
FFN_SUBTILES = 2
INPROJ_SUBTILES = 2
PLE_SUBTILES = 2
OPROJ_SUBTILES = 4
ATTN_HEADS_PER_STEP = 2
QK_AHEAD = 4

BF16 = jnp.bfloat16
F32 = jnp.float32


def _rms(t, g, width=None):
    n = t.shape[-1] if width is None else width
    ss = jnp.sum(t * t, axis=-1, keepdims=True) * (1.0 / n)
    return t * lax.rsqrt(ss + EPS) * g


def _dot(a, b):
    return jnp.dot(a, b, preferred_element_type=F32)


def _sigmoid(t):
    return 1.0 / (1.0 + jnp.exp(-t))


def _params(*sem):
    return pltpu.CompilerParams(dimension_semantics=sem, vmem_limit_bytes=VMEM_LIMIT)


def _inproj_kernel(x_ref, pos_ref, gmix_ref, wmla_ref, wssm_ref, gql_ref, wuq_ref, gkvl_ref, wukv_ref,
                   gq_ref, gk_ref, freq_ref, sgn_ref,
                   q_ref, k_ref, v_ref, u_ref):
    sub = x_ref.shape[0] // INPROJ_SUBTILES
    o1, o2, o3 = Q_LORA, Q_LORA + KV_LORA, Q_LORA + KV_LORA + KR_PAD
    gq = gq_ref[...]
    gk = gk_ref[...]
    scale = QK_HEAD_DIM ** -0.5 * LOG2_E

    def project(s):
        rs = slice(s * sub, (s + 1) * sub)
        h = _rms(x_ref[rs, :], gmix_ref[...]).astype(BF16)
        z = _dot(h, wmla_ref[...])
        u = _dot(h, wssm_ref[...])
        cs = slice(s * sub // CHUNK, (s + 1) * sub // CHUNK)
        for kt in range(N_LANE_TILES):
            u_ref[cs, kt, :, :] = u[:, kt * LANE:(kt + 1) * LANE].reshape(sub // CHUNK, CHUNK, LANE)
        c_q = _rms(z[:, :o1], gql_ref[...]).astype(BF16)
        q = _dot(c_q, wuq_ref[...])
        c_kv = _rms(z[:, o1:o2], gkvl_ref[...]).astype(BF16)
        kv = _dot(c_kv, wukv_ref[...])
        return q, kv, z[:, o2:o3]

    def finish_heads(s, q, kv, kr):
        rs = slice(s * sub, (s + 1) * sub)
        v_ref[rs, :] = kv[:, ATTN_WIDTH:].astype(BF16)
        ang = pos_ref[rs, :].astype(F32) * freq_ref[...]
        cos_t = jnp.cos(ang)
        sin_t = jnp.sin(ang) * sgn_ref[...]

        def rope(t):
            return t * cos_t + pltpu.roll(t, LANE // 2, 1) * sin_t

        kr_ss = jnp.sum(kr * kr, axis=-1, keepdims=True)
        for hd in range(N_HEADS):
            qh = q[:, hd * HEAD_PAD:(hd + 1) * HEAD_PAD]
            qn = _rms(qh, gq, width=QK_HEAD_DIM) * scale
            q_ref[rs, hd * HEAD_PAD:hd * HEAD_PAD + LANE] = qn[:, :LANE].astype(BF16)
            q_ref[rs, hd * HEAD_PAD + LANE:(hd + 1) * HEAD_PAD] = rope(qn[:, LANE:]).astype(BF16)
            kn = kv[:, hd * QK_NOPE_DIM:(hd + 1) * QK_NOPE_DIM]
            ss = (jnp.sum(kn * kn, axis=-1, keepdims=True) + kr_ss) * (1.0 / QK_HEAD_DIM)
            rinv = lax.rsqrt(ss + EPS)
            k_ref[rs, hd * HEAD_PAD:hd * HEAD_PAD + LANE] = (kn * rinv * gk[:, :LANE]).astype(BF16)
            k_ref[rs, hd * HEAD_PAD + LANE:(hd + 1) * HEAD_PAD] = rope(kr * rinv * gk[:, LANE:]).astype(BF16)

    for s in range(INPROJ_SUBTILES):
        finish_heads(s, *project(s))


def _inproj(x2d, pos2d, gmix, wmla, wssm, gql, wuq, gkvl, wukv, gq, gk, freq, sgn, *, batch, seq, tl):
    nl = seq // tl
    tok = lambda w: pl.BlockSpec((tl, w), lambda b, i: (b * nl + i, 0))
    full = lambda a: pl.BlockSpec(a.shape, lambda b, i: (0,) * a.ndim)
    n_tok = batch * seq
    return pl.pallas_call(
        _inproj_kernel,
        grid=(batch, nl),
        in_specs=[tok(D_MODEL), tok(1), full(gmix), full(wmla), full(wssm), full(gql), full(wuq), full(gkvl),
                  full(wukv), full(gq), full(gk), full(freq), full(sgn)],
        out_specs=[tok(N_HEADS * HEAD_PAD), tok(N_HEADS * HEAD_PAD), tok(ATTN_WIDTH),
                   pl.BlockSpec((tl // CHUNK, N_LANE_TILES, CHUNK, LANE), lambda b, i: (i, 0, b, 0))],
        out_shape=[jax.ShapeDtypeStruct((n_tok, N_HEADS * HEAD_PAD), BF16),
                   jax.ShapeDtypeStruct((n_tok, N_HEADS * HEAD_PAD), BF16),
                   jax.ShapeDtypeStruct((n_tok, ATTN_WIDTH), BF16),
                   jax.ShapeDtypeStruct((seq // CHUNK, N_LANE_TILES, batch * CHUNK, LANE), F32)],
        compiler_params=_params("arbitrary", "arbitrary"),
    )(x2d, pos2d, gmix, wmla, wssm, gql, wuq, gkvl, wukv, gq, gk, freq, sgn)


def _attn_kernel(q_ref, k_ref, v_ref, o_ref, *, tq):
    seq = q_ref.shape[0]
    n = seq // tq
    vts = [v_ref[:, hd * V_HEAD_DIM:(hd + 1) * V_HEAD_DIM].T for hd in range(ATTN_HEADS_PER_STEP)]
    diag_mask = (lax.broadcasted_iota(jnp.int32, (tq, tq), 0)
                 <= lax.broadcasted_iota(jnp.int32, (tq, tq), 1))

    def scores_t(item):
        hd, i = item
        hs = slice(hd * HEAD_PAD, (hd + 1) * HEAD_PAD)
        return lax.dot_general(k_ref[:(i + 1) * tq, hs], q_ref[i * tq:(i + 1) * tq, hs],
                               (((1,), (1,)), ((), ())), preferred_element_type=F32)

    order = [(hd, i) for i in range(n - 1, -1, -1) for hd in range(ATTN_HEADS_PER_STEP)]
    sts = {item: scores_t(item) for item in order[:QK_AHEAD]}
    for pos, item in enumerate(order):
        hd, i = item
        kv_len = (i + 1) * tq
        st = sts.pop(item)
        if pos + QK_AHEAD < len(order):
            ahead = order[pos + QK_AHEAD]
            sts[ahead] = scores_t(ahead)
        sd = jnp.where(diag_mask, st[kv_len - tq:], -jnp.inf)
        m = jnp.max(sd, axis=0, keepdims=True)
        if i:
            m = jnp.maximum(m, jnp.max(st[:kv_len - tq], axis=0, keepdims=True))
        pd = jnp.exp2(sd - m)
        l = jnp.sum(pd, axis=0, keepdims=True)
        if i:
            pt = jnp.exp2(st[:kv_len - tq] - m)
            l = l + jnp.sum(pt, axis=0, keepdims=True)
            p = jnp.concatenate([pt.astype(BF16), pd.astype(BF16)], axis=0)
        else:
            p = pd.astype(BF16)
        ot = _dot(vts[hd][:, :kv_len], p)
        o_ref[i * tq:(i + 1) * tq, hd * V_HEAD_DIM:(hd + 1) * V_HEAD_DIM] = (ot / l).T.astype(o_ref.dtype)


def _attention(q, k, v, *, batch, seq, tq):
    hps = ATTN_HEADS_PER_STEP
    return pl.pallas_call(
        functools.partial(_attn_kernel, tq=tq),
        grid=(batch, N_HEADS // hps),
        in_specs=[pl.BlockSpec((seq, hps * HEAD_PAD), lambda b, h: (b, h)),
                  pl.BlockSpec((seq, hps * HEAD_PAD), lambda b, h: (b, h)),
                  pl.BlockSpec((seq, hps * V_HEAD_DIM), lambda b, h: (b, h))],
        out_specs=pl.BlockSpec((seq, hps * V_HEAD_DIM), lambda b, h: (b, h)),
        out_shape=jax.ShapeDtypeStruct((batch * seq, ATTN_WIDTH), BF16),
        compiler_params=_params("arbitrary", "arbitrary"),
    )(q, k, v)


def _block_transpose(tiles):
    lane = lax.broadcasted_iota(jnp.int32, tiles[0].shape, 1)
    for d in (4, 2, 1):
        hi = (lane & (SSM_GROUP * d)) != 0
        new = list(tiles)
        for i in range(GROUPS_PER_TILE):
            if i & d == 0:
                a, b = tiles[i], tiles[i + d]
                new[i] = jnp.where(hi, pltpu.roll(b, SSM_GROUP * d, 1), a)
                new[i + d] = jnp.where(hi, b, pltpu.roll(a, LANE - SSM_GROUP * d, 1))
        tiles = new
    return tiles


def _s5_kernel(u_ref, wst_ref, are_ref, aim_ref, toep_ref, vmat_ref, wglu_ref, bglu_ref,
               gout_ref, o_ref, state_ref, x_ref, xs_ref, ys_ref, yg_ref, *, chunks, batch):
    rows_x = chunks * batch
    rows = rows_x * CHUNK

    @pl.when(pl.program_id(0) == 0)
    def _():
        state_ref[...] = jnp.zeros_like(state_ref)

    for kt in range(N_LANE_TILES):
        tiles = [u_ref[:, pl.ds(kt, 1), pl.ds(t, batch, stride=CHUNK), :].reshape(rows_x, LANE)
                 for t in range(CHUNK)]
        outs = _block_transpose([pltpu.bitcast(t.astype(BF16), jnp.uint32) for t in tiles])
        for g in range(GROUPS_PER_TILE):
            c0 = (kt * GROUPS_PER_TILE + g) * LANE
            x_ref[:, c0:c0 + LANE] = pltpu.bitcast(outs[g], BF16)

    for q in range(N_PAIRS):
        cs = slice(q * PAIR_W, (q + 1) * PAIR_W)
        xs_ref[:, cs] = _dot(x_ref[:, cs], wst_ref[q])

    def step(c, carry):
        r0 = pl.multiple_of(c * batch, batch)
        for q in range(N_PAIRS):
            re = slice(q * PAIR_W, q * PAIR_W + LANE)
            im = slice(q * PAIR_W + LANE, (q + 1) * PAIR_W)
            al = slice(q * LANE, (q + 1) * LANE)
            s_re = state_ref[:, re]
            s_im = state_ref[:, im]
            a_re = are_ref[:, al]
            a_im = aim_ref[:, al]
            x_re = xs_ref[pl.ds(r0, batch), re]
            x_im = xs_ref[pl.ds(r0, batch), im]
            xs_ref[pl.ds(r0, batch), re] = s_re
            xs_ref[pl.ds(r0, batch), im] = s_im
            state_ref[:, re] = a_re * s_re - a_im * s_im + x_re
            state_ref[:, im] = a_re * s_im + a_im * s_re + x_im
        return carry

    lax.fori_loop(0, chunks, step, 0)

    for q in range(N_PAIRS):
        cs = slice(q * PAIR_W, (q + 1) * PAIR_W)
        xs_ref[:, cs] = (_dot(x_ref[:, cs], toep_ref[q])
                         + _dot(xs_ref[:, cs].astype(BF16), vmat_ref[q]))

    for kt in range(N_LANE_TILES):
        tiles = [xs_ref[:, (kt * GROUPS_PER_TILE + g) * LANE:(kt * GROUPS_PER_TILE + g + 1) * LANE]
                 for g in range(GROUPS_PER_TILE)]
        outs = _block_transpose(tiles)
        for t in range(CHUNK):
            ys_ref[:, pl.ds(kt, 1), pl.ds(t, batch, stride=CHUNK), :] = outs[t].reshape(chunks, 1, batch, LANE)

    for kt in range(N_LANE_TILES):
        yg_ref[:, kt * LANE:(kt + 1) * LANE] = jax.nn.gelu(ys_ref[:, kt].reshape(rows, LANE), approximate=True)
    y = yg_ref[...]
    gate = _sigmoid(_dot(y.astype(BF16), wglu_ref[...]) + bglu_ref[...])
    o_ref[...] = _rms(y * gate, gout_ref[...]).reshape(chunks, batch, CHUNK, SSM_WIDTH)


def _s5(u_slab, wst, a_re, a_im, toep, vmat, wglu, bglu, gout, *, batch, seq, chunks):
    n_chunks = seq // CHUNK
    rows_x = chunks * batch
    full = lambda a: pl.BlockSpec(a.shape, lambda i: (0,) * a.ndim, pipeline_mode=pl.Buffered(1))
    return pl.pallas_call(
        functools.partial(_s5_kernel, chunks=chunks, batch=batch),
        grid=(n_chunks // chunks,),
        in_specs=[pl.BlockSpec((chunks, N_LANE_TILES, batch * CHUNK, LANE), lambda i: (i, 0, 0, 0)),
                  full(wst), full(a_re), full(a_im), full(toep), full(vmat), full(wglu),
                  full(bglu), full(gout)],
        out_specs=pl.BlockSpec((chunks, batch, CHUNK, SSM_WIDTH), lambda i: (i, 0, 0, 0)),
        out_shape=jax.ShapeDtypeStruct((n_chunks, batch, CHUNK, SSM_WIDTH), F32),
        scratch_shapes=[pltpu.VMEM((batch, N_PAIRS * PAIR_W), F32),
                        pltpu.VMEM((rows_x, N_PAIRS * PAIR_W), BF16),
                        pltpu.VMEM((rows_x, N_PAIRS * PAIR_W), F32),
                        pltpu.VMEM((chunks, N_LANE_TILES, batch * CHUNK, LANE), F32),
                        pltpu.VMEM((rows_x * CHUNK, SSM_WIDTH), F32)],
        compiler_params=_params("arbitrary"),
    )(u_slab, wst, a_re, a_im, toep, vmat, wglu, bglu, gout)


def _oproj_kernel(x_ref, oa_ref, os_ref, ga_ref, woa_ref, wos_ref, out_ref):
    sub = x_ref.shape[0] // OPROJ_SUBTILES
    for s in range(OPROJ_SUBTILES):
        rs = slice(s * sub, (s + 1) * sub)
        oa = _rms(oa_ref[rs, :].astype(F32), ga_ref[...]).astype(BF16)
        os_ = os_ref[s * sub // CHUNK:(s + 1) * sub // CHUNK].reshape(sub, SSM_WIDTH).astype(BF16)
        out_ref[rs, :] = x_ref[rs, :] + _dot(oa, woa_ref[...]) + _dot(os_, wos_ref[...])


def _oproj(x2d, o_attn, o_ssm, g_attn, wo_a, wo_s, *, batch, seq, tl):
    nl = seq // tl
    tok = lambda w: pl.BlockSpec((tl, w), lambda b, i: (b * nl + i, 0))
    full = lambda a: pl.BlockSpec(a.shape, lambda b, i: (0,) * a.ndim)
    return pl.pallas_call(
        _oproj_kernel,
        grid=(batch, nl),
        in_specs=[tok(D_MODEL), tok(ATTN_WIDTH),
                  pl.BlockSpec((tl // CHUNK, None, CHUNK, SSM_WIDTH), lambda b, i: (i, b, 0, 0)),
                  full(g_attn), full(wo_a), full(wo_s)],
        out_specs=tok(D_MODEL),
        out_shape=jax.ShapeDtypeStruct((batch * seq, D_MODEL), F32),
        compiler_params=_params("arbitrary", "arbitrary"),
    )(x2d, o_attn, o_ssm, g_attn, wo_a, wo_s)


def _ffn_kernel(x_ref, g_ref, wg_ref, wu_ref, wd_ref, out_ref, h_ref):
    j = pl.program_id(1)

    def ff_block(h):
        gate = _dot(h, wg_ref[...])
        up = _dot(h, wu_ref[...])
        act = (gate * _sigmoid(gate) * up).astype(BF16)
        return _dot(act, wd_ref[...])

    @pl.when(j == 0)
    def _():
        sub = x_ref.shape[0] // FFN_SUBTILES
        for s in range(FFN_SUBTILES):
            rs = slice(s * sub, (s + 1) * sub)
            x = x_ref[rs, :]
            h = _rms(x, g_ref[...]).astype(BF16)
            h_ref[rs, :] = h
            out_ref[rs, :] = x + ff_block(h)

    @pl.when(j > 0)
    def _():
        out_ref[...] += ff_block(h_ref[...])


def _ffn(x2d, g, wg, wu, wd, *, tm, tf):
    n_tok = x2d.shape[0]
    return pl.pallas_call(
        _ffn_kernel,
        grid=(n_tok // tm, D_FF // tf),
        in_specs=[pl.BlockSpec((tm, D_MODEL), lambda i, j: (i, 0)),
                  pl.BlockSpec((1, D_MODEL), lambda i, j: (0, 0)),
                  pl.BlockSpec((D_MODEL, tf), lambda i, j: (0, j)),
                  pl.BlockSpec((D_MODEL, tf), lambda i, j: (0, j)),
                  pl.BlockSpec((tf, D_MODEL), lambda i, j: (j, 0))],
        out_specs=pl.BlockSpec((tm, D_MODEL), lambda i, j: (i, 0)),
        out_shape=jax.ShapeDtypeStruct((n_tok, D_MODEL), F32),
        scratch_shapes=[pltpu.VMEM((tm, D_MODEL), BF16)],
        compiler_params=_params("arbitrary", "arbitrary"),
    )(x2d, g, wg, wu, wd)


def _ple_kernel(x_ref, p_ref, g_ref, wpg_ref, wpp_ref, out_ref):
    sub = x_ref.shape[0] // PLE_SUBTILES
    for s in range(PLE_SUBTILES):
        rs = slice(s * sub, (s + 1) * sub)
        x = x_ref[rs, :]
        h = _rms(x, g_ref[...]).astype(BF16)
        gate = _sigmoid(_dot(h, wpg_ref[...]))
        out_ref[rs, :] = x + gate * _dot(p_ref[rs, :].astype(BF16), wpp_ref[...])


def _ple(x2d, p2d, g, wpg, wpp, *, tm):
    n_tok = x2d.shape[0]
    full = lambda a: pl.BlockSpec(a.shape, lambda i: (0,) * a.ndim)
    return pl.pallas_call(
        _ple_kernel,
        grid=(n_tok // tm,),
        in_specs=[pl.BlockSpec((tm, D_MODEL), lambda i: (i, 0)),
                  pl.BlockSpec((tm, PLE_DIM), lambda i: (i, 0)),
                  full(g), full(wpg), full(wpp)],
        out_specs=pl.BlockSpec((tm, D_MODEL), lambda i: (i, 0)),
        out_shape=jax.ShapeDtypeStruct((n_tok, D_MODEL), F32),
        compiler_params=_params("arbitrary"),
    )(x2d, p2d, g, wpg, wpp)


def _rope_tile(t):
    z = jnp.zeros(t.shape[:-1] + (ROPE_HALF,), t.dtype)
    return jnp.concatenate([t[..., :ROPE_HALF], z, t[..., ROPE_HALF:], z], axis=-1)


def _head_gain(g):
    return jnp.concatenate([g[:QK_NOPE_DIM], _rope_tile(g[QK_NOPE_DIM:])])[None, :].astype(F32)


def _s5_params(lam_re, lam_im, log_dt, b_re, b_im, c_re, c_im, d_skip):
    G, P, H, T, NQ = SSM_GROUPS, SSM_STATE, SSM_GROUP, CHUNK, N_PAIRS
    TH = T * H
    lr = jnp.minimum(lam_re.astype(F32), -1e-4)
    li = lam_im.astype(F32)
    dt = jnp.exp(log_dt.astype(F32))[:, None]
    mag = jnp.exp(lr * dt)
    abar_re = mag * jnp.cos(li * dt)
    abar_im = mag * jnp.sin(li * dt)
    den = lr * lr + li * li
    num_re = abar_re - 1.0
    num_im = abar_im
    coef_re = ((num_re * lr + num_im * li) / den)[:, None, :]
    coef_im = ((num_im * lr - num_re * li) / den)[:, None, :]
    br = b_re.astype(F32).transpose(0, 2, 1)
    bim = b_im.astype(F32).transpose(0, 2, 1)
    bb_re = coef_re * br - coef_im * bim
    bb_im = coef_re * bim + coef_im * br
    pw_re, pw_im = [jnp.ones_like(abar_re)], [jnp.zeros_like(abar_im)]
    for _ in range(T):
        r, i = pw_re[-1], pw_im[-1]
        pw_re.append(r * abar_re - i * abar_im)
        pw_im.append(r * abar_im + i * abar_re)
    pw_re = jnp.stack(pw_re, axis=1)[:, :, None, :]
    pw_im = jnp.stack(pw_im, axis=1)[:, :, None, :]
    cr = c_re.astype(F32)[:, None]
    ci = c_im.astype(F32)[:, None]
    ca_re = cr * pw_re - ci * pw_im
    ca_im = cr * pw_im + ci * pw_re
    cat = lambda re, im: jnp.concatenate([re.reshape(G, -1, P), im.reshape(G, -1, P)], axis=-1)
    m1 = jnp.einsum('gik,gjk->gij', cat(bb_re, bb_im), cat(ca_re[:, :T], -ca_im[:, :T]),
                    precision=lax.Precision.HIGHEST)
    m1 = m1.at[:, :, :H].add(jnp.eye(H, dtype=F32)[None] * d_skip.astype(F32)[:, None, :])
    toep = jnp.stack([jnp.pad(m1[:, :, :TH - H * t], ((0, 0), (0, 0), (H * t, 0))) for t in range(T)],
                     axis=1).reshape(G, TH, TH)
    rev_re = pw_re[:, :T][:, ::-1]
    rev_im = pw_im[:, :T][:, ::-1]
    w_re = (rev_re * bb_re[:, None] - rev_im * bb_im[:, None]).reshape(NQ, 2, TH, P)
    w_im = (rev_re * bb_im[:, None] + rev_im * bb_re[:, None]).reshape(NQ, 2, TH, P)
    zw = jnp.zeros((NQ, TH, P), F32)
    wst_p = jnp.concatenate([
        jnp.concatenate([w_re[:, 0], zw, w_im[:, 0], zw], axis=-1),
        jnp.concatenate([zw, w_re[:, 1], zw, w_im[:, 1]], axis=-1)], axis=1)
    tp = toep.reshape(NQ, 2, TH, TH)
    zt = jnp.zeros((NQ, TH, TH), F32)
    toep_p = jnp.concatenate([jnp.concatenate([tp[:, 0], zt], axis=-1),
                              jnp.concatenate([zt, tp[:, 1]], axis=-1)], axis=1)
    v_t = cat(ca_re[:, 1:], -ca_im[:, 1:]).transpose(0, 2, 1).reshape(NQ, 2, 2, P, TH)
    zv = jnp.zeros((NQ, P, TH), F32)
    vmat_p = jnp.concatenate([
        jnp.concatenate([v_t[:, 0, 0], zv], axis=-1), jnp.concatenate([zv, v_t[:, 1, 0]], axis=-1),
        jnp.concatenate([v_t[:, 0, 1], zv], axis=-1), jnp.concatenate([zv, v_t[:, 1, 1]], axis=-1)], axis=1)
    a_re = jnp.broadcast_to(pw_re[:, T].reshape(1, G * P), (SUBLANE, G * P))
    a_im = jnp.broadcast_to(pw_im[:, T].reshape(1, G * P), (SUBLANE, G * P))
    return wst_p.astype(BF16), a_re, a_im, toep_p.astype(BF16), vmat_p.astype(BF16)


def kernel(x, p, positions, g_mix_norm, w_in, g_q_lora, w_uq, g_kv_lora, w_ukv, g_q_head, g_k_head,
           lam_re, lam_im, log_dt, b_re, b_im, c_re, c_im, d_skip, w_glu, b_glu, g_out_attn,
           g_out_ssm, w_o, g_ffn_norm, w_gate, w_up, w_down, g_ple_norm, w_ple_gate, w_ple_proj):
    batch, seq, _ = x.shape
    assert batch == SUBLANE, "the S5 chunk recurrence keeps one batch row per sublane"
    depth = w_in.shape[0]
    n_tok = batch * seq
    row = lambda g: g[None, :].astype(F32)

    inv_freq = 1.0 / (ROPE_THETA ** (jnp.arange(0, QK_ROPE_DIM, 2, dtype=F32) / QK_ROPE_DIM))
    freq = _rope_tile(jnp.concatenate([inv_freq, inv_freq]))[None, :]
    ones = jnp.ones((ROPE_HALF,), F32)
    sgn = _rope_tile(jnp.concatenate([-ones, ones]))[None, :]
    pos2d = positions.reshape(n_tok, 1)

    x2d = x.reshape(n_tok, D_MODEL)
    for i in range(depth):
        o1, o2, o3 = Q_LORA, Q_LORA + KV_LORA, Q_LORA + KV_LORA + QK_ROPE_DIM
        wi = w_in[i]
        wmla = jnp.concatenate([wi[:, :o2].astype(BF16), _rope_tile(wi[:, o2:o3].astype(BF16))], axis=1)
        wssm = wi[:, o3:].astype(BF16)
        wq = w_uq[i].reshape(Q_LORA, N_HEADS, QK_HEAD_DIM)
        wuq = jnp.concatenate([wq[..., :QK_NOPE_DIM], _rope_tile(wq[..., QK_NOPE_DIM:])], axis=-1)
        wuq = wuq.reshape(Q_LORA, N_HEADS * HEAD_PAD).astype(BF16)
        wkv = w_ukv[i].reshape(KV_LORA, N_HEADS, QK_NOPE_DIM + V_HEAD_DIM)
        wukv = jnp.concatenate([wkv[..., :QK_NOPE_DIM].reshape(KV_LORA, -1),
                                wkv[..., QK_NOPE_DIM:].reshape(KV_LORA, -1)], axis=1).astype(BF16)

        q, k, v, u_slab = _inproj(x2d, pos2d, row(g_mix_norm[i]), wmla, wssm, row(g_q_lora[i]), wuq,
                                  row(g_kv_lora[i]), wukv, _head_gain(g_q_head[i]),
                                  _head_gain(g_k_head[i]), freq, sgn, batch=batch, seq=seq, tl=512)
        o_attn = _attention(q, k, v, batch=batch, seq=seq, tq=256)

        wst, a_re, a_im, toep, vmat = _s5_params(lam_re[i], lam_im[i], log_dt[i], b_re[i], b_im[i],
                                                 c_re[i], c_im[i], d_skip[i])
        o_ssm = _s5(u_slab, wst, a_re, a_im, toep, vmat,
                    w_glu[i].astype(BF16), row(b_glu[i]), row(g_out_ssm[i]),
                    batch=batch, seq=seq, chunks=16)

        wo = w_o[i].astype(BF16)
        x2d = _oproj(x2d, o_attn, o_ssm, row(g_out_attn[i]), wo[:ATTN_WIDTH], wo[ATTN_WIDTH:],
                     batch=batch, seq=seq, tl=1024)
        x2d = _ffn(x2d, row(g_ffn_norm[i]), w_gate[i].astype(BF16), w_up[i].astype(BF16),
                   w_down[i].astype(BF16), tm=1024, tf=512)
        x2d = _ple(x2d, p[i].reshape(n_tok, PLE_DIM), row(g_ple_norm[i]),
                   w_ple_gate[i].astype(BF16), w_ple_proj[i].astype(BF16), tm=1024)
    return x2d.reshape(batch, seq, D_MODEL)
```

```python
import functools

import jax
import jax.numpy as jnp
from jax import lax
from jax.experimental import pallas as pl
from jax.experimental.pallas import tpu as pltpu

D_MODEL = 2048
PLE_DIM = 256
N_HEADS = 8
QK_NOPE_DIM = 128
QK_ROPE_DIM = 64
V_HEAD_DIM = 128
QK_HEAD_DIM = QK_NOPE_DIM + QK_ROPE_DIM
Q_LORA = 512
KV_LORA = 256
ATTN_WIDTH = N_HEADS * V_HEAD_DIM
ROPE_THETA = 10000.0
SSM_WIDTH = 1024
SSM_GROUP = 16
SSM_GROUPS = SSM_WIDTH // SSM_GROUP
SSM_STATE = 64
D_FF = 5632
EPS = 1e-6

LANE = 128
SUBLANE = 8
HEAD_PAD = 2 * LANE
ROPE_HALF = QK_ROPE_DIM // 2
CHUNK = SUBLANE
N_LANE_TILES = SSM_WIDTH // LANE
GROUPS_PER_TILE = LANE // SSM_GROUP
PAIR_W = 2 * LANE
N_PAIRS = SSM_GROUPS // 2
VMEM_LIMIT = 56 * 1024 * 1024
LOG2_E = 1.4426950408889634
FFN_SUBTILES = 2
INPROJ_SUBTILES = 2
PLE_SUBTILES = 2
OPROJ_SUBTILES = 4
ATTN_HEADS_PER_STEP = 2
QK_AHEAD = 4

BF16 = jnp.bfloat16
F32 = jnp.float32


def _rms(t, g, width=None):
    n = t.shape[-1] if width is None else width
    ss = jnp.sum(t * t, axis=-1, keepdims=True) * (1.0 / n)
    return t * lax.rsqrt(ss + EPS) * g


def _dot(a, b):
    return jnp.dot(a, b, preferred_element_type=F32)


def _sigmoid(t):
    return 1.0 / (1.0 + jnp.exp(-t))


def _params(*sem):
    return pltpu.CompilerParams(dimension_semantics=sem, vmem_limit_bytes=VMEM_LIMIT)


def _inproj_kernel(x_ref, pos_ref, gmix_ref, win_ref, gql_ref, wuq_ref, gkvl_ref, wukv_ref,
                   gq_ref, gk_ref, freq_ref, sgn_ref,
                   q_ref, k_ref, v_ref, u_ref):
    sub = x_ref.shape[0] // INPROJ_SUBTILES
    o1, o2 = Q_LORA, Q_LORA + KV_LORA
    kr_tile = o2 // LANE
    gq = gq_ref[...]
    gk = gk_ref[...]
    scale = QK_HEAD_DIM ** -0.5 * LOG2_E
    lane = lax.broadcasted_iota(jnp.int32, (sub, LANE), 1)
    lo_half = lane < LANE // 2

    def project(s):
        rs = slice(s * sub, (s + 1) * sub)
        h = _rms(x_ref[rs, :], gmix_ref[...]).astype(BF16)
        z = _dot(h, win_ref[...])
        sw = [pltpu.roll(z[:, (kr_tile + m) * LANE:(kr_tile + m + 1) * LANE], LANE // 2, 1)
              for m in range(N_LANE_TILES + 1)]
        cs = slice(s * sub // CHUNK, (s + 1) * sub // CHUNK)
        for kt in range(N_LANE_TILES):
            u_ref[cs, kt, :, :] = jnp.where(lo_half, sw[kt], sw[kt + 1]).reshape(sub // CHUNK, CHUNK, LANE)
        zk = z[:, kr_tile * LANE:(kr_tile + 1) * LANE]
        kr = (jnp.where(lane < ROPE_HALF, zk, 0.0)
              + jnp.where((lane >= LANE // 2) & (lane < LANE // 2 + ROPE_HALF),
                          pltpu.roll(zk, ROPE_HALF, 1), 0.0))
        c_q = _rms(z[:, :o1], gql_ref[...]).astype(BF16)
        q = _dot(c_q, wuq_ref[...])
        c_kv = _rms(z[:, o1:o2], gkvl_ref[...]).astype(BF16)
        kv = _dot(c_kv, wukv_ref[...])
        return q, kv, kr

    def finish_heads(s, q, kv, kr):
        rs = slice(s * sub, (s + 1) * sub)
        v_ref[rs, :] = kv[:, ATTN_WIDTH:].astype(BF16)
        ang = pos_ref[rs, :].astype(F32) * freq_ref[...]
        cos_t = jnp.cos(ang)
        sin_t = jnp.sin(ang) * sgn_ref[...]

        def rope(t):
            return t * cos_t + pltpu.roll(t, LANE // 2, 1) * sin_t

        kr_ss = jnp.sum(kr * kr, axis=-1, keepdims=True)
        kr_rope = rope(kr * gk[:, LANE:])
        for hd in range(N_HEADS):
            qh = q[:, hd * HEAD_PAD:(hd + 1) * HEAD_PAD]
            qn = _rms(qh, gq, width=QK_HEAD_DIM) * scale
            q_ref[rs, hd * HEAD_PAD:hd * HEAD_PAD + LANE] = qn[:, :LANE].astype(BF16)
            q_ref[rs, hd * HEAD_PAD + LANE:(hd + 1) * HEAD_PAD] = rope(qn[:, LANE:]).astype(BF16)
            kn = kv[:, hd * QK_NOPE_DIM:(hd + 1) * QK_NOPE_DIM]
            ss = (jnp.sum(kn * kn, axis=-1, keepdims=True) + kr_ss) * (1.0 / QK_HEAD_DIM)
            rinv = lax.rsqrt(ss + EPS)
            k_ref[rs, hd * HEAD_PAD:hd * HEAD_PAD + LANE] = (kn * rinv * gk[:, :LANE]).astype(BF16)
            k_ref[rs, hd * HEAD_PAD + LANE:(hd + 1) * HEAD_PAD] = (kr_rope * rinv).astype(BF16)

    for s in range(INPROJ_SUBTILES):
        finish_heads(s, *project(s))


def _inproj(x2d, pos2d, gmix, win, gql, wuq, gkvl, wukv, gq, gk, freq, sgn, *, batch, seq, tl):
    nl = seq // tl
    tok = lambda w: pl.BlockSpec((tl, w), lambda b, i: (b * nl + i, 0))
    full = lambda a: pl.BlockSpec(a.shape, lambda b, i: (0,) * a.ndim)
    n_tok = batch * seq
    return pl.pallas_call(
        _inproj_kernel,
        grid=(batch, nl),
        in_specs=[tok(D_MODEL), tok(1), full(gmix), full(win), full(gql), full(wuq), full(gkvl),
                  full(wukv), full(gq), full(gk), full(freq), full(sgn)],
        out_specs=[tok(N_HEADS * HEAD_PAD), tok(N_HEADS * HEAD_PAD), tok(ATTN_WIDTH),
                   pl.BlockSpec((tl // CHUNK, N_LANE_TILES, CHUNK, LANE), lambda b, i: (i, 0, b, 0))],
        out_shape=[jax.ShapeDtypeStruct((n_tok, N_HEADS * HEAD_PAD), BF16),
                   jax.ShapeDtypeStruct((n_tok, N_HEADS * HEAD_PAD), BF16),
                   jax.ShapeDtypeStruct((n_tok, ATTN_WIDTH), BF16),
                   jax.ShapeDtypeStruct((seq // CHUNK, N_LANE_TILES, batch * CHUNK, LANE), F32)],
        compiler_params=_params("arbitrary", "arbitrary"),
    )(x2d, pos2d, gmix, win, gql, wuq, gkvl, wukv, gq, gk, freq, sgn)


def _attn_kernel(q_ref, k_ref, v_ref, o_ref, *, tq):
    seq = q_ref.shape[0]
    n = seq // tq
    vts = [v_ref[:, hd * V_HEAD_DIM:(hd + 1) * V_HEAD_DIM].T for hd in range(ATTN_HEADS_PER_STEP)]
    diag_mask = (lax.broadcasted_iota(jnp.int32, (tq, tq), 0)
                 <= lax.broadcasted_iota(jnp.int32, (tq, tq), 1))

    def scores_t(item):
        hd, i = item
        hs = slice(hd * HEAD_PAD, (hd + 1) * HEAD_PAD)
        return lax.dot_general(k_ref[:(i + 1) * tq, hs], q_ref[i * tq:(i + 1) * tq, hs],
                               (((1,), (1,)), ((), ())), preferred_element_type=F32)

    order = [(hd, i) for i in range(n - 1, -1, -1) for hd in range(ATTN_HEADS_PER_STEP)]
    sts = {item: scores_t(item) for item in order[:QK_AHEAD]}
    for pos, item in enumerate(order):
        hd, i = item
        kv_len = (i + 1) * tq
        st = sts.pop(item)
        if pos + QK_AHEAD < len(order):
            ahead = order[pos + QK_AHEAD]
            sts[ahead] = scores_t(ahead)
        sd = jnp.where(diag_mask, st[kv_len - tq:], -jnp.inf)
        m = jnp.max(sd, axis=0, keepdims=True)
        if i:
            m = jnp.maximum(m, jnp.max(st[:kv_len - tq], axis=0, keepdims=True))
        pd = jnp.exp2(sd - m)
        l = jnp.sum(pd, axis=0, keepdims=True)
        if i:
            pt = jnp.exp2(st[:kv_len - tq] - m)
            l = l + jnp.sum(pt, axis=0, keepdims=True)
            p = jnp.concatenate([pt.astype(BF16), pd.astype(BF16)], axis=0)
        else:
            p = pd.astype(BF16)
        ot = _dot(vts[hd][:, :kv_len], p)
        o_ref[i * tq:(i + 1) * tq, hd * V_HEAD_DIM:(hd + 1) * V_HEAD_DIM] = (ot / l).T.astype(o_ref.dtype)


def _attention(q, k, v, *, batch, seq, tq):
    hps = ATTN_HEADS_PER_STEP
    return pl.pallas_call(
        functools.partial(_attn_kernel, tq=tq),
        grid=(batch, N_HEADS // hps),
        in_specs=[pl.BlockSpec((seq, hps * HEAD_PAD), lambda b, h: (b, h)),
                  pl.BlockSpec((seq, hps * HEAD_PAD), lambda b, h: (b, h)),
                  pl.BlockSpec((seq, hps * V_HEAD_DIM), lambda b, h: (b, h))],
        out_specs=pl.BlockSpec((seq, hps * V_HEAD_DIM), lambda b, h: (b, h)),
        out_shape=jax.ShapeDtypeStruct((batch * seq, ATTN_WIDTH), BF16),
        compiler_params=_params("arbitrary", "arbitrary"),
    )(q, k, v)


def _block_transpose(tiles):
    lane = lax.broadcasted_iota(jnp.int32, tiles[0].shape, 1)
    for d in (4, 2, 1):
        hi = (lane & (SSM_GROUP * d)) != 0
        new = list(tiles)
        for i in range(GROUPS_PER_TILE):
            if i & d == 0:
                a, b = tiles[i], tiles[i + d]
                new[i] = jnp.where(hi, pltpu.roll(b, SSM_GROUP * d, 1), a)
                new[i + d] = jnp.where(hi, b, pltpu.roll(a, LANE - SSM_GROUP * d, 1))
        tiles = new
    return tiles


def _s5_kernel(u_ref, wst_ref, are_ref, aim_ref, toep_ref, vmat_ref, wglu_ref, bglu_ref,
               gout_ref, o_ref, state_ref, x_ref, xs_ref, ys_ref, yg_ref, *, chunks, batch):
    rows_x = chunks * batch
    rows = rows_x * CHUNK

    @pl.when(pl.program_id(0) == 0)
    def _():
        state_ref[...] = jnp.zeros_like(state_ref)

    for kt in range(N_LANE_TILES):
        tiles = [u_ref[:, pl.ds(kt, 1), pl.ds(t, batch, stride=CHUNK), :].reshape(rows_x, LANE)
                 for t in range(CHUNK)]
        outs = _block_transpose(tiles)
        for g in range(GROUPS_PER_TILE):
            c0 = (kt * GROUPS_PER_TILE + g) * LANE
            x_ref[:, c0:c0 + LANE] = outs[g].astype(BF16)

    for q in range(N_PAIRS):
        cs = slice(q * PAIR_W, (q + 1) * PAIR_W)
        xs_ref[:, cs] = _dot(x_ref[:, cs], wst_ref[q])

    def step(c, carry):
        r0 = pl.multiple_of(c * batch, batch)
        for q in range(N_PAIRS):
            re = slice(q * PAIR_W, q * PAIR_W + LANE)
            im = slice(q * PAIR_W + LANE, (q + 1) * PAIR_W)
            al = slice(q * LANE, (q + 1) * LANE)
            s_re = state_ref[:, re]
            s_im = state_ref[:, im]
            a_re = are_ref[:, al]
            a_im = aim_ref[:, al]
            x_re = xs_ref[pl.ds(r0, batch), re]
            x_im = xs_ref[pl.ds(r0, batch), im]
            xs_ref[pl.ds(r0, batch), re] = s_re
            xs_ref[pl.ds(r0, batch), im] = s_im
            state_ref[:, re] = a_re * s_re - a_im * s_im + x_re
            state_ref[:, im] = a_re * s_im + a_im * s_re + x_im
        return carry

    lax.fori_loop(0, chunks, step, 0)

    for q in range(N_PAIRS):
        cs = slice(q * PAIR_W, (q + 1) * PAIR_W)
        xs_ref[:, cs] = (_dot(x_ref[:, cs], toep_ref[q])
                         + _dot(xs_ref[:, cs].astype(BF16), vmat_ref[q]))

    for kt in range(N_LANE_TILES):
        tiles = [xs_ref[:, (kt * GROUPS_PER_TILE + g) * LANE:(kt * GROUPS_PER_TILE + g + 1) * LANE]
                 for g in range(GROUPS_PER_TILE)]
        outs = _block_transpose(tiles)
        for t in range(CHUNK):
            ys_ref[:, pl.ds(kt, 1), pl.ds(t, batch, stride=CHUNK), :] = outs[t].reshape(chunks, 1, batch, LANE)

    for kt in range(N_LANE_TILES):
        yg_ref[:, kt * LANE:(kt + 1) * LANE] = jax.nn.gelu(ys_ref[:, kt].reshape(rows, LANE), approximate=True)
    y = yg_ref[...]
    gate = _sigmoid(_dot(y.astype(BF16), wglu_ref[...]) + bglu_ref[...])
    o_ref[...] = _rms(y * gate, gout_ref[...]).reshape(chunks, batch, CHUNK, SSM_WIDTH)


def _s5(u_slab, wst, a_re, a_im, toep, vmat, wglu, bglu, gout, *, batch, seq, chunks):
    n_chunks = seq // CHUNK
    rows_x = chunks * batch
    full = lambda a: pl.BlockSpec(a.shape, lambda i: (0,) * a.ndim, pipeline_mode=pl.Buffered(1))
    return pl.pallas_call(
        functools.partial(_s5_kernel, chunks=chunks, batch=batch),
        grid=(n_chunks // chunks,),
        in_specs=[pl.BlockSpec((chunks, N_LANE_TILES, batch * CHUNK, LANE), lambda i: (i, 0, 0, 0)),
                  full(wst), full(a_re), full(a_im), full(toep), full(vmat), full(wglu),
                  full(bglu), full(gout)],
        out_specs=pl.BlockSpec((chunks, batch, CHUNK, SSM_WIDTH), lambda i: (i, 0, 0, 0)),
        out_shape=jax.ShapeDtypeStruct((n_chunks, batch, CHUNK, SSM_WIDTH), F32),
        scratch_shapes=[pltpu.VMEM((batch, N_PAIRS * PAIR_W), F32),
                        pltpu.VMEM((rows_x, N_PAIRS * PAIR_W), BF16),
                        pltpu.VMEM((rows_x, N_PAIRS * PAIR_W), F32),
                        pltpu.VMEM((chunks, N_LANE_TILES, batch * CHUNK, LANE), F32),
                        pltpu.VMEM((rows_x * CHUNK, SSM_WIDTH), F32)],
        compiler_params=_params("arbitrary"),
    )(u_slab, wst, a_re, a_im, toep, vmat, wglu, bglu, gout)


def _oproj_kernel(x_ref, oa_ref, os_ref, ga_ref, woa_ref, wos_ref, out_ref):
    sub = x_ref.shape[0] // OPROJ_SUBTILES
    for s in range(OPROJ_SUBTILES):
        rs = slice(s * sub, (s + 1) * sub)
        oa = _rms(oa_ref[rs, :].astype(F32), ga_ref[...]).astype(BF16)
        os_ = os_ref[s * sub // CHUNK:(s + 1) * sub // CHUNK].reshape(sub, SSM_WIDTH).astype(BF16)
        out_ref[rs, :] = x_ref[rs, :] + _dot(oa, woa_ref[...]) + _dot(os_, wos_ref[...])


def _oproj(x2d, o_attn, o_ssm, g_attn, wo_a, wo_s, *, batch, seq, tl):
    nl = seq // tl
    tok = lambda w: pl.BlockSpec((tl, w), lambda b, i: (b * nl + i, 0))
    full = lambda a: pl.BlockSpec(a.shape, lambda b, i: (0,) * a.ndim)
    return pl.pallas_call(
        _oproj_kernel,
        grid=(batch, nl),
        in_specs=[tok(D_MODEL), tok(ATTN_WIDTH),
                  pl.BlockSpec((tl // CHUNK, None, CHUNK, SSM_WIDTH), lambda b, i: (i, b, 0, 0)),
                  full(g_attn), full(wo_a), full(wo_s)],
        out_specs=tok(D_MODEL),
        out_shape=jax.ShapeDtypeStruct((batch * seq, D_MODEL), F32),
        compiler_params=_params("arbitrary", "arbitrary"),
    )(x2d, o_attn, o_ssm, g_attn, wo_a, wo_s)


def _ffn_kernel(x_ref, g_ref, wg_ref, wu_ref, wd_ref, out_ref, h_ref):
    j = pl.program_id(1)

    def ff_block(h):
        gate = _dot(h, wg_ref[...])
        up = _dot(h, wu_ref[...])
        act = (gate * _sigmoid(gate) * up).astype(BF16)
        return _dot(act, wd_ref[...])

    @pl.when(j == 0)
    def _():
        sub = x_ref.shape[0] // FFN_SUBTILES
        for s in range(FFN_SUBTILES):
            rs = slice(s * sub, (s + 1) * sub)
            x = x_ref[rs, :]
            h = _rms(x, g_ref[...]).astype(BF16)
            h_ref[rs, :] = h
            out_ref[rs, :] = x + ff_block(h)

    @pl.when(j > 0)
    def _():
        out_ref[...] += ff_block(h_ref[...])


def _ffn(x2d, g, wg, wu, wd, *, tm, tf):
    n_tok = x2d.shape[0]
    return pl.pallas_call(
        _ffn_kernel,
        grid=(n_tok // tm, D_FF // tf),
        in_specs=[pl.BlockSpec((tm, D_MODEL), lambda i, j: (i, 0)),
                  pl.BlockSpec((1, D_MODEL), lambda i, j: (0, 0)),
                  pl.BlockSpec((D_MODEL, tf), lambda i, j: (0, j)),
                  pl.BlockSpec((D_MODEL, tf), lambda i, j: (0, j)),
                  pl.BlockSpec((tf, D_MODEL), lambda i, j: (j, 0))],
        out_specs=pl.BlockSpec((tm, D_MODEL), lambda i, j: (i, 0)),
        out_shape=jax.ShapeDtypeStruct((n_tok, D_MODEL), F32),
        scratch_shapes=[pltpu.VMEM((tm, D_MODEL), BF16)],
        compiler_params=_params("arbitrary", "arbitrary"),
    )(x2d, g, wg, wu, wd)


def _ple_kernel(x_ref, p_ref, g_ref, wpg_ref, wpp_ref, out_ref):
    sub = x_ref.shape[0] // PLE_SUBTILES
    for s in range(PLE_SUBTILES):
        rs = slice(s * sub, (s + 1) * sub)
        x = x_ref[rs, :]
        h = _rms(x, g_ref[...]).astype(BF16)
        gate = _sigmoid(_dot(h, wpg_ref[...]))
        out_ref[rs, :] = x + gate * _dot(p_ref[rs, :].astype(BF16), wpp_ref[...])


def _ple(x2d, p2d, g, wpg, wpp, *, tm):
    n_tok = x2d.shape[0]
    full = lambda a: pl.BlockSpec(a.shape, lambda i: (0,) * a.ndim)
    return pl.pallas_call(
        _ple_kernel,
        grid=(n_tok // tm,),
        in_specs=[pl.BlockSpec((tm, D_MODEL), lambda i: (i, 0)),
                  pl.BlockSpec((tm, PLE_DIM), lambda i: (i, 0)),
                  full(g), full(wpg), full(wpp)],
        out_specs=pl.BlockSpec((tm, D_MODEL), lambda i: (i, 0)),
        out_shape=jax.ShapeDtypeStruct((n_tok, D_MODEL), F32),
        compiler_params=_params("arbitrary"),
    )(x2d, p2d, g, wpg, wpp)


def _rope_tile(t):
    z = jnp.zeros(t.shape[:-1] + (ROPE_HALF,), t.dtype)
    return jnp.concatenate([t[..., :ROPE_HALF], z, t[..., ROPE_HALF:], z], axis=-1)


def _head_gain(g):
    return jnp.concatenate([g[:QK_NOPE_DIM], _rope_tile(g[QK_NOPE_DIM:])])[None, :].astype(F32)


def _s5_params(lam_re, lam_im, log_dt, b_re, b_im, c_re, c_im, d_skip):
    G, P, H, T, NQ = SSM_GROUPS, SSM_STATE, SSM_GROUP, CHUNK, N_PAIRS
    TH = T * H
    lr = jnp.minimum(lam_re.astype(F32), -1e-4)
    li = lam_im.astype(F32)
    dt = jnp.exp(log_dt.astype(F32))[:, None]
    mag = jnp.exp(lr * dt)
    abar_re = mag * jnp.cos(li * dt)
    abar_im = mag * jnp.sin(li * dt)
    den = lr * lr + li * li
    num_re = abar_re - 1.0
    num_im = abar_im
    coef_re = ((num_re * lr + num_im * li) / den)[:, None, :]
    coef_im = ((num_im * lr - num_re * li) / den)[:, None, :]
    br = b_re.astype(F32).transpose(0, 2, 1)
    bim = b_im.astype(F32).transpose(0, 2, 1)
    bb_re = coef_re * br - coef_im * bim
    bb_im = coef_re * bim + coef_im * br
    pw_re, pw_im = [jnp.ones_like(abar_re)], [jnp.zeros_like(abar_im)]
    for _ in range(T):
        r, i = pw_re[-1], pw_im[-1]
        pw_re.append(r * abar_re - i * abar_im)
        pw_im.append(r * abar_im + i * abar_re)
    pw_re = jnp.stack(pw_re, axis=1)[:, :, None, :]
    pw_im = jnp.stack(pw_im, axis=1)[:, :, None, :]
    cr = c_re.astype(F32)[:, None]
    ci = c_im.astype(F32)[:, None]
    ca_re = cr * pw_re - ci * pw_im
    ca_im = cr * pw_im + ci * pw_re
    cat = lambda re, im: jnp.concatenate([re.reshape(G, -1, P), im.reshape(G, -1, P)], axis=-1)
    m1 = jnp.einsum('gik,gjk->gij', cat(bb_re, bb_im), cat(ca_re[:, :T], -ca_im[:, :T]),
                    precision=lax.Precision.HIGHEST)
    m1 = m1.at[:, :, :H].add(jnp.eye(H, dtype=F32)[None] * d_skip.astype(F32)[:, None, :])
    toep = jnp.stack([jnp.pad(m1[:, :, :TH - H * t], ((0, 0), (0, 0), (H * t, 0))) for t in range(T)],
                     axis=1).reshape(G, TH, TH)
    rev_re = pw_re[:, :T][:, ::-1]
    rev_im = pw_im[:, :T][:, ::-1]
    w_re = (rev_re * bb_re[:, None] - rev_im * bb_im[:, None]).reshape(NQ, 2, TH, P)
    w_im = (rev_re * bb_im[:, None] + rev_im * bb_re[:, None]).reshape(NQ, 2, TH, P)
    zw = jnp.zeros((NQ, TH, P), F32)
    wst_p = jnp.concatenate([
        jnp.concatenate([w_re[:, 0], zw, w_im[:, 0], zw], axis=-1),
        jnp.concatenate([zw, w_re[:, 1], zw, w_im[:, 1]], axis=-1)], axis=1)
    tp = toep.reshape(NQ, 2, TH, TH)
    zt = jnp.zeros((NQ, TH, TH), F32)
    toep_p = jnp.concatenate([jnp.concatenate([tp[:, 0], zt], axis=-1),
                              jnp.concatenate([zt, tp[:, 1]], axis=-1)], axis=1)
    v_t = cat(ca_re[:, 1:], -ca_im[:, 1:]).transpose(0, 2, 1).reshape(NQ, 2, 2, P, TH)
    zv = jnp.zeros((NQ, P, TH), F32)
    vmat_p = jnp.concatenate([
        jnp.concatenate([v_t[:, 0, 0], zv], axis=-1), jnp.concatenate([zv, v_t[:, 1, 0]], axis=-1),
        jnp.concatenate([v_t[:, 0, 1], zv], axis=-1), jnp.concatenate([zv, v_t[:, 1, 1]], axis=-1)], axis=1)
    a_re = jnp.broadcast_to(pw_re[:, T].reshape(1, G * P), (SUBLANE, G * P))
    a_im = jnp.broadcast_to(pw_im[:, T].reshape(1, G * P), (SUBLANE, G * P))
    return wst_p.astype(BF16), a_re, a_im, toep_p.astype(BF16), vmat_p.astype(BF16)


def kernel(x, p, positions, g_mix_norm, w_in, g_q_lora, w_uq, g_kv_lora, w_ukv, g_q_head, g_k_head,
           lam_re, lam_im, log_dt, b_re, b_im, c_re, c_im, d_skip, w_glu, b_glu, g_out_attn,
           g_out_ssm, w_o, g_ffn_norm, w_gate, w_up, w_down, g_ple_norm, w_ple_gate, w_ple_proj):
    batch, seq, _ = x.shape
    assert batch == SUBLANE, "the S5 chunk recurrence keeps one batch row per sublane"
    depth = w_in.shape[0]
    n_tok = batch * seq
    row = lambda g: g[None, :].astype(F32)

    inv_freq = 1.0 / (ROPE_THETA ** (jnp.arange(0, QK_ROPE_DIM, 2, dtype=F32) / QK_ROPE_DIM))
    freq = _rope_tile(jnp.concatenate([inv_freq, inv_freq]))[None, :]
    ones = jnp.ones((ROPE_HALF,), F32)
    sgn = _rope_tile(jnp.concatenate([-ones, ones]))[None, :]
    pos2d = positions.reshape(n_tok, 1)

    x2d = x.reshape(n_tok, D_MODEL)
    for i in range(depth):
        win = jnp.pad(w_in[i].astype(BF16), ((0, 0), (0, LANE // 2)))
        wq = w_uq[i].reshape(Q_LORA, N_HEADS, QK_HEAD_DIM)
        wuq = jnp.concatenate([wq[..., :QK_NOPE_DIM], _rope_tile(wq[..., QK_NOPE_DIM:])], axis=-1)
        wuq = wuq.reshape(Q_LORA, N_HEADS * HEAD_PAD).astype(BF16)
        wkv = w_ukv[i].reshape(KV_LORA, N_HEADS, QK_NOPE_DIM + V_HEAD_DIM)
        wukv = jnp.concatenate([wkv[..., :QK_NOPE_DIM].reshape(KV_LORA, -1),
                                wkv[..., QK_NOPE_DIM:].reshape(KV_LORA, -1)], axis=1).astype(BF16)

        q, k, v, u_slab = _inproj(x2d, pos2d, row(g_mix_norm[i]), win, row(g_q_lora[i]), wuq,
                                  row(g_kv_lora[i]), wukv, _head_gain(g_q_head[i]),
                                  _head_gain(g_k_head[i]), freq, sgn, batch=batch, seq=seq, tl=512)
        o_attn = _attention(q, k, v, batch=batch, seq=seq, tq=256)

        wst, a_re, a_im, toep, vmat = _s5_params(lam_re[i], lam_im[i], log_dt[i], b_re[i], b_im[i],
                                                 c_re[i], c_im[i], d_skip[i])
        o_ssm = _s5(u_slab, wst, a_re, a_im, toep, vmat,
                    w_glu[i].astype(BF16), row(b_glu[i]), row(g_out_ssm[i]),
                    batch=batch, seq=seq, chunks=16)

        wo = w_o[i].astype(BF16)
        x2d = _oproj(x2d, o_attn, o_ssm, row(g_out_attn[i]), wo[:ATTN_WIDTH], wo[ATTN_WIDTH:],
                     batch=batch, seq=seq, tl=1024)
        x2d = _ffn(x2d, row(g_ffn_norm[i]), w_gate[i].astype(BF16), w_up[i].astype(BF16),
                   w_down[i].astype(BF16), tm=1024, tf=512)
        x2d = _ple(x2d, p[i].reshape(n_tok, PLE_DIM), row(g_ple_norm[i]),
                   w_ple_gate[i].astype(BF16), w_ple_proj[i].astype(BF16), tm=1024)
    return x2d.reshape(batch, seq, D_MODEL)
```

```python
import functools

import jax
import jax.numpy as jnp
from jax import lax
from jax.experimental import pallas as pl
from jax.experimental.pallas import tpu as pltpu

D_MODEL = 2048
PLE_DIM = 256
N_HEADS = 8
QK_NOPE_DIM = 128
QK_ROPE_DIM = 64
V_HEAD_DIM = 128
QK_HEAD_DIM = QK_NOPE_DIM + QK_ROPE_DIM
Q_LORA = 512
KV_LORA = 256
ATTN_WIDTH = N_HEADS * V_HEAD_DIM
ROPE_THETA = 10000.0
SSM_WIDTH = 1024
SSM_GROUP = 16
SSM_GROUPS = SSM_WIDTH // SSM_GROUP
SSM_STATE = 64
D_FF = 5632
EPS = 1e-6

LANE = 128
SUBLANE = 8
HEAD_PAD = 2 * LANE
ROPE_HALF = QK_ROPE_DIM // 2
CHUNK = SUBLANE
N_LANE_TILES = SSM_WIDTH // LANE
GROUPS_PER_TILE = LANE // SSM_GROUP
PAIR_W = 2 * LANE
N_PAIRS = SSM_GROUPS // 2
VMEM_LIMIT = 56 * 1024 * 1024
LOG2_E = 1.4426950408889634
FFN_SUBTILES = 2
INPROJ_SUBTILES = 2
PLE_SUBTILES = 2
OPROJ_SUBTILES = 4
ATTN_HEADS_PER_STEP = 2
QK_AHEAD = 4

BF16 = jnp.bfloat16
F32 = jnp.float32


def _rms(t, g, width=None):
    n = t.shape[-1] if width is None else width
    ss = jnp.sum(t * t, axis=-1, keepdims=True) * (1.0 / n)
    return t * lax.rsqrt(ss + EPS) * g


def _dot(a, b):
    return jnp.dot(a, b, preferred_element_type=F32)


def _sigmoid(t):
    return 1.0 / (1.0 + jnp.exp(-t))


def _params(*sem):
    return pltpu.CompilerParams(dimension_semantics=sem, vmem_limit_bytes=VMEM_LIMIT)


def _inproj_kernel(x_ref, pos_ref, gmix_ref, win_ref, gql_ref, wuq_ref, gkvl_ref, wukv_ref,
                   gq_ref, gk_ref, freq_ref, sgn_ref,
                   q_ref, k_ref, v_ref, u_ref):
    sub = x_ref.shape[0] // INPROJ_SUBTILES
    o1, o2 = Q_LORA, Q_LORA + KV_LORA
    kr_tile = o2 // LANE
    gq = gq_ref[...]
    gk = gk_ref[...]
    scale = QK_HEAD_DIM ** -0.5 * LOG2_E
    lane = lax.broadcasted_iota(jnp.int32, (sub, LANE), 1)
    lo_half = lane < LANE // 2

    def project(s):
        rs = slice(s * sub, (s + 1) * sub)
        h = _rms(x_ref[rs, :], gmix_ref[...]).astype(BF16)
        z = _dot(h, win_ref[...])
        tiles = [z[:, (kr_tile + m) * LANE:(kr_tile + m + 1) * LANE] for m in range(N_LANE_TILES)]
        last = z[:, (kr_tile + N_LANE_TILES) * LANE:]
        tiles.append(jnp.concatenate([last, jnp.zeros_like(last)], axis=1))
        sw = [pltpu.roll(t, LANE // 2, 1) for t in tiles]
        cs = slice(s * sub // CHUNK, (s + 1) * sub // CHUNK)
        for kt in range(N_LANE_TILES):
            u_ref[cs, kt, :, :] = jnp.where(lo_half, sw[kt], sw[kt + 1]).reshape(sub // CHUNK, CHUNK, LANE)
        zk = z[:, kr_tile * LANE:(kr_tile + 1) * LANE]
        kr = (jnp.where(lane < ROPE_HALF, zk, 0.0)
              + jnp.where((lane >= LANE // 2) & (lane < LANE // 2 + ROPE_HALF),
                          pltpu.roll(zk, ROPE_HALF, 1), 0.0))
        c_q = _rms(z[:, :o1], gql_ref[...]).astype(BF16)
        q = _dot(c_q, wuq_ref[...])
        c_kv = _rms(z[:, o1:o2], gkvl_ref[...]).astype(BF16)
        kv = _dot(c_kv, wukv_ref[...])
        return q, kv, kr

    def finish_heads(s, q, kv, kr):
        rs = slice(s * sub, (s + 1) * sub)
        v_ref[rs, :] = kv[:, ATTN_WIDTH:].astype(BF16)
        ang = pos_ref[rs, :].astype(F32) * freq_ref[...]
        cos_t = jnp.cos(ang)
        sin_t = jnp.sin(ang) * sgn_ref[...]

        def rope(t):
            return t * cos_t + pltpu.roll(t, LANE // 2, 1) * sin_t

        kr_ss = jnp.sum(kr * kr, axis=-1, keepdims=True)
        kr_rope = rope(kr * gk[:, LANE:])
        for hd in range(N_HEADS):
            qh = q[:, hd * HEAD_PAD:(hd + 1) * HEAD_PAD]
            qn = _rms(qh, gq, width=QK_HEAD_DIM) * scale
            q_ref[rs, hd * HEAD_PAD:hd * HEAD_PAD + LANE] = qn[:, :LANE].astype(BF16)
            q_ref[rs, hd * HEAD_PAD + LANE:(hd + 1) * HEAD_PAD] = rope(qn[:, LANE:]).astype(BF16)
            kn = kv[:, hd * QK_NOPE_DIM:(hd + 1) * QK_NOPE_DIM]
            ss = (jnp.sum(kn * kn, axis=-1, keepdims=True) + kr_ss) * (1.0 / QK_HEAD_DIM)
            rinv = lax.rsqrt(ss + EPS)
            k_ref[rs, hd * HEAD_PAD:hd * HEAD_PAD + LANE] = (kn * rinv * gk[:, :LANE]).astype(BF16)
            k_ref[rs, hd * HEAD_PAD + LANE:(hd + 1) * HEAD_PAD] = (kr_rope * rinv).astype(BF16)

    for s in range(INPROJ_SUBTILES):
        finish_heads(s, *project(s))


def _inproj(x2d, pos2d, gmix, win, gql, wuq, gkvl, wukv, gq, gk, freq, sgn, *, batch, seq, tl):
    nl = seq // tl
    tok = lambda w: pl.BlockSpec((tl, w), lambda b, i: (b * nl + i, 0))
    full = lambda a: pl.BlockSpec(a.shape, lambda b, i: (0,) * a.ndim)
    n_tok = batch * seq
    return pl.pallas_call(
        _inproj_kernel,
        grid=(batch, nl),
        in_specs=[tok(D_MODEL), tok(1), full(gmix), full(win), full(gql), full(wuq), full(gkvl),
                  full(wukv), full(gq), full(gk), full(freq), full(sgn)],
        out_specs=[tok(N_HEADS * HEAD_PAD), tok(N_HEADS * HEAD_PAD), tok(ATTN_WIDTH),
                   pl.BlockSpec((tl // CHUNK, N_LANE_TILES, CHUNK, LANE), lambda b, i: (i, 0, b, 0))],
        out_shape=[jax.ShapeDtypeStruct((n_tok, N_HEADS * HEAD_PAD), BF16),
                   jax.ShapeDtypeStruct((n_tok, N_HEADS * HEAD_PAD), BF16),
                   jax.ShapeDtypeStruct((n_tok, ATTN_WIDTH), BF16),
                   jax.ShapeDtypeStruct((seq // CHUNK, N_LANE_TILES, batch * CHUNK, LANE), F32)],
        compiler_params=_params("arbitrary", "arbitrary"),
    )(x2d, pos2d, gmix, win, gql, wuq, gkvl, wukv, gq, gk, freq, sgn)


def _attn_kernel(q_ref, k_ref, v_ref, o_ref, *, tq):
    seq = q_ref.shape[0]
    n = seq // tq
    vts = [v_ref[:, hd * V_HEAD_DIM:(hd + 1) * V_HEAD_DIM].T for hd in range(ATTN_HEADS_PER_STEP)]
    diag_mask = (lax.broadcasted_iota(jnp.int32, (tq, tq), 0)
                 <= lax.broadcasted_iota(jnp.int32, (tq, tq), 1))

    def scores_t(item):
        hd, i = item
        hs = slice(hd * HEAD_PAD, (hd + 1) * HEAD_PAD)
        return lax.dot_general(k_ref[:(i + 1) * tq, hs], q_ref[i * tq:(i + 1) * tq, hs],
                               (((1,), (1,)), ((), ())), preferred_element_type=F32)

    order = [(hd, i) for i in range(n - 1, -1, -1) for hd in range(ATTN_HEADS_PER_STEP)]
    sts = {item: scores_t(item) for item in order[:QK_AHEAD]}
    for pos, item in enumerate(order):
        hd, i = item
        kv_len = (i + 1) * tq
        st = sts.pop(item)
        if pos + QK_AHEAD < len(order):
            ahead = order[pos + QK_AHEAD]
            sts[ahead] = scores_t(ahead)
        sd = jnp.where(diag_mask, st[kv_len - tq:], -jnp.inf)
        m = jnp.max(sd, axis=0, keepdims=True)
        if i:
            m = jnp.maximum(m, jnp.max(st[:kv_len - tq], axis=0, keepdims=True))
        pd = jnp.exp2(sd - m)
        l = jnp.sum(pd, axis=0, keepdims=True)
        if i:
            pt = jnp.exp2(st[:kv_len - tq] - m)
            l = l + jnp.sum(pt, axis=0, keepdims=True)
            p = jnp.concatenate([pt.astype(BF16), pd.astype(BF16)], axis=0)
        else:
            p = pd.astype(BF16)
        ot = _dot(vts[hd][:, :kv_len], p)
        o_ref[i * tq:(i + 1) * tq, hd * V_HEAD_DIM:(hd + 1) * V_HEAD_DIM] = (ot / l).T.astype(o_ref.dtype)


def _attention(q, k, v, *, batch, seq, tq):
    hps = ATTN_HEADS_PER_STEP
    return pl.pallas_call(
        functools.partial(_attn_kernel, tq=tq),
        grid=(batch, N_HEADS // hps),
        in_specs=[pl.BlockSpec((seq, hps * HEAD_PAD), lambda b, h: (b, h)),
                  pl.BlockSpec((seq, hps * HEAD_PAD), lambda b, h: (b, h)),
                  pl.BlockSpec((seq, hps * V_HEAD_DIM), lambda b, h: (b, h))],
        out_specs=pl.BlockSpec((seq, hps * V_HEAD_DIM), lambda b, h: (b, h)),
        out_shape=jax.ShapeDtypeStruct((batch * seq, ATTN_WIDTH), BF16),
        compiler_params=_params("arbitrary", "arbitrary"),
    )(q, k, v)


def _block_transpose(tiles):
    lane = lax.broadcasted_iota(jnp.int32, tiles[0].shape, 1)
    for d in (4, 2, 1):
        hi = (lane & (SSM_GROUP * d)) != 0
        new = list(tiles)
        for i in range(GROUPS_PER_TILE):
            if i & d == 0:
                a, b = tiles[i], tiles[i + d]
                new[i] = jnp.where(hi, pltpu.roll(b, SSM_GROUP * d, 1), a)
                new[i + d] = jnp.where(hi, b, pltpu.roll(a, LANE - SSM_GROUP * d, 1))
        tiles = new
    return tiles


def _s5_kernel(u_ref, wst_ref, are_ref, aim_ref, toep_ref, vmat_ref, wglu_ref, bglu_ref,
               gout_ref, o_ref, state_ref, x_ref, xs_ref, ys_ref, yg_ref, *, chunks, batch):
    rows_x = chunks * batch
    rows = rows_x * CHUNK

    @pl.when(pl.program_id(0) == 0)
    def _():
        state_ref[...] = jnp.zeros_like(state_ref)

    for kt in range(N_LANE_TILES):
        tiles = [u_ref[:, pl.ds(kt, 1), pl.ds(t, batch, stride=CHUNK), :].reshape(rows_x, LANE)
                 for t in range(CHUNK)]
        outs = _block_transpose(tiles)
        for g in range(GROUPS_PER_TILE):
            c0 = (kt * GROUPS_PER_TILE + g) * LANE
            x_ref[:, c0:c0 + LANE] = outs[g].astype(BF16)

    for q in range(N_PAIRS):
        cs = slice(q * PAIR_W, (q + 1) * PAIR_W)
        xs_ref[:, cs] = _dot(x_ref[:, cs], wst_ref[q])

    def step(c, carry):
        r0 = pl.multiple_of(c * batch, batch)
        for q in range(N_PAIRS):
            re = slice(q * PAIR_W, q * PAIR_W + LANE)
            im = slice(q * PAIR_W + LANE, (q + 1) * PAIR_W)
            al = slice(q * LANE, (q + 1) * LANE)
            s_re = state_ref[:, re]
            s_im = state_ref[:, im]
            a_re = are_ref[:, al]
            a_im = aim_ref[:, al]
            x_re = xs_ref[pl.ds(r0, batch), re]
            x_im = xs_ref[pl.ds(r0, batch), im]
            xs_ref[pl.ds(r0, batch), re] = s_re
            xs_ref[pl.ds(r0, batch), im] = s_im
            state_ref[:, re] = a_re * s_re - a_im * s_im + x_re
            state_ref[:, im] = a_re * s_im + a_im * s_re + x_im
        return carry

    lax.fori_loop(0, chunks, step, 0)

    for q in range(N_PAIRS):
        cs = slice(q * PAIR_W, (q + 1) * PAIR_W)
        xs_ref[:, cs] = (_dot(x_ref[:, cs], toep_ref[q])
                         + _dot(xs_ref[:, cs].astype(BF16), vmat_ref[q]))

    for kt in range(N_LANE_TILES):
        tiles = [xs_ref[:, (kt * GROUPS_PER_TILE + g) * LANE:(kt * GROUPS_PER_TILE + g + 1) * LANE]
                 for g in range(GROUPS_PER_TILE)]
        outs = _block_transpose(tiles)
        for t in range(CHUNK):
            ys_ref[:, pl.ds(kt, 1), pl.ds(t, batch, stride=CHUNK), :] = outs[t].reshape(chunks, 1, batch, LANE)

    for kt in range(N_LANE_TILES):
        yg_ref[:, kt * LANE:(kt + 1) * LANE] = jax.nn.gelu(ys_ref[:, kt].reshape(rows, LANE), approximate=True)
    y = yg_ref[...]
    gate = _sigmoid(_dot(y.astype(BF16), wglu_ref[...]) + bglu_ref[...])
    o_ref[...] = _rms(y * gate, gout_ref[...]).reshape(chunks, batch, CHUNK, SSM_WIDTH)


def _s5(u_slab, wst, a_re, a_im, toep, vmat, wglu, bglu, gout, *, batch, seq, chunks):
    n_chunks = seq // CHUNK
    rows_x = chunks * batch
    full = lambda a: pl.BlockSpec(a.shape, lambda i: (0,) * a.ndim, pipeline_mode=pl.Buffered(1))
    return pl.pallas_call(
        functools.partial(_s5_kernel, chunks=chunks, batch=batch),
        grid=(n_chunks // chunks,),
        in_specs=[pl.BlockSpec((chunks, N_LANE_TILES, batch * CHUNK, LANE), lambda i: (i, 0, 0, 0)),
                  full(wst), full(a_re), full(a_im), full(toep), full(vmat), full(wglu),
                  full(bglu), full(gout)],
        out_specs=pl.BlockSpec((chunks, batch, CHUNK, SSM_WIDTH), lambda i: (i, 0, 0, 0)),
        out_shape=jax.ShapeDtypeStruct((n_chunks, batch, CHUNK, SSM_WIDTH), F32),
        scratch_shapes=[pltpu.VMEM((batch, N_PAIRS * PAIR_W), F32),
                        pltpu.VMEM((rows_x, N_PAIRS * PAIR_W), BF16),
                        pltpu.VMEM((rows_x, N_PAIRS * PAIR_W), F32),
                        pltpu.VMEM((chunks, N_LANE_TILES, batch * CHUNK, LANE), F32),
                        pltpu.VMEM((rows_x * CHUNK, SSM_WIDTH), F32)],
        compiler_params=_params("arbitrary"),
    )(u_slab, wst, a_re, a_im, toep, vmat, wglu, bglu, gout)


def _oproj_kernel(x_ref, oa_ref, os_ref, ga_ref, woa_ref, wos_ref, out_ref):
    sub = x_ref.shape[0] // OPROJ_SUBTILES
    for s in range(OPROJ_SUBTILES):
        rs = slice(s * sub, (s + 1) * sub)
        oa = _rms(oa_ref[rs, :].astype(F32), ga_ref[...]).astype(BF16)
        os_ = os_ref[s * sub // CHUNK:(s + 1) * sub // CHUNK].reshape(sub, SSM_WIDTH).astype(BF16)
        out_ref[rs, :] = x_ref[rs, :] + _dot(oa, woa_ref[...]) + _dot(os_, wos_ref[...])


def _oproj(x2d, o_attn, o_ssm, g_attn, wo_a, wo_s, *, batch, seq, tl):
    nl = seq // tl
    tok = lambda w: pl.BlockSpec((tl, w), lambda b, i: (b * nl + i, 0))
    full = lambda a: pl.BlockSpec(a.shape, lambda b, i: (0,) * a.ndim)
    return pl.pallas_call(
        _oproj_kernel,
        grid=(batch, nl),
        in_specs=[tok(D_MODEL), tok(ATTN_WIDTH),
                  pl.BlockSpec((tl // CHUNK, None, CHUNK, SSM_WIDTH), lambda b, i: (i, b, 0, 0)),
                  full(g_attn), full(wo_a), full(wo_s)],
        out_specs=tok(D_MODEL),
        out_shape=jax.ShapeDtypeStruct((batch * seq, D_MODEL), F32),
        compiler_params=_params("arbitrary", "arbitrary"),
    )(x2d, o_attn, o_ssm, g_attn, wo_a, wo_s)


def _ffn_kernel(x_ref, g_ref, wg_ref, wu_ref, wd_ref, out_ref, h_ref):
    j = pl.program_id(1)

    def ff_block(h):
        gate = _dot(h, wg_ref[...])
        up = _dot(h, wu_ref[...])
        act = (gate * _sigmoid(gate) * up).astype(BF16)
        return _dot(act, wd_ref[...])

    @pl.when(j == 0)
    def _():
        sub = x_ref.shape[0] // FFN_SUBTILES
        for s in range(FFN_SUBTILES):
            rs = slice(s * sub, (s + 1) * sub)
            x = x_ref[rs, :]
            h = _rms(x, g_ref[...]).astype(BF16)
            h_ref[rs, :] = h
            out_ref[rs, :] = x + ff_block(h)

    @pl.when(j > 0)
    def _():
        out_ref[...] += ff_block(h_ref[...])


def _ffn(x2d, g, wg, wu, wd, *, tm, tf):
    n_tok = x2d.shape[0]
    return pl.pallas_call(
        _ffn_kernel,
        grid=(n_tok // tm, D_FF // tf),
        in_specs=[pl.BlockSpec((tm, D_MODEL), lambda i, j: (i, 0)),
                  pl.BlockSpec((1, D_MODEL), lambda i, j: (0, 0)),
                  pl.BlockSpec((D_MODEL, tf), lambda i, j: (0, j)),
                  pl.BlockSpec((D_MODEL, tf), lambda i, j: (0, j)),
                  pl.BlockSpec((tf, D_MODEL), lambda i, j: (j, 0))],
        out_specs=pl.BlockSpec((tm, D_MODEL), lambda i, j: (i, 0)),
        out_shape=jax.ShapeDtypeStruct((n_tok, D_MODEL), F32),
        scratch_shapes=[pltpu.VMEM((tm, D_MODEL), BF16)],
        compiler_params=_params("arbitrary", "arbitrary"),
    )(x2d, g, wg, wu, wd)


def _ple_kernel(x_ref, p_ref, g_ref, wpg_ref, wpp_ref, out_ref):
    sub = x_ref.shape[0] // PLE_SUBTILES
    for s in range(PLE_SUBTILES):
        rs = slice(s * sub, (s + 1) * sub)
        x = x_ref[rs, :]
        h = _rms(x, g_ref[...]).astype(BF16)
        gate = _sigmoid(_dot(h, wpg_ref[...]))
        out_ref[rs, :] = x + gate * _dot(p_ref[rs, :].astype(BF16), wpp_ref[...])


def _ple(x2d, p2d, g, wpg, wpp, *, tm):
    n_tok = x2d.shape[0]
    full = lambda a: pl.BlockSpec(a.shape, lambda i: (0,) * a.ndim)
    return pl.pallas_call(
        _ple_kernel,
        grid=(n_tok // tm,),
        in_specs=[pl.BlockSpec((tm, D_MODEL), lambda i: (i, 0)),
                  pl.BlockSpec((tm, PLE_DIM), lambda i: (i, 0)),
                  full(g), full(wpg), full(wpp)],
        out_specs=pl.BlockSpec((tm, D_MODEL), lambda i: (i, 0)),
        out_shape=jax.ShapeDtypeStruct((n_tok, D_MODEL), F32),
        compiler_params=_params("arbitrary"),
    )(x2d, p2d, g, wpg, wpp)


def _rope_tile(t):
    z = jnp.zeros(t.shape[:-1] + (ROPE_HALF,), t.dtype)
    return jnp.concatenate([t[..., :ROPE_HALF], z, t[..., ROPE_HALF:], z], axis=-1)


def _head_gain(g):
    return jnp.concatenate([g[:QK_NOPE_DIM], _rope_tile(g[QK_NOPE_DIM:])])[None, :].astype(F32)


def _s5_params(lam_re, lam_im, log_dt, b_re, b_im, c_re, c_im, d_skip):
    G, P, H, T, NQ = SSM_GROUPS, SSM_STATE, SSM_GROUP, CHUNK, N_PAIRS
    TH = T * H
    lr = jnp.minimum(lam_re.astype(F32), -1e-4)
    li = lam_im.astype(F32)
    dt = jnp.exp(log_dt.astype(F32))[:, None]
    mag = jnp.exp(lr * dt)
    abar_re = mag * jnp.cos(li * dt)
    abar_im = mag * jnp.sin(li * dt)
    den = lr * lr + li * li
    num_re = abar_re - 1.0
    num_im = abar_im
    coef_re = ((num_re * lr + num_im * li) / den)[:, None, :]
    coef_im = ((num_im * lr - num_re * li) / den)[:, None, :]
    br = b_re.astype(F32).transpose(0, 2, 1)
    bim = b_im.astype(F32).transpose(0, 2, 1)
    bb_re = coef_re * br - coef_im * bim
    bb_im = coef_re * bim + coef_im * br
    pw_re, pw_im = [jnp.ones_like(abar_re)], [jnp.zeros_like(abar_im)]
    for _ in range(T):
        r, i = pw_re[-1], pw_im[-1]
        pw_re.append(r * abar_re - i * abar_im)
        pw_im.append(r * abar_im + i * abar_re)
    pw_re = jnp.stack(pw_re, axis=1)[:, :, None, :]
    pw_im = jnp.stack(pw_im, axis=1)[:, :, None, :]
    cr = c_re.astype(F32)[:, None]
    ci = c_im.astype(F32)[:, None]
    ca_re = cr * pw_re - ci * pw_im
    ca_im = cr * pw_im + ci * pw_re
    cat = lambda re, im: jnp.concatenate([re.reshape(G, -1, P), im.reshape(G, -1, P)], axis=-1)
    m1 = jnp.einsum('gik,gjk->gij', cat(bb_re, bb_im), cat(ca_re[:, :T], -ca_im[:, :T]),
                    precision=lax.Precision.HIGHEST)
    m1 = m1.at[:, :, :H].add(jnp.eye(H, dtype=F32)[None] * d_skip.astype(F32)[:, None, :])
    toep = jnp.stack([jnp.pad(m1[:, :, :TH - H * t], ((0, 0), (0, 0), (H * t, 0))) for t in range(T)],
                     axis=1).reshape(G, TH, TH)
    rev_re = pw_re[:, :T][:, ::-1]
    rev_im = pw_im[:, :T][:, ::-1]
    w_re = (rev_re * bb_re[:, None] - rev_im * bb_im[:, None]).reshape(NQ, 2, TH, P)
    w_im = (rev_re * bb_im[:, None] + rev_im * bb_re[:, None]).reshape(NQ, 2, TH, P)
    zw = jnp.zeros((NQ, TH, P), F32)
    wst_p = jnp.concatenate([
        jnp.concatenate([w_re[:, 0], zw, w_im[:, 0], zw], axis=-1),
        jnp.concatenate([zw, w_re[:, 1], zw, w_im[:, 1]], axis=-1)], axis=1)
    tp = toep.reshape(NQ, 2, TH, TH)
    zt = jnp.zeros((NQ, TH, TH), F32)
    toep_p = jnp.concatenate([jnp.concatenate([tp[:, 0], zt], axis=-1),
                              jnp.concatenate([zt, tp[:, 1]], axis=-1)], axis=1)
    v_t = cat(ca_re[:, 1:], -ca_im[:, 1:]).transpose(0, 2, 1).reshape(NQ, 2, 2, P, TH)
    zv = jnp.zeros((NQ, P, TH), F32)
    vmat_p = jnp.concatenate([
        jnp.concatenate([v_t[:, 0, 0], zv], axis=-1), jnp.concatenate([zv, v_t[:, 1, 0]], axis=-1),
        jnp.concatenate([v_t[:, 0, 1], zv], axis=-1), jnp.concatenate([zv, v_t[:, 1, 1]], axis=-1)], axis=1)
    a_re = jnp.broadcast_to(pw_re[:, T].reshape(1, G * P), (SUBLANE, G * P))
    a_im = jnp.broadcast_to(pw_im[:, T].reshape(1, G * P), (SUBLANE, G * P))
    return wst_p.astype(BF16), a_re, a_im, toep_p.astype(BF16), vmat_p.astype(BF16)


def kernel(x, p, positions, g_mix_norm, w_in, g_q_lora, w_uq, g_kv_lora, w_ukv, g_q_head, g_k_head,
           lam_re, lam_im, log_dt, b_re, b_im, c_re, c_im, d_skip, w_glu, b_glu, g_out_attn,
           g_out_ssm, w_o, g_ffn_norm, w_gate, w_up, w_down, g_ple_norm, w_ple_gate, w_ple_proj):
    batch, seq, _ = x.shape
    assert batch == SUBLANE, "the S5 chunk recurrence keeps one batch row per sublane"
    depth = w_in.shape[0]
    n_tok = batch * seq
    row = lambda g: g[None, :].astype(F32)

    inv_freq = 1.0 / (ROPE_THETA ** (jnp.arange(0, QK_ROPE_DIM, 2, dtype=F32) / QK_ROPE_DIM))
    freq = _rope_tile(jnp.concatenate([inv_freq, inv_freq]))[None, :]
    ones = jnp.ones((ROPE_HALF,), F32)
    sgn = _rope_tile(jnp.concatenate([-ones, ones]))[None, :]
    pos2d = positions.reshape(n_tok, 1)

    x2d = x.reshape(n_tok, D_MODEL)
    for i in range(depth):
        win = w_in[i].astype(BF16)
        wq = w_uq[i].reshape(Q_LORA, N_HEADS, QK_HEAD_DIM)
        wuq = jnp.concatenate([wq[..., :QK_NOPE_DIM], _rope_tile(wq[..., QK_NOPE_DIM:])], axis=-1)
        wuq = wuq.reshape(Q_LORA, N_HEADS * HEAD_PAD).astype(BF16)
        wkv = w_ukv[i].reshape(KV_LORA, N_HEADS, QK_NOPE_DIM + V_HEAD_DIM)
        wukv = jnp.concatenate([wkv[..., :QK_NOPE_DIM].reshape(KV_LORA, -1),
                                wkv[..., QK_NOPE_DIM:].reshape(KV_LORA, -1)], axis=1).astype(BF16)

        q, k, v, u_slab = _inproj(x2d, pos2d, row(g_mix_norm[i]), win, row(g_q_lora[i]), wuq,
                                  row(g_kv_lora[i]), wukv, _head_gain(g_q_head[i]),
                                  _head_gain(g_k_head[i]), freq, sgn, batch=batch, seq=seq, tl=512)
        o_attn = _attention(q, k, v, batch=batch, seq=seq, tq=256)

        wst, a_re, a_im, toep, vmat = _s5_params(lam_re[i], lam_im[i], log_dt[i], b_re[i], b_im[i],
                                                 c_re[i], c_im[i], d_skip[i])
        o_ssm = _s5(u_slab, wst, a_re, a_im, toep, vmat,
                    w_glu[i].astype(BF16), row(b_glu[i]), row(g_out_ssm[i]),
                    batch=batch, seq=seq, chunks=16)

        wo = w_o[i].astype(BF16)
        x2d = _oproj(x2d, o_attn, o_ssm, row(g_out_attn[i]), wo[:ATTN_WIDTH], wo[ATTN_WIDTH:],
                     batch=batch, seq=seq, tl=1024)
        x2d = _ffn(x2d, row(g_ffn_norm[i]), w_gate[i].astype(BF16), w_up[i].astype(BF16),
                   w_down[i].astype(BF16), tm=1024, tf=512)
        x2d = _ple(x2d, p[i].reshape(n_tok, PLE_DIM), row(g_ple_norm[i]),
                   w_ple_gate[i].astype(BF16), w_ple_proj[i].astype(BF16), tm=1024)
    return x2d.reshape(batch, seq, D_MODEL)
```

```python
import functools

import jax
import jax.numpy as jnp
from jax import lax
from jax.experimental import pallas as pl
from jax.experimental.pallas import tpu as pltpu

D_MODEL = 2048
PLE_DIM = 256
N_HEADS = 8
QK_NOPE_DIM = 128
QK_ROPE_DIM = 64
V_HEAD_DIM = 128
QK_HEAD_DIM = QK_NOPE_DIM + QK_ROPE_DIM
Q_LORA = 512
KV_LORA = 256
ATTN_WIDTH = N_HEADS * V_HEAD_DIM
ROPE_THETA = 10000.0
SSM_WIDTH = 1024
SSM_GROUP = 16
SSM_GROUPS = SSM_WIDTH // SSM_GROUP
SSM_STATE = 64
D_FF = 5632
EPS = 1e-6

LANE = 128
SUBLANE = 8
HEAD_PAD = 2 * LANE
ROPE_HALF = QK_ROPE_DIM // 2
CHUNK = SUBLANE
N_LANE_TILES = SSM_WIDTH // LANE
GROUPS_PER_TILE = LANE // SSM_GROUP
PAIR_W = 2 * LANE
N_PAIRS = SSM_GROUPS // 2
VMEM_LIMIT = 56 * 1024 * 1024
LOG2_E = 1.4426950408889634
INPROJ_TILE = 512
INPROJ_SUBTILES = 2
ATTN_Q_TILE = 256
ATTN_HEADS_PER_STEP = 4
QK_AHEAD = 8
S5_CHUNKS_PER_STEP = 16
OPROJ_TILE = 1024
OPROJ_SUBTILES = 4
FFN_TOKEN_TILE = 1024
FFN_FF_TILE = 512
FFN_SUBTILES = 2
PLE_TILE = 1024
PLE_SUBTILES = 2

BF16 = jnp.bfloat16
F32 = jnp.float32


def _rms(t, g, width=None):
    n = t.shape[-1] if width is None else width
    ss = jnp.sum(t * t, axis=-1, keepdims=True) * (1.0 / n)
    return t * lax.rsqrt(ss + EPS) * g


def _dot(a, b):
    return jnp.dot(a, b, preferred_element_type=F32)


def _sigmoid(t):
    return 1.0 / (1.0 + jnp.exp(-t))


def _params(*sem):
    return pltpu.CompilerParams(dimension_semantics=sem, vmem_limit_bytes=VMEM_LIMIT)


def _inproj_kernel(x_ref, pos_ref, gmix_ref, win_ref, gql_ref, wuq_ref, gkvl_ref, wukv_ref,
                   gq_ref, gk_ref, freq_ref, sgn_ref,
                   q_ref, k_ref, v_ref, u_ref):
    sub = x_ref.shape[0] // INPROJ_SUBTILES
    o1, o2 = Q_LORA, Q_LORA + KV_LORA
    kr_tile = o2 // LANE
    gq = gq_ref[...]
    gk = gk_ref[...]
    scale = QK_HEAD_DIM ** -0.5 * LOG2_E
    lane = lax.broadcasted_iota(jnp.int32, (sub, LANE), 1)
    lo_half = lane < LANE // 2

    def project(s):
        rs = slice(s * sub, (s + 1) * sub)
        h = _rms(x_ref[rs, :], gmix_ref[...]).astype(BF16)
        z = _dot(h, win_ref[...])
        tiles = [z[:, (kr_tile + m) * LANE:(kr_tile + m + 1) * LANE] for m in range(N_LANE_TILES)]
        last = z[:, (kr_tile + N_LANE_TILES) * LANE:]
        tiles.append(jnp.concatenate([last, jnp.zeros_like(last)], axis=1))
        sw = [pltpu.roll(t, LANE // 2, 1) for t in tiles]
        cs = slice(s * sub // CHUNK, (s + 1) * sub // CHUNK)
        for kt in range(N_LANE_TILES):
            u_ref[cs, kt, :, :] = jnp.where(lo_half, sw[kt], sw[kt + 1]).reshape(sub // CHUNK, CHUNK, LANE)
        zk = z[:, kr_tile * LANE:(kr_tile + 1) * LANE]
        kr = (jnp.where(lane < ROPE_HALF, zk, 0.0)
              + jnp.where((lane >= LANE // 2) & (lane < LANE // 2 + ROPE_HALF),
                          pltpu.roll(zk, ROPE_HALF, 1), 0.0))
        c_q = _rms(z[:, :o1], gql_ref[...]).astype(BF16)
        q = _dot(c_q, wuq_ref[...])
        c_kv = _rms(z[:, o1:o2], gkvl_ref[...]).astype(BF16)
        kv = _dot(c_kv, wukv_ref[...])
        return q, kv, kr

    def finish_heads(s, q, kv, kr):
        rs = slice(s * sub, (s + 1) * sub)
        v_ref[rs, :] = kv[:, ATTN_WIDTH:].astype(BF16)
        ang = pos_ref[rs, :].astype(F32) * freq_ref[...]
        cos_t = jnp.cos(ang)
        sin_t = jnp.sin(ang) * sgn_ref[...]

        def rope(t):
            return t * cos_t + pltpu.roll(t, LANE // 2, 1) * sin_t

        kr_ss = jnp.sum(kr * kr, axis=-1, keepdims=True)
        kr_rope = rope(kr * gk[:, LANE:])
        for hd in range(N_HEADS):
            qh = q[:, hd * HEAD_PAD:(hd + 1) * HEAD_PAD]
            qn = _rms(qh, gq, width=QK_HEAD_DIM) * scale
            q_ref[rs, hd * HEAD_PAD:hd * HEAD_PAD + LANE] = qn[:, :LANE].astype(BF16)
            q_ref[rs, hd * HEAD_PAD + LANE:(hd + 1) * HEAD_PAD] = rope(qn[:, LANE:]).astype(BF16)
            kn = kv[:, hd * QK_NOPE_DIM:(hd + 1) * QK_NOPE_DIM]
            ss = (jnp.sum(kn * kn, axis=-1, keepdims=True) + kr_ss) * (1.0 / QK_HEAD_DIM)
            rinv = lax.rsqrt(ss + EPS)
            k_ref[rs, hd * HEAD_PAD:hd * HEAD_PAD + LANE] = (kn * rinv * gk[:, :LANE]).astype(BF16)
            k_ref[rs, hd * HEAD_PAD + LANE:(hd + 1) * HEAD_PAD] = (kr_rope * rinv).astype(BF16)

    for s in range(INPROJ_SUBTILES):
        finish_heads(s, *project(s))


def _inproj(x2d, pos2d, gmix, win, gql, wuq, gkvl, wukv, gq, gk, freq, sgn, *, batch, seq, tl):
    nl = seq // tl
    tok = lambda w: pl.BlockSpec((tl, w), lambda b, i: (b * nl + i, 0))
    full = lambda a: pl.BlockSpec(a.shape, lambda b, i: (0,) * a.ndim)
    n_tok = batch * seq
    return pl.pallas_call(
        _inproj_kernel,
        grid=(batch, nl),
        in_specs=[tok(D_MODEL), tok(1), full(gmix), full(win), full(gql), full(wuq), full(gkvl),
                  full(wukv), full(gq), full(gk), full(freq), full(sgn)],
        out_specs=[tok(N_HEADS * HEAD_PAD), tok(N_HEADS * HEAD_PAD), tok(ATTN_WIDTH),
                   pl.BlockSpec((tl // CHUNK, N_LANE_TILES, CHUNK, LANE), lambda b, i: (i, 0, b, 0))],
        out_shape=[jax.ShapeDtypeStruct((n_tok, N_HEADS * HEAD_PAD), BF16),
                   jax.ShapeDtypeStruct((n_tok, N_HEADS * HEAD_PAD), BF16),
                   jax.ShapeDtypeStruct((n_tok, ATTN_WIDTH), BF16),
                   jax.ShapeDtypeStruct((seq // CHUNK, N_LANE_TILES, batch * CHUNK, LANE), F32)],
        compiler_params=_params("arbitrary", "arbitrary"),
    )(x2d, pos2d, gmix, win, gql, wuq, gkvl, wukv, gq, gk, freq, sgn)


def _attn_kernel(q_ref, k_ref, v_ref, o_ref, *, tq):
    seq = q_ref.shape[0]
    n = seq // tq
    vts = [v_ref[:, hd * V_HEAD_DIM:(hd + 1) * V_HEAD_DIM].T for hd in range(ATTN_HEADS_PER_STEP)]
    diag_mask = (lax.broadcasted_iota(jnp.int32, (tq, tq), 0)
                 <= lax.broadcasted_iota(jnp.int32, (tq, tq), 1))

    def scores_t(item):
        hd, i = item
        hs = slice(hd * HEAD_PAD, (hd + 1) * HEAD_PAD)
        return lax.dot_general(k_ref[:(i + 1) * tq, hs], q_ref[i * tq:(i + 1) * tq, hs],
                               (((1,), (1,)), ((), ())), preferred_element_type=F32)

    order = [(hd, i) for i in range(n - 1, -1, -1) for hd in range(ATTN_HEADS_PER_STEP)]
    sts = {item: scores_t(item) for item in order[:QK_AHEAD]}
    for pos, item in enumerate(order):
        hd, i = item
        kv_len = (i + 1) * tq
        st = sts.pop(item)
        if pos + QK_AHEAD < len(order):
            ahead = order[pos + QK_AHEAD]
            sts[ahead] = scores_t(ahead)
        sd = jnp.where(diag_mask, st[kv_len - tq:], -jnp.inf)
        m = jnp.max(sd, axis=0, keepdims=True)
        if i:
            m = jnp.maximum(m, jnp.max(st[:kv_len - tq], axis=0, keepdims=True))
        pd = jnp.exp2(sd - m)
        l = jnp.sum(pd, axis=0, keepdims=True)
        if i:
            pt = jnp.exp2(st[:kv_len - tq] - m)
            l = l + jnp.sum(pt, axis=0, keepdims=True)
            p = jnp.concatenate([pt.astype(BF16), pd.astype(BF16)], axis=0)
        else:
            p = pd.astype(BF16)
        ot = _dot(vts[hd][:, :kv_len], p)
        o_ref[i * tq:(i + 1) * tq, hd * V_HEAD_DIM:(hd + 1) * V_HEAD_DIM] = (ot / l).T.astype(o_ref.dtype)


def _attention(q, k, v, *, batch, seq, tq):
    hps = ATTN_HEADS_PER_STEP
    return pl.pallas_call(
        functools.partial(_attn_kernel, tq=tq),
        grid=(batch, N_HEADS // hps),
        in_specs=[pl.BlockSpec((seq, hps * HEAD_PAD), lambda b, h: (b, h)),
                  pl.BlockSpec((seq, hps * HEAD_PAD), lambda b, h: (b, h)),
                  pl.BlockSpec((seq, hps * V_HEAD_DIM), lambda b, h: (b, h))],
        out_specs=pl.BlockSpec((seq, hps * V_HEAD_DIM), lambda b, h: (b, h)),
        out_shape=jax.ShapeDtypeStruct((batch * seq, ATTN_WIDTH), BF16),
        compiler_params=_params("arbitrary", "arbitrary"),
    )(q, k, v)


def _block_transpose(tiles):
    lane = lax.broadcasted_iota(jnp.int32, tiles[0].shape, 1)
    for d in (4, 2, 1):
        hi = (lane & (SSM_GROUP * d)) != 0
        new = list(tiles)
        for i in range(GROUPS_PER_TILE):
            if i & d == 0:
                a, b = tiles[i], tiles[i + d]
                new[i] = jnp.where(hi, pltpu.roll(b, SSM_GROUP * d, 1), a)
                new[i + d] = jnp.where(hi, b, pltpu.roll(a, LANE - SSM_GROUP * d, 1))
        tiles = new
    return tiles


def _s5_kernel(u_ref, wst_ref, are_ref, aim_ref, toep_ref, vmat_ref, wglu_ref, bglu_ref,
               gout_ref, o_ref, state_ref, x_ref, xs_ref, ys_ref, yg_ref, *, chunks, batch):
    rows_x = chunks * batch
    rows = rows_x * CHUNK

    @pl.when(pl.program_id(0) == 0)
    def _():
        state_ref[...] = jnp.zeros_like(state_ref)

    for kt in range(N_LANE_TILES):
        tiles = [u_ref[:, pl.ds(kt, 1), pl.ds(t, batch, stride=CHUNK), :].reshape(rows_x, LANE)
                 for t in range(CHUNK)]
        outs = _block_transpose(tiles)
        for g in range(GROUPS_PER_TILE):
            c0 = (kt * GROUPS_PER_TILE + g) * LANE
            x_ref[:, c0:c0 + LANE] = outs[g].astype(BF16)

    for q in range(N_PAIRS):
        cs = slice(q * PAIR_W, (q + 1) * PAIR_W)
        xs_ref[:, cs] = _dot(x_ref[:, cs], wst_ref[q])

    def step(c, carry):
        r0 = pl.multiple_of(c * batch, batch)
        for q in range(N_PAIRS):
            re = slice(q * PAIR_W, q * PAIR_W + LANE)
            im = slice(q * PAIR_W + LANE, (q + 1) * PAIR_W)
            al = slice(q * LANE, (q + 1) * LANE)
            s_re = state_ref[:, re]
            s_im = state_ref[:, im]
            a_re = are_ref[:, al]
            a_im = aim_ref[:, al]
            x_re = xs_ref[pl.ds(r0, batch), re]
            x_im = xs_ref[pl.ds(r0, batch), im]
            xs_ref[pl.ds(r0, batch), re] = s_re
            xs_ref[pl.ds(r0, batch), im] = s_im
            state_ref[:, re] = a_re * s_re - a_im * s_im + x_re
            state_ref[:, im] = a_re * s_im + a_im * s_re + x_im
        return carry

    lax.fori_loop(0, chunks, step, 0)

    for q in range(N_PAIRS):
        cs = slice(q * PAIR_W, (q + 1) * PAIR_W)
        xs_ref[:, cs] = (_dot(x_ref[:, cs], toep_ref[q])
                         + _dot(xs_ref[:, cs].astype(BF16), vmat_ref[q]))

    for kt in range(N_LANE_TILES):
        tiles = [xs_ref[:, (kt * GROUPS_PER_TILE + g) * LANE:(kt * GROUPS_PER_TILE + g + 1) * LANE]
                 for g in range(GROUPS_PER_TILE)]
        outs = _block_transpose(tiles)
        for t in range(CHUNK):
            ys_ref[:, pl.ds(kt, 1), pl.ds(t, batch, stride=CHUNK), :] = outs[t].reshape(chunks, 1, batch, LANE)

    for kt in range(N_LANE_TILES):
        yg_ref[:, kt * LANE:(kt + 1) * LANE] = jax.nn.gelu(ys_ref[:, kt].reshape(rows, LANE), approximate=True)
    y = yg_ref[...]
    gate = _sigmoid(_dot(y.astype(BF16), wglu_ref[...]) + bglu_ref[...])
    o_ref[...] = _rms(y * gate, gout_ref[...]).reshape(chunks, batch, CHUNK, SSM_WIDTH)


def _s5(u_slab, wst, a_re, a_im, toep, vmat, wglu, bglu, gout, *, batch, seq, chunks):
    n_chunks = seq // CHUNK
    rows_x = chunks * batch
    full = lambda a: pl.BlockSpec(a.shape, lambda i: (0,) * a.ndim, pipeline_mode=pl.Buffered(1))
    return pl.pallas_call(
        functools.partial(_s5_kernel, chunks=chunks, batch=batch),
        grid=(n_chunks // chunks,),
        in_specs=[pl.BlockSpec((chunks, N_LANE_TILES, batch * CHUNK, LANE), lambda i: (i, 0, 0, 0)),
                  full(wst), full(a_re), full(a_im), full(toep), full(vmat), full(wglu),
                  full(bglu), full(gout)],
        out_specs=pl.BlockSpec((chunks, batch, CHUNK, SSM_WIDTH), lambda i: (i, 0, 0, 0)),
        out_shape=jax.ShapeDtypeStruct((n_chunks, batch, CHUNK, SSM_WIDTH), F32),
        scratch_shapes=[pltpu.VMEM((batch, N_PAIRS * PAIR_W), F32),
                        pltpu.VMEM((rows_x, N_PAIRS * PAIR_W), BF16),
                        pltpu.VMEM((rows_x, N_PAIRS * PAIR_W), F32),
                        pltpu.VMEM((chunks, N_LANE_TILES, batch * CHUNK, LANE), F32),
                        pltpu.VMEM((rows_x * CHUNK, SSM_WIDTH), F32)],
        compiler_params=_params("arbitrary"),
    )(u_slab, wst, a_re, a_im, toep, vmat, wglu, bglu, gout)


def _oproj_kernel(x_ref, oa_ref, os_ref, ga_ref, woa_ref, wos_ref, out_ref):
    sub = x_ref.shape[0] // OPROJ_SUBTILES
    for s in range(OPROJ_SUBTILES):
        rs = slice(s * sub, (s + 1) * sub)
        oa = _rms(oa_ref[rs, :].astype(F32), ga_ref[...]).astype(BF16)
        os_ = os_ref[s * sub // CHUNK:(s + 1) * sub // CHUNK].reshape(sub, SSM_WIDTH).astype(BF16)
        out_ref[rs, :] = x_ref[rs, :] + _dot(oa, woa_ref[...]) + _dot(os_, wos_ref[...])


def _oproj(x2d, o_attn, o_ssm, g_attn, wo_a, wo_s, *, batch, seq, tl):
    nl = seq // tl
    tok = lambda w: pl.BlockSpec((tl, w), lambda b, i: (b * nl + i, 0))
    full = lambda a: pl.BlockSpec(a.shape, lambda b, i: (0,) * a.ndim)
    return pl.pallas_call(
        _oproj_kernel,
        grid=(batch, nl),
        in_specs=[tok(D_MODEL), tok(ATTN_WIDTH),
                  pl.BlockSpec((tl // CHUNK, None, CHUNK, SSM_WIDTH), lambda b, i: (i, b, 0, 0)),
                  full(g_attn), full(wo_a), full(wo_s)],
        out_specs=tok(D_MODEL),
        out_shape=jax.ShapeDtypeStruct((batch * seq, D_MODEL), F32),
        compiler_params=_params("arbitrary", "arbitrary"),
    )(x2d, o_attn, o_ssm, g_attn, wo_a, wo_s)


def _ffn_kernel(x_ref, g_ref, wg_ref, wu_ref, wd_ref, out_ref, h_ref):
    j = pl.program_id(1)

    def ff_block(h):
        gate = _dot(h, wg_ref[...])
        up = _dot(h, wu_ref[...])
        act = (gate * _sigmoid(gate) * up).astype(BF16)
        return _dot(act, wd_ref[...])

    @pl.when(j == 0)
    def _():
        sub = x_ref.shape[0] // FFN_SUBTILES
        for s in range(FFN_SUBTILES):
            rs = slice(s * sub, (s + 1) * sub)
            x = x_ref[rs, :]
            h = _rms(x, g_ref[...]).astype(BF16)
            h_ref[rs, :] = h
            out_ref[rs, :] = x + ff_block(h)

    @pl.when(j > 0)
    def _():
        out_ref[...] += ff_block(h_ref[...])


def _ffn(x2d, g, wg, wu, wd, *, tm, tf):
    n_tok = x2d.shape[0]
    return pl.pallas_call(
        _ffn_kernel,
        grid=(n_tok // tm, D_FF // tf),
        in_specs=[pl.BlockSpec((tm, D_MODEL), lambda i, j: (i, 0)),
                  pl.BlockSpec((1, D_MODEL), lambda i, j: (0, 0)),
                  pl.BlockSpec((D_MODEL, tf), lambda i, j: (0, j)),
                  pl.BlockSpec((D_MODEL, tf), lambda i, j: (0, j)),
                  pl.BlockSpec((tf, D_MODEL), lambda i, j: (j, 0))],
        out_specs=pl.BlockSpec((tm, D_MODEL), lambda i, j: (i, 0)),
        out_shape=jax.ShapeDtypeStruct((n_tok, D_MODEL), F32),
        scratch_shapes=[pltpu.VMEM((tm, D_MODEL), BF16)],
        compiler_params=_params("arbitrary", "arbitrary"),
    )(x2d, g, wg, wu, wd)


def _ple_kernel(x_ref, p_ref, g_ref, wpg_ref, wpp_ref, out_ref):
    sub = x_ref.shape[0] // PLE_SUBTILES
    for s in range(PLE_SUBTILES):
        rs = slice(s * sub, (s + 1) * sub)
        x = x_ref[rs, :]
        h = _rms(x, g_ref[...]).astype(BF16)
        gate = _sigmoid(_dot(h, wpg_ref[...]))
        out_ref[rs, :] = x + gate * _dot(p_ref[rs, :].astype(BF16), wpp_ref[...])


def _ple(x2d, p2d, g, wpg, wpp, *, tm):
    n_tok = x2d.shape[0]
    full = lambda a: pl.BlockSpec(a.shape, lambda i: (0,) * a.ndim)
    return pl.pallas_call(
        _ple_kernel,
        grid=(n_tok // tm,),
        in_specs=[pl.BlockSpec((tm, D_MODEL), lambda i: (i, 0)),
                  pl.BlockSpec((tm, PLE_DIM), lambda i: (i, 0)),
                  full(g), full(wpg), full(wpp)],
        out_specs=pl.BlockSpec((tm, D_MODEL), lambda i: (i, 0)),
        out_shape=jax.ShapeDtypeStruct((n_tok, D_MODEL), F32),
        compiler_params=_params("arbitrary"),
    )(x2d, p2d, g, wpg, wpp)


def _rope_tile(t):
    z = jnp.zeros(t.shape[:-1] + (ROPE_HALF,), t.dtype)
    return jnp.concatenate([t[..., :ROPE_HALF], z, t[..., ROPE_HALF:], z], axis=-1)


def _head_gain(g):
    return jnp.concatenate([g[:QK_NOPE_DIM], _rope_tile(g[QK_NOPE_DIM:])])[None, :].astype(F32)


def _s5_params(lam_re, lam_im, log_dt, b_re, b_im, c_re, c_im, d_skip):
    G, P, H, T, NQ = SSM_GROUPS, SSM_STATE, SSM_GROUP, CHUNK, N_PAIRS
    TH = T * H
    lr = jnp.minimum(lam_re.astype(F32), -1e-4)
    li = lam_im.astype(F32)
    dt = jnp.exp(log_dt.astype(F32))[:, None]
    mag = jnp.exp(lr * dt)
    abar_re = mag * jnp.cos(li * dt)
    abar_im = mag * jnp.sin(li * dt)
    den = lr * lr + li * li
    num_re = abar_re - 1.0
    num_im = abar_im
    coef_re = ((num_re * lr + num_im * li) / den)[:, None, :]
    coef_im = ((num_im * lr - num_re * li) / den)[:, None, :]
    br = b_re.astype(F32).transpose(0, 2, 1)
    bim = b_im.astype(F32).transpose(0, 2, 1)
    bb_re = coef_re * br - coef_im * bim
    bb_im = coef_re * bim + coef_im * br
    pw_re, pw_im = [jnp.ones_like(abar_re)], [jnp.zeros_like(abar_im)]
    for _ in range(T):
        r, i = pw_re[-1], pw_im[-1]
        pw_re.append(r * abar_re - i * abar_im)
        pw_im.append(r * abar_im + i * abar_re)
    pw_re = jnp.stack(pw_re, axis=1)[:, :, None, :]
    pw_im = jnp.stack(pw_im, axis=1)[:, :, None, :]
    cr = c_re.astype(F32)[:, None]
    ci = c_im.astype(F32)[:, None]
    ca_re = cr * pw_re - ci * pw_im
    ca_im = cr * pw_im + ci * pw_re
    cat = lambda re, im: jnp.concatenate([re.reshape(G, -1, P), im.reshape(G, -1, P)], axis=-1)
    m1 = jnp.einsum('gik,gjk->gij', cat(bb_re, bb_im), cat(ca_re[:, :T], -ca_im[:, :T]),
                    precision=lax.Precision.HIGHEST)
    m1 = m1.at[:, :, :H].add(jnp.eye(H, dtype=F32)[None] * d_skip.astype(F32)[:, None, :])
    toep = jnp.stack([jnp.pad(m1[:, :, :TH - H * t], ((0, 0), (0, 0), (H * t, 0))) for t in range(T)],
                     axis=1).reshape(G, TH, TH)
    rev_re = pw_re[:, :T][:, ::-1]
    rev_im = pw_im[:, :T][:, ::-1]
    w_re = (rev_re * bb_re[:, None] - rev_im * bb_im[:, None]).reshape(NQ, 2, TH, P)
    w_im = (rev_re * bb_im[:, None] + rev_im * bb_re[:, None]).reshape(NQ, 2, TH, P)
    zw = jnp.zeros((NQ, TH, P), F32)
    wst_p = jnp.concatenate([
        jnp.concatenate([w_re[:, 0], zw, w_im[:, 0], zw], axis=-1),
        jnp.concatenate([zw, w_re[:, 1], zw, w_im[:, 1]], axis=-1)], axis=1)
    tp = toep.reshape(NQ, 2, TH, TH)
    zt = jnp.zeros((NQ, TH, TH), F32)
    toep_p = jnp.concatenate([jnp.concatenate([tp[:, 0], zt], axis=-1),
                              jnp.concatenate([zt, tp[:, 1]], axis=-1)], axis=1)
    v_t = cat(ca_re[:, 1:], -ca_im[:, 1:]).transpose(0, 2, 1).reshape(NQ, 2, 2, P, TH)
    zv = jnp.zeros((NQ, P, TH), F32)
    vmat_p = jnp.concatenate([
        jnp.concatenate([v_t[:, 0, 0], zv], axis=-1), jnp.concatenate([zv, v_t[:, 1, 0]], axis=-1),
        jnp.concatenate([v_t[:, 0, 1], zv], axis=-1), jnp.concatenate([zv, v_t[:, 1, 1]], axis=-1)], axis=1)
    a_re = jnp.broadcast_to(pw_re[:, T].reshape(1, G * P), (SUBLANE, G * P))
    a_im = jnp.broadcast_to(pw_im[:, T].reshape(1, G * P), (SUBLANE, G * P))
    return wst_p.astype(BF16), a_re, a_im, toep_p.astype(BF16), vmat_p.astype(BF16)


def kernel(x, p, positions, g_mix_norm, w_in, g_q_lora, w_uq, g_kv_lora, w_ukv, g_q_head, g_k_head,
           lam_re, lam_im, log_dt, b_re, b_im, c_re, c_im, d_skip, w_glu, b_glu, g_out_attn,
           g_out_ssm, w_o, g_ffn_norm, w_gate, w_up, w_down, g_ple_norm, w_ple_gate, w_ple_proj):
    batch, seq, _ = x.shape
    assert batch == SUBLANE, "the S5 chunk recurrence keeps one batch row per sublane"
    depth = w_in.shape[0]
    n_tok = batch * seq
    row = lambda g: g[None, :].astype(F32)

    inv_freq = 1.0 / (ROPE_THETA ** (jnp.arange(0, QK_ROPE_DIM, 2, dtype=F32) / QK_ROPE_DIM))
    freq = _rope_tile(jnp.concatenate([inv_freq, inv_freq]))[None, :]
    ones = jnp.ones((ROPE_HALF,), F32)
    sgn = _rope_tile(jnp.concatenate([-ones, ones]))[None, :]
    pos2d = positions.reshape(n_tok, 1)

    x2d = x.reshape(n_tok, D_MODEL)
    for i in range(depth):
        win = w_in[i].astype(BF16)
        wq = w_uq[i].reshape(Q_LORA, N_HEADS, QK_HEAD_DIM)
        wuq = jnp.concatenate([wq[..., :QK_NOPE_DIM], _rope_tile(wq[..., QK_NOPE_DIM:])], axis=-1)
        wuq = wuq.reshape(Q_LORA, N_HEADS * HEAD_PAD).astype(BF16)
        wkv = w_ukv[i].reshape(KV_LORA, N_HEADS, QK_NOPE_DIM + V_HEAD_DIM)
        wukv = jnp.concatenate([wkv[..., :QK_NOPE_DIM].reshape(KV_LORA, -1),
                                wkv[..., QK_NOPE_DIM:].reshape(KV_LORA, -1)], axis=1).astype(BF16)

        q, k, v, u_slab = _inproj(x2d, pos2d, row(g_mix_norm[i]), win, row(g_q_lora[i]), wuq,
                                  row(g_kv_lora[i]), wukv, _head_gain(g_q_head[i]),
                                  _head_gain(g_k_head[i]), freq, sgn, batch=batch, seq=seq, tl=INPROJ_TILE)
        o_attn = _attention(q, k, v, batch=batch, seq=seq, tq=ATTN_Q_TILE)

        wst, a_re, a_im, toep, vmat = _s5_params(lam_re[i], lam_im[i], log_dt[i], b_re[i], b_im[i],
                                                 c_re[i], c_im[i], d_skip[i])
        o_ssm = _s5(u_slab, wst, a_re, a_im, toep, vmat,
                    w_glu[i].astype(BF16), row(b_glu[i]), row(g_out_ssm[i]),
                    batch=batch, seq=seq, chunks=S5_CHUNKS_PER_STEP)

        wo = w_o[i].astype(BF16)
        x2d = _oproj(x2d, o_attn, o_ssm, row(g_out_attn[i]), wo[:ATTN_WIDTH], wo[ATTN_WIDTH:],
                     batch=batch, seq=seq, tl=OPROJ_TILE)
        x2d = _ffn(x2d, row(g_ffn_norm[i]), w_gate[i].astype(BF16), w_up[i].astype(BF16),
                   w_down[i].astype(BF16), tm=FFN_TOKEN_TILE, tf=FFN_FF_TILE)
        x2d = _ple(x2d, p[i].reshape(n_tok, PLE_DIM), row(g_ple_norm[i]),
                   w_ple_gate[i].astype(BF16), w_ple_proj[i].astype(BF16), tm=PLE_TILE)
    return x2d.reshape(batch, seq, D_MODEL)
```

```python
import functools

import jax
import jax.numpy as jnp
from jax import lax
from jax.experimental import pallas as pl
from jax.experimental.pallas import tpu as pltpu

D_MODEL = 2048
PLE_DIM = 256
N_HEADS = 8
QK_NOPE_DIM = 128
QK_ROPE_DIM = 64
V_HEAD_DIM = 128
QK_HEAD_DIM = QK_NOPE_DIM + QK_ROPE_DIM
Q_LORA = 512
KV_LORA = 256
ATTN_WIDTH = N_HEADS * V_HEAD_DIM
ROPE_THETA = 10000.0
SSM_WIDTH = 1024
SSM_GROUP = 16
SSM_GROUPS = SSM_WIDTH // SSM_GROUP
SSM_STATE = 64
D_FF = 5632
EPS = 1e-6

LANE = 128
SUBLANE = 8
HEAD_PAD = 2 * LANE
ROPE_HALF = QK_ROPE_DIM // 2
CHUNK = SUBLANE
N_LANE_TILES = SSM_WIDTH // LANE
PAIR_BLOCK = 2 * SSM_GROUP
PAIRS_PER_TILE = LANE // PAIR_BLOCK
PAIR_W = 2 * LANE
N_PAIRS = SSM_GROUPS // 2
VMEM_LIMIT = 56 * 1024 * 1024
LOG2_E = 1.4426950408889634
INPROJ_TILE = 512
INPROJ_SUBTILES = 2
ATTN_Q_TILE = 256
ATTN_HEADS_PER_STEP = 2
QK_AHEAD = 4
S5_CHUNKS_PER_STEP = 16
OPROJ_TILE = 1024
OPROJ_SUBTILES = 4
FFN_TOKEN_TILE = 1024
FFN_FF_TILE = 512
FFN_SUBTILES = 2
PLE_TILE = 1024
PLE_SUBTILES = 2

BF16 = jnp.bfloat16
F32 = jnp.float32


def _rms(t, g, width=None):
    n = t.shape[-1] if width is None else width
    ss = jnp.sum(t * t, axis=-1, keepdims=True) * (1.0 / n)
    return t * lax.rsqrt(ss + EPS) * g


def _dot(a, b):
    return jnp.dot(a, b, preferred_element_type=F32)


def _sigmoid(t):
    return 1.0 / (1.0 + jnp.exp(-t))


def _params(*sem):
    return pltpu.CompilerParams(dimension_semantics=sem, vmem_limit_bytes=VMEM_LIMIT)


def _inproj_kernel(x_ref, pos_ref, gmix_ref, win_ref, gql_ref, wuq_ref, gkvl_ref, wukv_ref,
                   gq_ref, gk_ref, freq_ref, sgn_ref,
                   q_ref, k_ref, v_ref, u_ref):
    sub = x_ref.shape[0] // INPROJ_SUBTILES
    o1, o2 = Q_LORA, Q_LORA + KV_LORA
    kr_tile = o2 // LANE
    gq = gq_ref[...]
    gk = gk_ref[...]
    scale = QK_HEAD_DIM ** -0.5 * LOG2_E
    lane = lax.broadcasted_iota(jnp.int32, (sub, LANE), 1)
    lo_half = lane < LANE // 2

    def project(s):
        rs = slice(s * sub, (s + 1) * sub)
        h = _rms(x_ref[rs, :], gmix_ref[...]).astype(BF16)
        z = _dot(h, win_ref[...])
        tiles = [z[:, (kr_tile + m) * LANE:(kr_tile + m + 1) * LANE] for m in range(N_LANE_TILES)]
        last = z[:, (kr_tile + N_LANE_TILES) * LANE:]
        tiles.append(jnp.concatenate([last, jnp.zeros_like(last)], axis=1))
        sw = [pltpu.roll(t, LANE // 2, 1) for t in tiles]
        cs = slice(s * sub // CHUNK, (s + 1) * sub // CHUNK)
        for kt in range(N_LANE_TILES):
            u_ref[cs, kt, :, :] = jnp.where(lo_half, sw[kt], sw[kt + 1]).reshape(sub // CHUNK, CHUNK, LANE)
        zk = z[:, kr_tile * LANE:(kr_tile + 1) * LANE]
        kr = (jnp.where(lane < ROPE_HALF, zk, 0.0)
              + jnp.where((lane >= LANE // 2) & (lane < LANE // 2 + ROPE_HALF),
                          pltpu.roll(zk, ROPE_HALF, 1), 0.0))
        c_q = _rms(z[:, :o1], gql_ref[...]).astype(BF16)
        q = _dot(c_q, wuq_ref[...])
        c_kv = _rms(z[:, o1:o2], gkvl_ref[...]).astype(BF16)
        kv = _dot(c_kv, wukv_ref[...])
        return q, kv, kr

    def finish_heads(s, q, kv, kr):
        rs = slice(s * sub, (s + 1) * sub)
        v_ref[rs, :] = kv[:, ATTN_WIDTH:].astype(BF16)
        ang = pos_ref[rs, :].astype(F32) * freq_ref[...]
        cos_t = jnp.cos(ang)
        sin_t = jnp.sin(ang) * sgn_ref[...]

        def rope(t):
            return t * cos_t + pltpu.roll(t, LANE // 2, 1) * sin_t

        kr_ss = jnp.sum(kr * kr, axis=-1, keepdims=True)
        kr_rope = rope(kr * gk[:, LANE:])
        for hd in range(N_HEADS):
            qh = q[:, hd * HEAD_PAD:(hd + 1) * HEAD_PAD]
            qn = _rms(qh, gq, width=QK_HEAD_DIM) * scale
            q_ref[rs, hd * HEAD_PAD:hd * HEAD_PAD + LANE] = qn[:, :LANE].astype(BF16)
            q_ref[rs, hd * HEAD_PAD + LANE:(hd + 1) * HEAD_PAD] = rope(qn[:, LANE:]).astype(BF16)
            kn = kv[:, hd * QK_NOPE_DIM:(hd + 1) * QK_NOPE_DIM]
            ss = (jnp.sum(kn * kn, axis=-1, keepdims=True) + kr_ss) * (1.0 / QK_HEAD_DIM)
            rinv = lax.rsqrt(ss + EPS)
            k_ref[rs, hd * HEAD_PAD:hd * HEAD_PAD + LANE] = (kn * rinv * gk[:, :LANE]).astype(BF16)
            k_ref[rs, hd * HEAD_PAD + LANE:(hd + 1) * HEAD_PAD] = (kr_rope * rinv).astype(BF16)

    for s in range(INPROJ_SUBTILES):
        finish_heads(s, *project(s))


def _inproj(x2d, pos2d, gmix, win, gql, wuq, gkvl, wukv, gq, gk, freq, sgn, *, batch, seq, tl):
    nl = seq // tl
    tok = lambda w: pl.BlockSpec((tl, w), lambda b, i: (b * nl + i, 0))
    full = lambda a: pl.BlockSpec(a.shape, lambda b, i: (0,) * a.ndim)
    n_tok = batch * seq
    return pl.pallas_call(
        _inproj_kernel,
        grid=(batch, nl),
        in_specs=[tok(D_MODEL), tok(1), full(gmix), full(win), full(gql), full(wuq), full(gkvl),
                  full(wukv), full(gq), full(gk), full(freq), full(sgn)],
        out_specs=[tok(N_HEADS * HEAD_PAD), tok(N_HEADS * HEAD_PAD), tok(ATTN_WIDTH),
                   pl.BlockSpec((tl // CHUNK, N_LANE_TILES, CHUNK, LANE), lambda b, i: (i, 0, b, 0))],
        out_shape=[jax.ShapeDtypeStruct((n_tok, N_HEADS * HEAD_PAD), BF16),
                   jax.ShapeDtypeStruct((n_tok, N_HEADS * HEAD_PAD), BF16),
                   jax.ShapeDtypeStruct((n_tok, ATTN_WIDTH), BF16),
                   jax.ShapeDtypeStruct((seq // CHUNK, N_LANE_TILES, batch * CHUNK, LANE), F32)],
        compiler_params=_params("arbitrary", "arbitrary"),
    )(x2d, pos2d, gmix, win, gql, wuq, gkvl, wukv, gq, gk, freq, sgn)


def _attn_kernel(q_ref, k_ref, v_ref, o_ref, *, tq):
    seq = q_ref.shape[0]
    n = seq // tq
    vts = [v_ref[:, hd * V_HEAD_DIM:(hd + 1) * V_HEAD_DIM].T for hd in range(ATTN_HEADS_PER_STEP)]
    diag_mask = (lax.broadcasted_iota(jnp.int32, (tq, tq), 0)
                 <= lax.broadcasted_iota(jnp.int32, (tq, tq), 1))

    def scores_t(item):
        hd, i = item
        hs = slice(hd * HEAD_PAD, (hd + 1) * HEAD_PAD)
        return lax.dot_general(k_ref[:(i + 1) * tq, hs], q_ref[i * tq:(i + 1) * tq, hs],
                               (((1,), (1,)), ((), ())), preferred_element_type=F32)

    order = [(hd, i) for i in range(n - 1, -1, -1) for hd in range(ATTN_HEADS_PER_STEP)]
    sts = {item: scores_t(item) for item in order[:QK_AHEAD]}
    for pos, item in enumerate(order):
        hd, i = item
        kv_len = (i + 1) * tq
        st = sts.pop(item)
        if pos + QK_AHEAD < len(order):
            ahead = order[pos + QK_AHEAD]
            sts[ahead] = scores_t(ahead)
        sd = jnp.where(diag_mask, st[kv_len - tq:], -jnp.inf)
        m = jnp.max(sd, axis=0, keepdims=True)
        if i:
            m = jnp.maximum(m, jnp.max(st[:kv_len - tq], axis=0, keepdims=True))
        pd = jnp.exp2(sd - m)
        l = jnp.sum(pd, axis=0, keepdims=True)
        if i:
            pt = jnp.exp2(st[:kv_len - tq] - m)
            l = l + jnp.sum(pt, axis=0, keepdims=True)
            p = jnp.concatenate([pt.astype(BF16), pd.astype(BF16)], axis=0)
        else:
            p = pd.astype(BF16)
        ot = _dot(vts[hd][:, :kv_len], p)
        o_ref[i * tq:(i + 1) * tq, hd * V_HEAD_DIM:(hd + 1) * V_HEAD_DIM] = (ot / l).T.astype(o_ref.dtype)


def _attention(q, k, v, *, batch, seq, tq):
    hps = ATTN_HEADS_PER_STEP
    return pl.pallas_call(
        functools.partial(_attn_kernel, tq=tq),
        grid=(batch, N_HEADS // hps),
        in_specs=[pl.BlockSpec((seq, hps * HEAD_PAD), lambda b, h: (b, h)),
                  pl.BlockSpec((seq, hps * HEAD_PAD), lambda b, h: (b, h)),
                  pl.BlockSpec((seq, hps * V_HEAD_DIM), lambda b, h: (b, h))],
        out_specs=pl.BlockSpec((seq, hps * V_HEAD_DIM), lambda b, h: (b, h)),
        out_shape=jax.ShapeDtypeStruct((batch * seq, ATTN_WIDTH), BF16),
        compiler_params=_params("arbitrary", "arbitrary"),
    )(q, k, v)


def _block_transpose(tiles):
    lane = lax.broadcasted_iota(jnp.int32, tiles[0].shape, 1)
    for d in (2, 1):
        hi = (lane & (PAIR_BLOCK * d)) != 0
        new = list(tiles)
        for i in range(PAIRS_PER_TILE):
            if i & d == 0:
                a, b = tiles[i], tiles[i + d]
                new[i] = jnp.where(hi, pltpu.roll(b, PAIR_BLOCK * d, 1), a)
                new[i + d] = jnp.where(hi, b, pltpu.roll(a, LANE - PAIR_BLOCK * d, 1))
        tiles = new
    return tiles


def _s5_kernel(u_ref, wst_ref, are_ref, aim_ref, toep_ref, vmat_ref, wglu_ref, bglu_ref,
               gout_ref, o_ref, state_ref, x_ref, xs_ref, ys_ref, yg_ref, *, chunks, batch):
    rows_x = chunks * batch
    rows = rows_x * CHUNK

    @pl.when(pl.program_id(0) == 0)
    def _():
        state_ref[...] = jnp.zeros_like(state_ref)

    for kt in range(N_LANE_TILES):
        tiles = [u_ref[:, pl.ds(kt, 1), pl.ds(t, batch, stride=CHUNK), :].reshape(rows_x, LANE)
                 for t in range(CHUNK)]
        for half in range(CHUNK // PAIRS_PER_TILE):
            outs = _block_transpose(tiles[half * PAIRS_PER_TILE:(half + 1) * PAIRS_PER_TILE])
            for j in range(PAIRS_PER_TILE):
                c0 = (kt * PAIRS_PER_TILE + j) * PAIR_W + half * LANE
                x_ref[:, c0:c0 + LANE] = outs[j].astype(BF16)

    for q in range(N_PAIRS):
        cs = slice(q * PAIR_W, (q + 1) * PAIR_W)
        xs_ref[:, cs] = _dot(x_ref[:, cs], wst_ref[q])

    def step(c, carry):
        r0 = pl.multiple_of(c * batch, batch)
        for q in range(N_PAIRS):
            re = slice(q * PAIR_W, q * PAIR_W + LANE)
            im = slice(q * PAIR_W + LANE, (q + 1) * PAIR_W)
            al = slice(q * LANE, (q + 1) * LANE)
            s_re = state_ref[:, re]
            s_im = state_ref[:, im]
            a_re = are_ref[:, al]
            a_im = aim_ref[:, al]
            x_re = xs_ref[pl.ds(r0, batch), re]
            x_im = xs_ref[pl.ds(r0, batch), im]
            xs_ref[pl.ds(r0, batch), re] = s_re
            xs_ref[pl.ds(r0, batch), im] = s_im
            state_ref[:, re] = a_re * s_re - a_im * s_im + x_re
            state_ref[:, im] = a_re * s_im + a_im * s_re + x_im
        return carry

    lax.fori_loop(0, chunks, step, 0)

    for q in range(N_PAIRS):
        cs = slice(q * PAIR_W, (q + 1) * PAIR_W)
        xs_ref[:, cs] = (_dot(x_ref[:, cs], toep_ref[q])
                         + _dot(xs_ref[:, cs].astype(BF16), vmat_ref[q]))

    for kt in range(N_LANE_TILES):
        for half in range(CHUNK // PAIRS_PER_TILE):
            tiles = [xs_ref[:, (kt * PAIRS_PER_TILE + j) * PAIR_W + half * LANE:
                            (kt * PAIRS_PER_TILE + j) * PAIR_W + (half + 1) * LANE]
                     for j in range(PAIRS_PER_TILE)]
            outs = _block_transpose(tiles)
            for t4 in range(PAIRS_PER_TILE):
                t = half * PAIRS_PER_TILE + t4
                ys_ref[:, pl.ds(kt, 1), pl.ds(t, batch, stride=CHUNK), :] = (
                    outs[t4].reshape(chunks, 1, batch, LANE))

    for kt in range(N_LANE_TILES):
        yg_ref[:, kt * LANE:(kt + 1) * LANE] = jax.nn.gelu(ys_ref[:, kt].reshape(rows, LANE), approximate=True)
    y = yg_ref[...]
    gate = _sigmoid(_dot(y.astype(BF16), wglu_ref[...]) + bglu_ref[...])
    o_ref[...] = _rms(y * gate, gout_ref[...]).reshape(chunks, batch, CHUNK, SSM_WIDTH)


def _s5(u_slab, wst, a_re, a_im, toep, vmat, wglu, bglu, gout, *, batch, seq, chunks):
    n_chunks = seq // CHUNK
    rows_x = chunks * batch
    full = lambda a: pl.BlockSpec(a.shape, lambda i: (0,) * a.ndim, pipeline_mode=pl.Buffered(1))
    return pl.pallas_call(
        functools.partial(_s5_kernel, chunks=chunks, batch=batch),
        grid=(n_chunks // chunks,),
        in_specs=[pl.BlockSpec((chunks, N_LANE_TILES, batch * CHUNK, LANE), lambda i: (i, 0, 0, 0)),
                  full(wst), full(a_re), full(a_im), full(toep), full(vmat), full(wglu),
                  full(bglu), full(gout)],
        out_specs=pl.BlockSpec((chunks, batch, CHUNK, SSM_WIDTH), lambda i: (i, 0, 0, 0)),
        out_shape=jax.ShapeDtypeStruct((n_chunks, batch, CHUNK, SSM_WIDTH), F32),
        scratch_shapes=[pltpu.VMEM((batch, N_PAIRS * PAIR_W), F32),
                        pltpu.VMEM((rows_x, N_PAIRS * PAIR_W), BF16),
                        pltpu.VMEM((rows_x, N_PAIRS * PAIR_W), F32),
                        pltpu.VMEM((chunks, N_LANE_TILES, batch * CHUNK, LANE), F32),
                        pltpu.VMEM((rows_x * CHUNK, SSM_WIDTH), F32)],
        compiler_params=_params("arbitrary"),
    )(u_slab, wst, a_re, a_im, toep, vmat, wglu, bglu, gout)


def _oproj_kernel(x_ref, oa_ref, os_ref, ga_ref, woa_ref, wos_ref, out_ref):
    sub = x_ref.shape[0] // OPROJ_SUBTILES
    for s in range(OPROJ_SUBTILES):
        rs = slice(s * sub, (s + 1) * sub)
        oa = _rms(oa_ref[rs, :].astype(F32), ga_ref[...]).astype(BF16)
        os_ = os_ref[s * sub // CHUNK:(s + 1) * sub // CHUNK].reshape(sub, SSM_WIDTH).astype(BF16)
        out_ref[rs, :] = x_ref[rs, :] + _dot(oa, woa_ref[...]) + _dot(os_, wos_ref[...])


def _oproj(x2d, o_attn, o_ssm, g_attn, wo_a, wo_s, *, batch, seq, tl):
    nl = seq // tl
    tok = lambda w: pl.BlockSpec((tl, w), lambda b, i: (b * nl + i, 0))
    full = lambda a: pl.BlockSpec(a.shape, lambda b, i: (0,) * a.ndim)
    return pl.pallas_call(
        _oproj_kernel,
        grid=(batch, nl),
        in_specs=[tok(D_MODEL), tok(ATTN_WIDTH),
                  pl.BlockSpec((tl // CHUNK, None, CHUNK, SSM_WIDTH), lambda b, i: (i, b, 0, 0)),
                  full(g_attn), full(wo_a), full(wo_s)],
        out_specs=tok(D_MODEL),
        out_shape=jax.ShapeDtypeStruct((batch * seq, D_MODEL), F32),
        compiler_params=_params("arbitrary", "arbitrary"),
    )(x2d, o_attn, o_ssm, g_attn, wo_a, wo_s)


def _ffn_kernel(x_ref, g_ref, wg_ref, wu_ref, wd_ref, out_ref, h_ref):
    j = pl.program_id(1)

    def ff_block(h):
        gate = _dot(h, wg_ref[...])
        up = _dot(h, wu_ref[...])
        act = (gate * _sigmoid(gate) * up).astype(BF16)
        return _dot(act, wd_ref[...])

    @pl.when(j == 0)
    def _():
        sub = x_ref.shape[0] // FFN_SUBTILES
        for s in range(FFN_SUBTILES):
            rs = slice(s * sub, (s + 1) * sub)
            x = x_ref[rs, :]
            h = _rms(x, g_ref[...]).astype(BF16)
            h_ref[rs, :] = h
            out_ref[rs, :] = x + ff_block(h)

    @pl.when(j > 0)
    def _():
        out_ref[...] += ff_block(h_ref[...])


def _ffn(x2d, g, wg, wu, wd, *, tm, tf):
    n_tok = x2d.shape[0]
    return pl.pallas_call(
        _ffn_kernel,
        grid=(n_tok // tm, D_FF // tf),
        in_specs=[pl.BlockSpec((tm, D_MODEL), lambda i, j: (i, 0)),
                  pl.BlockSpec((1, D_MODEL), lambda i, j: (0, 0)),
                  pl.BlockSpec((D_MODEL, tf), lambda i, j: (0, j)),
                  pl.BlockSpec((D_MODEL, tf), lambda i, j: (0, j)),
                  pl.BlockSpec((tf, D_MODEL), lambda i, j: (j, 0))],
        out_specs=pl.BlockSpec((tm, D_MODEL), lambda i, j: (i, 0)),
        out_shape=jax.ShapeDtypeStruct((n_tok, D_MODEL), F32),
        scratch_shapes=[pltpu.VMEM((tm, D_MODEL), BF16)],
        compiler_params=_params("arbitrary", "arbitrary"),
    )(x2d, g, wg, wu, wd)


def _ple_kernel(x_ref, p_ref, g_ref, wpg_ref, wpp_ref, out_ref):
    sub = x_ref.shape[0] // PLE_SUBTILES
    for s in range(PLE_SUBTILES):
        rs = slice(s * sub, (s + 1) * sub)
        x = x_ref[rs, :]
        h = _rms(x, g_ref[...]).astype(BF16)
        gate = _sigmoid(_dot(h, wpg_ref[...]))
        out_ref[rs, :] = x + gate * _dot(p_ref[rs, :].astype(BF16), wpp_ref[...])


def _ple(x2d, p2d, g, wpg, wpp, *, tm):
    n_tok = x2d.shape[0]
    full = lambda a: pl.BlockSpec(a.shape, lambda i: (0,) * a.ndim)
    return pl.pallas_call(
        _ple_kernel,
        grid=(n_tok // tm,),
        in_specs=[pl.BlockSpec((tm, D_MODEL), lambda i: (i, 0)),
                  pl.BlockSpec((tm, PLE_DIM), lambda i: (i, 0)),
                  full(g), full(wpg), full(wpp)],
        out_specs=pl.BlockSpec((tm, D_MODEL), lambda i: (i, 0)),
        out_shape=jax.ShapeDtypeStruct((n_tok, D_MODEL), F32),
        compiler_params=_params("arbitrary"),
    )(x2d, p2d, g, wpg, wpp)


def _rope_tile(t):
    z = jnp.zeros(t.shape[:-1] + (ROPE_HALF,), t.dtype)
    return jnp.concatenate([t[..., :ROPE_HALF], z, t[..., ROPE_HALF:], z], axis=-1)


def _head_gain(g):
    return jnp.concatenate([g[:QK_NOPE_DIM], _rope_tile(g[QK_NOPE_DIM:])])[None, :].astype(F32)


def _s5_params(lam_re, lam_im, log_dt, b_re, b_im, c_re, c_im, d_skip):
    G, P, H, T, NQ = SSM_GROUPS, SSM_STATE, SSM_GROUP, CHUNK, N_PAIRS
    TH = T * H
    lr = jnp.minimum(lam_re.astype(F32), -1e-4)
    li = lam_im.astype(F32)
    dt = jnp.exp(log_dt.astype(F32))[:, None]
    mag = jnp.exp(lr * dt)
    abar_re = mag * jnp.cos(li * dt)
    abar_im = mag * jnp.sin(li * dt)
    den = lr * lr + li * li
    num_re = abar_re - 1.0
    num_im = abar_im
    coef_re = ((num_re * lr + num_im * li) / den)[:, None, :]
    coef_im = ((num_im * lr - num_re * li) / den)[:, None, :]
    br = b_re.astype(F32).transpose(0, 2, 1)
    bim = b_im.astype(F32).transpose(0, 2, 1)
    bb_re = coef_re * br - coef_im * bim
    bb_im = coef_re * bim + coef_im * br
    pw_re, pw_im = [jnp.ones_like(abar_re)], [jnp.zeros_like(abar_im)]
    for _ in range(T):
        r, i = pw_re[-1], pw_im[-1]
        pw_re.append(r * abar_re - i * abar_im)
        pw_im.append(r * abar_im + i * abar_re)
    pw_re = jnp.stack(pw_re, axis=1)[:, :, None, :]
    pw_im = jnp.stack(pw_im, axis=1)[:, :, None, :]
    cr = c_re.astype(F32)[:, None]
    ci = c_im.astype(F32)[:, None]
    ca_re = cr * pw_re - ci * pw_im
    ca_im = cr * pw_im + ci * pw_re
    cat = lambda re, im: jnp.concatenate([re.reshape(G, -1, P), im.reshape(G, -1, P)], axis=-1)
    m1 = jnp.einsum('gik,gjk->gij', cat(bb_re, bb_im), cat(ca_re[:, :T], -ca_im[:, :T]),
                    precision=lax.Precision.HIGHEST)
    m1 = m1.at[:, :, :H].add(jnp.eye(H, dtype=F32)[None] * d_skip.astype(F32)[:, None, :])
    eye2 = jnp.eye(2, dtype=F32)
    m1p = (m1.reshape(NQ, 2, H, T, 1, H) * eye2[None, :, None, None, :, None]).reshape(NQ, 2 * H, 2 * TH)
    toep_p = jnp.stack([jnp.pad(m1p[:, :, :2 * TH - PAIR_BLOCK * t], ((0, 0), (0, 0), (PAIR_BLOCK * t, 0)))
                        for t in range(T)], axis=1).reshape(NQ, 2 * TH, 2 * TH)
    rev_re = pw_re[:, :T][:, ::-1]
    rev_im = pw_im[:, :T][:, ::-1]
    w_re = rev_re * bb_re[:, None] - rev_im * bb_im[:, None]
    w_im = rev_re * bb_im[:, None] + rev_im * bb_re[:, None]

    def pair_rows(re, im):
        ri = jnp.stack([re, im], axis=-2).reshape(NQ, 2, T, H, 2, 1, P)
        spread = ri * eye2[None, :, None, None, None, :, None]
        return spread.transpose(0, 2, 1, 3, 4, 5, 6).reshape(NQ, 2 * TH, 4 * P)

    wst_p = pair_rows(w_re, w_im)
    vmat_p = pair_rows(ca_re[:, 1:], -ca_im[:, 1:]).transpose(0, 2, 1)
    a_re = jnp.broadcast_to(pw_re[:, T].reshape(1, G * P), (SUBLANE, G * P))
    a_im = jnp.broadcast_to(pw_im[:, T].reshape(1, G * P), (SUBLANE, G * P))
    return wst_p.astype(BF16), a_re, a_im, toep_p.astype(BF16), vmat_p.astype(BF16)


def kernel(x, p, positions, g_mix_norm, w_in, g_q_lora, w_uq, g_kv_lora, w_ukv, g_q_head, g_k_head,
           lam_re, lam_im, log_dt, b_re, b_im, c_re, c_im, d_skip, w_glu, b_glu, g_out_attn,
           g_out_ssm, w_o, g_ffn_norm, w_gate, w_up, w_down, g_ple_norm, w_ple_gate, w_ple_proj):
    batch, seq, _ = x.shape
    assert batch == SUBLANE, "the S5 chunk recurrence keeps one batch row per sublane"
    depth = w_in.shape[0]
    n_tok = batch * seq
    row = lambda g: g[None, :].astype(F32)

    inv_freq = 1.0 / (ROPE_THETA ** (jnp.arange(0, QK_ROPE_DIM, 2, dtype=F32) / QK_ROPE_DIM))
    freq = _rope_tile(jnp.concatenate([inv_freq, inv_freq]))[None, :]
    ones = jnp.ones((ROPE_HALF,), F32)
    sgn = _rope_tile(jnp.concatenate([-ones, ones]))[None, :]
    pos2d = positions.reshape(n_tok, 1)

    x2d = x.reshape(n_tok, D_MODEL)
    for i in range(depth):
        win = w_in[i].astype(BF16)
        wq = w_uq[i].reshape(Q_LORA, N_HEADS, QK_HEAD_DIM)
        wuq = jnp.concatenate([wq[..., :QK_NOPE_DIM], _rope_tile(wq[..., QK_NOPE_DIM:])], axis=-1)
        wuq = wuq.reshape(Q_LORA, N_HEADS * HEAD_PAD).astype(BF16)
        wkv = w_ukv[i].reshape(KV_LORA, N_HEADS, QK_NOPE_DIM + V_HEAD_DIM)
        wukv = jnp.concatenate([wkv[..., :QK_NOPE_DIM].reshape(KV_LORA, -1),
                                wkv[..., QK_NOPE_DIM:].reshape(KV_LORA, -1)], axis=1).astype(BF16)

        q, k, v, u_slab = _inproj(x2d, pos2d, row(g_mix_norm[i]), win, row(g_q_lora[i]), wuq,
                                  row(g_kv_lora[i]), wukv, _head_gain(g_q_head[i]),
                                  _head_gain(g_k_head[i]), freq, sgn, batch=batch, seq=seq, tl=INPROJ_TILE)
        o_attn = _attention(q, k, v, batch=batch, seq=seq, tq=ATTN_Q_TILE)

        wst, a_re, a_im, toep, vmat = _s5_params(lam_re[i], lam_im[i], log_dt[i], b_re[i], b_im[i],
                                                 c_re[i], c_im[i], d_skip[i])
        o_ssm = _s5(u_slab, wst, a_re, a_im, toep, vmat,
                    w_glu[i].astype(BF16), row(b_glu[i]), row(g_out_ssm[i]),
                    batch=batch, seq=seq, chunks=S5_CHUNKS_PER_STEP)

        wo = w_o[i].astype(BF16)
        x2d = _oproj(x2d, o_attn, o_ssm, row(g_out_attn[i]), wo[:ATTN_WIDTH], wo[ATTN_WIDTH:],
                     batch=batch, seq=seq, tl=OPROJ_TILE)
        x2d = _ffn(x2d, row(g_ffn_norm[i]), w_gate[i].astype(BF16), w_up[i].astype(BF16),
                   w_down[i].astype(BF16), tm=FFN_TOKEN_TILE, tf=FFN_FF_TILE)
        x2d = _ple(x2d, p[i].reshape(n_tok, PLE_DIM), row(g_ple_norm[i]),
                   w_ple_gate[i].astype(BF16), w_ple_proj[i].astype(BF16), tm=PLE_TILE)
    return x2d.reshape(batch, seq, D_MODEL)
```

```python
import functools

import jax
import jax.numpy as jnp
from jax import lax
from jax.experimental import pallas as pl
from jax.experimental.pallas import tpu as pltpu

D_MODEL = 2048
PLE_DIM = 256
N_HEADS = 8
QK_NOPE_DIM = 128
QK_ROPE_DIM = 64
V_HEAD_DIM = 128
QK_HEAD_DIM = QK_NOPE_DIM + QK_ROPE_DIM
Q_LORA = 512
KV_LORA = 256
ATTN_WIDTH = N_HEADS * V_HEAD_DIM
ROPE_THETA = 10000.0
SSM_WIDTH = 1024
SSM_GROUP = 16
SSM_GROUPS = SSM_WIDTH // SSM_GROUP
SSM_STATE = 64
D_FF = 5632
EPS = 1e-6

LANE = 128
SUBLANE = 8
HEAD_PAD = 2 * LANE
ROPE_HALF = QK_ROPE_DIM // 2
CHUNK = SUBLANE
N_LANE_TILES = SSM_WIDTH // LANE
PAIR_BLOCK = 2 * SSM_GROUP
PAIRS_PER_TILE = LANE // PAIR_BLOCK
PAIR_W = 2 * LANE
N_PAIRS = SSM_GROUPS // 2
VMEM_LIMIT = 56 * 1024 * 1024
LOG2_E = 1.4426950408889634
INPROJ_TILE = 512
INPROJ_SUBTILES = 2
ATTN_Q_TILE = 256
ATTN_HEADS_PER_STEP = 2
QK_AHEAD = 4
S5_CHUNKS_PER_STEP = 16
OPROJ_TILE = 1024
OPROJ_SUBTILES = 4
FFN_TOKEN_TILE = 1024
FFN_FF_TILE = 512
FFN_SUBTILES = 2
PLE_TILE = 1024
PLE_SUBTILES = 2

BF16 = jnp.bfloat16
F32 = jnp.float32


def _rms(t, g, width=None):
    n = t.shape[-1] if width is None else width
    ss = jnp.sum(t * t, axis=-1, keepdims=True) * (1.0 / n)
    return t * lax.rsqrt(ss + EPS) * g


def _dot(a, b):
    return jnp.dot(a, b, preferred_element_type=F32)


def _sigmoid(t):
    return 1.0 / (1.0 + jnp.exp(-t))


def _params(*sem):
    return pltpu.CompilerParams(dimension_semantics=sem, vmem_limit_bytes=VMEM_LIMIT)


def _inproj_kernel(x_ref, pos_ref, gmix_ref, win_ref, gql_ref, wuq_ref, gkvl_ref, wukv_ref,
                   gq_ref, gk_ref, freq_ref, sgn_ref,
                   q_ref, k_ref, v_ref, u_ref):
    sub = x_ref.shape[0] // INPROJ_SUBTILES
    o1, o2 = Q_LORA, Q_LORA + KV_LORA
    kr_tile = o2 // LANE
    gq = gq_ref[...]
    gk = gk_ref[...]
    scale = QK_HEAD_DIM ** -0.5 * LOG2_E
    lane = lax.broadcasted_iota(jnp.int32, (sub, LANE), 1)
    lo_half = lane < LANE // 2

    def project(s):
        rs = slice(s * sub, (s + 1) * sub)
        h = _rms(x_ref[rs, :], gmix_ref[...]).astype(BF16)
        z = _dot(h, win_ref[...])
        tiles = [z[:, (kr_tile + m) * LANE:(kr_tile + m + 1) * LANE] for m in range(N_LANE_TILES)]
        last = z[:, (kr_tile + N_LANE_TILES) * LANE:]
        tiles.append(jnp.concatenate([last, jnp.zeros_like(last)], axis=1))
        sw = [pltpu.roll(t, LANE // 2, 1) for t in tiles]
        cs = slice(s * sub // CHUNK, (s + 1) * sub // CHUNK)
        for kt in range(N_LANE_TILES):
            u_ref[cs, kt, :, :] = jnp.where(lo_half, sw[kt], sw[kt + 1]).reshape(sub // CHUNK, CHUNK, LANE)
        zk = z[:, kr_tile * LANE:(kr_tile + 1) * LANE]
        kr = (jnp.where(lane < ROPE_HALF, zk, 0.0)
              + jnp.where((lane >= LANE // 2) & (lane < LANE // 2 + ROPE_HALF),
                          pltpu.roll(zk, ROPE_HALF, 1), 0.0))
        c_q = _rms(z[:, :o1], gql_ref[...]).astype(BF16)
        q = _dot(c_q, wuq_ref[...])
        c_kv = _rms(z[:, o1:o2], gkvl_ref[...]).astype(BF16)
        kv = _dot(c_kv, wukv_ref[...])
        return q, kv, kr

    def finish_heads(s, q, kv, kr):
        rs = slice(s * sub, (s + 1) * sub)
        v_ref[rs, :] = kv[:, ATTN_WIDTH:].astype(BF16)
        ang = pos_ref[rs, :].astype(F32) * freq_ref[...]
        cos_t = jnp.cos(ang)
        sin_t = jnp.sin(ang) * sgn_ref[...]

        def rope(t):
            return t * cos_t + pltpu.roll(t, LANE // 2, 1) * sin_t

        kr_ss = jnp.sum(kr * kr, axis=-1, keepdims=True)
        kr_rope = rope(kr * gk[:, LANE:])
        for hd in range(N_HEADS):
            qh = q[:, hd * HEAD_PAD:(hd + 1) * HEAD_PAD]
            qn = _rms(qh, gq, width=QK_HEAD_DIM) * scale
            q_ref[rs, hd * HEAD_PAD:hd * HEAD_PAD + LANE] = qn[:, :LANE].astype(BF16)
            q_ref[rs, hd * HEAD_PAD + LANE:(hd + 1) * HEAD_PAD] = rope(qn[:, LANE:]).astype(BF16)
            kn = kv[:, hd * QK_NOPE_DIM:(hd + 1) * QK_NOPE_DIM]
            ss = (jnp.sum(kn * kn, axis=-1, keepdims=True) + kr_ss) * (1.0 / QK_HEAD_DIM)
            rinv = lax.rsqrt(ss + EPS)
            k_ref[rs, hd * HEAD_PAD:hd * HEAD_PAD + LANE] = (kn * rinv * gk[:, :LANE]).astype(BF16)
            k_ref[rs, hd * HEAD_PAD + LANE:(hd + 1) * HEAD_PAD] = (kr_rope * rinv).astype(BF16)

    for s in range(INPROJ_SUBTILES):
        finish_heads(s, *project(s))


def _inproj(x2d, pos2d, gmix, win, gql, wuq, gkvl, wukv, gq, gk, freq, sgn, *, batch, seq, tl):
    nl = seq // tl
    tok = lambda w: pl.BlockSpec((tl, w), lambda b, i: (b * nl + i, 0))
    full = lambda a: pl.BlockSpec(a.shape, lambda b, i: (0,) * a.ndim)
    n_tok = batch * seq
    return pl.pallas_call(
        _inproj_kernel,
        grid=(batch, nl),
        in_specs=[tok(D_MODEL), tok(1), full(gmix), full(win), full(gql), full(wuq), full(gkvl),
                  full(wukv), full(gq), full(gk), full(freq), full(sgn)],
        out_specs=[tok(N_HEADS * HEAD_PAD), tok(N_HEADS * HEAD_PAD), tok(ATTN_WIDTH),
                   pl.BlockSpec((tl // CHUNK, N_LANE_TILES, CHUNK, LANE), lambda b, i: (i, 0, b, 0))],
        out_shape=[jax.ShapeDtypeStruct((n_tok, N_HEADS * HEAD_PAD), BF16),
                   jax.ShapeDtypeStruct((n_tok, N_HEADS * HEAD_PAD), BF16),
                   jax.ShapeDtypeStruct((n_tok, ATTN_WIDTH), BF16),
                   jax.ShapeDtypeStruct((seq // CHUNK, N_LANE_TILES, batch * CHUNK, LANE), F32)],
        compiler_params=_params("arbitrary", "arbitrary"),
    )(x2d, pos2d, gmix, win, gql, wuq, gkvl, wukv, gq, gk, freq, sgn)


def _attn_kernel(q_ref, k_ref, v_ref, o_ref, *, tq):
    seq = q_ref.shape[0]
    n = seq // tq
    vts = [v_ref[:, hd * V_HEAD_DIM:(hd + 1) * V_HEAD_DIM].T for hd in range(ATTN_HEADS_PER_STEP)]
    diag_mask = (lax.broadcasted_iota(jnp.int32, (tq, tq), 0)
                 <= lax.broadcasted_iota(jnp.int32, (tq, tq), 1))

    def scores_t(item):
        hd, i = item
        hs = slice(hd * HEAD_PAD, (hd + 1) * HEAD_PAD)
        return lax.dot_general(k_ref[:(i + 1) * tq, hs], q_ref[i * tq:(i + 1) * tq, hs],
                               (((1,), (1,)), ((), ())), preferred_element_type=F32)

    order = [(hd, i) for i in range(n - 1, -1, -1) for hd in range(ATTN_HEADS_PER_STEP)]
    sts = {item: scores_t(item) for item in order[:QK_AHEAD]}
    for pos, item in enumerate(order):
        hd, i = item
        kv_len = (i + 1) * tq
        st = sts.pop(item)
        if pos + QK_AHEAD < len(order):
            ahead = order[pos + QK_AHEAD]
            sts[ahead] = scores_t(ahead)
        sd = jnp.where(diag_mask, st[kv_len - tq:], -jnp.inf)
        m = jnp.max(sd, axis=0, keepdims=True)
        if i:
            m = jnp.maximum(m, jnp.max(st[:kv_len - tq], axis=0, keepdims=True))
        pd = jnp.exp2(sd - m)
        l = jnp.sum(pd, axis=0, keepdims=True)
        if i:
            pt = jnp.exp2(st[:kv_len - tq] - m)
            l = l + jnp.sum(pt, axis=0, keepdims=True)
            p = jnp.concatenate([pt.astype(BF16), pd.astype(BF16)], axis=0)
        else:
            p = pd.astype(BF16)
        ot = _dot(vts[hd][:, :kv_len], p)
        o_ref[i * tq:(i + 1) * tq, hd * V_HEAD_DIM:(hd + 1) * V_HEAD_DIM] = (ot / l).T.astype(o_ref.dtype)


def _attention(q, k, v, *, batch, seq, tq):
    hps = ATTN_HEADS_PER_STEP
    return pl.pallas_call(
        functools.partial(_attn_kernel, tq=tq),
        grid=(batch, N_HEADS // hps),
        in_specs=[pl.BlockSpec((seq, hps * HEAD_PAD), lambda b, h: (b, h)),
                  pl.BlockSpec((seq, hps * HEAD_PAD), lambda b, h: (b, h)),
                  pl.BlockSpec((seq, hps * V_HEAD_DIM), lambda b, h: (b, h))],
        out_specs=pl.BlockSpec((seq, hps * V_HEAD_DIM), lambda b, h: (b, h)),
        out_shape=jax.ShapeDtypeStruct((batch * seq, ATTN_WIDTH), BF16),
        compiler_params=_params("arbitrary", "arbitrary"),
    )(q, k, v)


def _block_transpose(tiles):
    lane = lax.broadcasted_iota(jnp.int32, tiles[0].shape, 1)
    for d in (2, 1):
        hi = (lane & (PAIR_BLOCK * d)) != 0
        new = list(tiles)
        for i in range(PAIRS_PER_TILE):
            if i & d == 0:
                a, b = tiles[i], tiles[i + d]
                new[i] = jnp.where(hi, pltpu.roll(b, PAIR_BLOCK * d, 1), a)
                new[i + d] = jnp.where(hi, b, pltpu.roll(a, LANE - PAIR_BLOCK * d, 1))
        tiles = new
    return tiles


def _s5_kernel(u_ref, wst_ref, are_ref, aim_ref, toep_ref, vmat_ref, wglu_ref, bglu_ref,
               gout_ref, o_ref, state_ref, x_ref, xs_ref, ys_ref, yg_ref, *, chunks, batch):
    rows_x = chunks * batch
    rows = rows_x * CHUNK

    @pl.when(pl.program_id(0) == 0)
    def _():
        state_ref[...] = jnp.zeros_like(state_ref)

    for kt in range(N_LANE_TILES):
        tiles = [u_ref[:, pl.ds(kt, 1), pl.ds(t, batch, stride=CHUNK), :].reshape(rows_x, LANE)
                 for t in range(CHUNK)]
        for half in range(CHUNK // PAIRS_PER_TILE):
            outs = _block_transpose(tiles[half * PAIRS_PER_TILE:(half + 1) * PAIRS_PER_TILE])
            for j in range(PAIRS_PER_TILE):
                c0 = (kt * PAIRS_PER_TILE + j) * PAIR_W + half * LANE
                x_ref[:, c0:c0 + LANE] = outs[j].astype(BF16)

    for q in range(N_PAIRS):
        cs = slice(q * PAIR_W, (q + 1) * PAIR_W)
        xs_ref[:, cs] = _dot(x_ref[:, cs], wst_ref[q])

    def step(c, carry):
        r0 = pl.multiple_of(c * batch, batch)
        for q in range(N_PAIRS):
            re = slice(q * PAIR_W, q * PAIR_W + LANE)
            im = slice(q * PAIR_W + LANE, (q + 1) * PAIR_W)
            al = slice(q * LANE, (q + 1) * LANE)
            s_re = state_ref[:, re]
            s_im = state_ref[:, im]
            a_re = are_ref[:, al]
            a_im = aim_ref[:, al]
            x_re = xs_ref[pl.ds(r0, batch), re]
            x_im = xs_ref[pl.ds(r0, batch), im]
            xs_ref[pl.ds(r0, batch), re] = s_re
            xs_ref[pl.ds(r0, batch), im] = s_im
            state_ref[:, re] = a_re * s_re - a_im * s_im + x_re
            state_ref[:, im] = a_re * s_im + a_im * s_re + x_im
        return carry

    lax.fori_loop(0, chunks, step, 0)

    for q in range(N_PAIRS):
        cs = slice(q * PAIR_W, (q + 1) * PAIR_W)
        xs_ref[:, cs] = (_dot(x_ref[:, cs], toep_ref[q])
                         + _dot(xs_ref[:, cs].astype(BF16), vmat_ref[q]))

    for kt in range(N_LANE_TILES):
        for half in range(CHUNK // PAIRS_PER_TILE):
            tiles = [xs_ref[:, (kt * PAIRS_PER_TILE + j) * PAIR_W + half * LANE:
                            (kt * PAIRS_PER_TILE + j) * PAIR_W + (half + 1) * LANE]
                     for j in range(PAIRS_PER_TILE)]
            outs = _block_transpose(tiles)
            for t4 in range(PAIRS_PER_TILE):
                t = half * PAIRS_PER_TILE + t4
                ys_ref[:, pl.ds(kt, 1), pl.ds(t, batch, stride=CHUNK), :] = (
                    outs[t4].reshape(chunks, 1, batch, LANE))

    for kt in range(N_LANE_TILES):
        yg_ref[:, kt * LANE:(kt + 1) * LANE] = jax.nn.gelu(ys_ref[:, kt].reshape(rows, LANE), approximate=True)
    y = yg_ref[...]
    gate = _sigmoid(_dot(y.astype(BF16), wglu_ref[...]) + bglu_ref[...])
    o_ref[...] = _rms(y * gate, gout_ref[...]).reshape(chunks, batch, CHUNK, SSM_WIDTH)


def _s5(u_slab, wst, a_re, a_im, toep, vmat, wglu, bglu, gout, *, batch, seq, chunks):
    n_chunks = seq // CHUNK
    rows_x = chunks * batch
    full = lambda a: pl.BlockSpec(a.shape, lambda i: (0,) * a.ndim, pipeline_mode=pl.Buffered(1))
    return pl.pallas_call(
        functools.partial(_s5_kernel, chunks=chunks, batch=batch),
        grid=(n_chunks // chunks,),
        in_specs=[pl.BlockSpec((chunks, N_LANE_TILES, batch * CHUNK, LANE), lambda i: (i, 0, 0, 0)),
                  full(wst), full(a_re), full(a_im), full(toep), full(vmat), full(wglu),
                  full(bglu), full(gout)],
        out_specs=pl.BlockSpec((chunks, batch, CHUNK, SSM_WIDTH), lambda i: (i, 0, 0, 0)),
        out_shape=jax.ShapeDtypeStruct((n_chunks, batch, CHUNK, SSM_WIDTH), F32),
        scratch_shapes=[pltpu.VMEM((batch, N_PAIRS * PAIR_W), F32),
                        pltpu.VMEM((rows_x, N_PAIRS * PAIR_W), BF16),
                        pltpu.VMEM((rows_x, N_PAIRS * PAIR_W), F32),
                        pltpu.VMEM((chunks, N_LANE_TILES, batch * CHUNK, LANE), F32),
                        pltpu.VMEM((rows_x * CHUNK, SSM_WIDTH), F32)],
        compiler_params=_params("arbitrary"),
    )(u_slab, wst, a_re, a_im, toep, vmat, wglu, bglu, gout)


def _oproj_kernel(x_ref, oa_ref, os_ref, ga_ref, woa_ref, wos_ref, out_ref):
    sub = x_ref.shape[0] // OPROJ_SUBTILES
    for s in range(OPROJ_SUBTILES):
        rs = slice(s * sub, (s + 1) * sub)
        oa = _rms(oa_ref[rs, :].astype(F32), ga_ref[...]).astype(BF16)
        os_ = os_ref[s * sub // CHUNK:(s + 1) * sub // CHUNK].reshape(sub, SSM_WIDTH).astype(BF16)
        out_ref[rs, :] = x_ref[rs, :] + _dot(oa, woa_ref[...]) + _dot(os_, wos_ref[...])


def _oproj(x2d, o_attn, o_ssm, g_attn, wo_a, wo_s, *, batch, seq, tl):
    nl = seq // tl
    tok = lambda w: pl.BlockSpec((tl, w), lambda b, i: (b * nl + i, 0))
    full = lambda a: pl.BlockSpec(a.shape, lambda b, i: (0,) * a.ndim)
    return pl.pallas_call(
        _oproj_kernel,
        grid=(batch, nl),
        in_specs=[tok(D_MODEL), tok(ATTN_WIDTH),
                  pl.BlockSpec((tl // CHUNK, None, CHUNK, SSM_WIDTH), lambda b, i: (i, b, 0, 0)),
                  full(g_attn), full(wo_a), full(wo_s)],
        out_specs=tok(D_MODEL),
        out_shape=jax.ShapeDtypeStruct((batch * seq, D_MODEL), F32),
        compiler_params=_params("arbitrary", "arbitrary"),
    )(x2d, o_attn, o_ssm, g_attn, wo_a, wo_s)


def _ffn_kernel(x_ref, g_ref, wg_ref, wu_ref, wd_ref, out_ref, h_ref):
    j = pl.program_id(1)

    def ff_block(h):
        gate = _dot(h, wg_ref[...])
        up = _dot(h, wu_ref[...])
        act = (gate * _sigmoid(gate) * up).astype(BF16)
        return _dot(act, wd_ref[...])

    @pl.when(j == 0)
    def _():
        sub = x_ref.shape[0] // FFN_SUBTILES
        for s in range(FFN_SUBTILES):
            rs = slice(s * sub, (s + 1) * sub)
            x = x_ref[rs, :]
            h = _rms(x, g_ref[...]).astype(BF16)
            h_ref[rs, :] = h
            out_ref[rs, :] = x + ff_block(h)

    @pl.when(j > 0)
    def _():
        out_ref[...] += ff_block(h_ref[...])


def _ffn(x2d, g, wg, wu, wd, *, tm, tf):
    n_tok = x2d.shape[0]
    return pl.pallas_call(
        _ffn_kernel,
        grid=(n_tok // tm, D_FF // tf),
        in_specs=[pl.BlockSpec((tm, D_MODEL), lambda i, j: (i, 0)),
                  pl.BlockSpec((1, D_MODEL), lambda i, j: (0, 0)),
                  pl.BlockSpec((D_MODEL, tf), lambda i, j: (0, j)),
                  pl.BlockSpec((D_MODEL, tf), lambda i, j: (0, j)),
                  pl.BlockSpec((tf, D_MODEL), lambda i, j: (j, 0))],
        out_specs=pl.BlockSpec((tm, D_MODEL), lambda i, j: (i, 0)),
        out_shape=jax.ShapeDtypeStruct((n_tok, D_MODEL), F32),
        scratch_shapes=[pltpu.VMEM((tm, D_MODEL), BF16)],
        compiler_params=_params("arbitrary", "arbitrary"),
    )(x2d, g, wg, wu, wd)


def _ple_kernel(x_ref, p_ref, g_ref, wpg_ref, wpp_ref, out_ref):
    sub = x_ref.shape[0] // PLE_SUBTILES
    for s in range(PLE_SUBTILES):
        rs = slice(s * sub, (s + 1) * sub)
        x = x_ref[rs, :]
        h = _rms(x, g_ref[...]).astype(BF16)
        gate = _sigmoid(_dot(h, wpg_ref[...]))
        out_ref[rs, :] = x + gate * _dot(p_ref[rs, :].astype(BF16), wpp_ref[...])


def _ple(x2d, p2d, g, wpg, wpp, *, tm):
    n_tok = x2d.shape[0]
    full = lambda a: pl.BlockSpec(a.shape, lambda i: (0,) * a.ndim)
    return pl.pallas_call(
        _ple_kernel,
        grid=(n_tok // tm,),
        in_specs=[pl.BlockSpec((tm, D_MODEL), lambda i: (i, 0)),
                  pl.BlockSpec((tm, PLE_DIM), lambda i: (i, 0)),
                  full(g), full(wpg), full(wpp)],
        out_specs=pl.BlockSpec((tm, D_MODEL), lambda i: (i, 0)),
        out_shape=jax.ShapeDtypeStruct((n_tok, D_MODEL), F32),
        compiler_params=_params("arbitrary"),
    )(x2d, p2d, g, wpg, wpp)


def _rope_tile(t):
    z = jnp.zeros(t.shape[:-1] + (ROPE_HALF,), t.dtype)
    return jnp.concatenate([t[..., :ROPE_HALF], z, t[..., ROPE_HALF:], z], axis=-1)


def _head_gain(g):
    return jnp.concatenate([g[:QK_NOPE_DIM], _rope_tile(g[QK_NOPE_DIM:])])[None, :].astype(F32)


def _s5_params(lam_re, lam_im, log_dt, b_re, b_im, c_re, c_im, d_skip):
    G, P, H, T, NQ = SSM_GROUPS, SSM_STATE, SSM_GROUP, CHUNK, N_PAIRS
    TH = T * H
    lr = jnp.minimum(lam_re.astype(F32), -1e-4)
    li = lam_im.astype(F32)
    dt = jnp.exp(log_dt.astype(F32))[:, None]
    mag = jnp.exp(lr * dt)
    abar_re = mag * jnp.cos(li * dt)
    abar_im = mag * jnp.sin(li * dt)
    den = lr * lr + li * li
    num_re = abar_re - 1.0
    num_im = abar_im
    coef_re = ((num_re * lr + num_im * li) / den)[:, None, :]
    coef_im = ((num_im * lr - num_re * li) / den)[:, None, :]
    br = b_re.astype(F32).transpose(0, 2, 1)
    bim = b_im.astype(F32).transpose(0, 2, 1)
    bb_re = coef_re * br - coef_im * bim
    bb_im = coef_re * bim + coef_im * br
    pw_re, pw_im = [jnp.ones_like(abar_re)], [jnp.zeros_like(abar_im)]
    for _ in range(T):
        r, i = pw_re[-1], pw_im[-1]
        pw_re.append(r * abar_re - i * abar_im)
        pw_im.append(r * abar_im + i * abar_re)
    pw_re = jnp.stack(pw_re, axis=1)[:, :, None, :]
    pw_im = jnp.stack(pw_im, axis=1)[:, :, None, :]
    cr = c_re.astype(F32)[:, None]
    ci = c_im.astype(F32)[:, None]
    ca_re = cr * pw_re - ci * pw_im
    ca_im = cr * pw_im + ci * pw_re
    cat = lambda re, im: jnp.concatenate([re.reshape(G, -1, P), im.reshape(G, -1, P)], axis=-1)
    m1 = jnp.einsum('gik,gjk->gij', cat(bb_re, bb_im), cat(ca_re[:, :T], -ca_im[:, :T]),
                    precision=lax.Precision.HIGHEST)
    m1 = m1.at[:, :, :H].add(jnp.eye(H, dtype=F32)[None] * d_skip.astype(F32)[:, None, :])
    toep = jnp.stack([jnp.pad(m1[:, :, :TH - H * t], ((0, 0), (0, 0), (H * t, 0))) for t in range(T)],
                     axis=1).reshape(G, TH, TH)
    rev_re = pw_re[:, :T][:, ::-1]
    rev_im = pw_im[:, :T][:, ::-1]
    w_re = (rev_re * bb_re[:, None] - rev_im * bb_im[:, None]).reshape(NQ, 2, TH, P)
    w_im = (rev_re * bb_im[:, None] + rev_im * bb_re[:, None]).reshape(NQ, 2, TH, P)
    zw = jnp.zeros((NQ, TH, P), F32)
    wst_p = jnp.concatenate([
        jnp.concatenate([w_re[:, 0], zw, w_im[:, 0], zw], axis=-1),
        jnp.concatenate([zw, w_re[:, 1], zw, w_im[:, 1]], axis=-1)], axis=1)
    tp = toep.reshape(NQ, 2, TH, TH)
    zt = jnp.zeros((NQ, TH, TH), F32)
    toep_p = jnp.concatenate([jnp.concatenate([tp[:, 0], zt], axis=-1),
                              jnp.concatenate([zt, tp[:, 1]], axis=-1)], axis=1)
    v_t = cat(ca_re[:, 1:], -ca_im[:, 1:]).transpose(0, 2, 1).reshape(NQ, 2, 2, P, TH)
    zv = jnp.zeros((NQ, P, TH), F32)
    vmat_p = jnp.concatenate([
        jnp.concatenate([v_t[:, 0, 0], zv], axis=-1), jnp.concatenate([zv, v_t[:, 1, 0]], axis=-1),
        jnp.concatenate([v_t[:, 0, 1], zv], axis=-1), jnp.concatenate([zv, v_t[:, 1, 1]], axis=-1)], axis=1)
    old = jnp.arange(2 * TH).reshape(2, T, H)
    perm = jax.nn.one_hot(old.transpose(1, 0, 2).reshape(-1), 2 * TH, dtype=BF16).T
    wst_p, toep_p, vmat_p = wst_p.astype(BF16), toep_p.astype(BF16), vmat_p.astype(BF16)
    wst_p = jnp.einsum('ia,qij->qaj', perm, wst_p, preferred_element_type=F32)
    toep_p = jnp.einsum('ia,qij,jb->qab', perm, toep_p, perm, preferred_element_type=F32)
    vmat_p = jnp.einsum('qij,jb->qib', vmat_p, perm, preferred_element_type=F32)
    a_re = jnp.broadcast_to(pw_re[:, T].reshape(1, G * P), (SUBLANE, G * P))
    a_im = jnp.broadcast_to(pw_im[:, T].reshape(1, G * P), (SUBLANE, G * P))
    return wst_p.astype(BF16), a_re, a_im, toep_p.astype(BF16), vmat_p.astype(BF16)


def kernel(x, p, positions, g_mix_norm, w_in, g_q_lora, w_uq, g_kv_lora, w_ukv, g_q_head, g_k_head,
           lam_re, lam_im, log_dt, b_re, b_im, c_re, c_im, d_skip, w_glu, b_glu, g_out_attn,
           g_out_ssm, w_o, g_ffn_norm, w_gate, w_up, w_down, g_ple_norm, w_ple_gate, w_ple_proj):
    batch, seq, _ = x.shape
    assert batch == SUBLANE, "the S5 chunk recurrence keeps one batch row per sublane"
    depth = w_in.shape[0]
    n_tok = batch * seq
    row = lambda g: g[None, :].astype(F32)

    inv_freq = 1.0 / (ROPE_THETA ** (jnp.arange(0, QK_ROPE_DIM, 2, dtype=F32) / QK_ROPE_DIM))
    freq = _rope_tile(jnp.concatenate([inv_freq, inv_freq]))[None, :]
    ones = jnp.ones((ROPE_HALF,), F32)
    sgn = _rope_tile(jnp.concatenate([-ones, ones]))[None, :]
    pos2d = positions.reshape(n_tok, 1)

    x2d = x.reshape(n_tok, D_MODEL)
    for i in range(depth):
        win = w_in[i].astype(BF16)
        wq = w_uq[i].reshape(Q_LORA, N_HEADS, QK_HEAD_DIM)
        wuq = jnp.concatenate([wq[..., :QK_NOPE_DIM], _rope_tile(wq[..., QK_NOPE_DIM:])], axis=-1)
        wuq = wuq.reshape(Q_LORA, N_HEADS * HEAD_PAD).astype(BF16)
        wkv = w_ukv[i].reshape(KV_LORA, N_HEADS, QK_NOPE_DIM + V_HEAD_DIM)
        wukv = jnp.concatenate([wkv[..., :QK_NOPE_DIM].reshape(KV_LORA, -1),
                                wkv[..., QK_NOPE_DIM:].reshape(KV_LORA, -1)], axis=1).astype(BF16)

        q, k, v, u_slab = _inproj(x2d, pos2d, row(g_mix_norm[i]), win, row(g_q_lora[i]), wuq,
                                  row(g_kv_lora[i]), wukv, _head_gain(g_q_head[i]),
                                  _head_gain(g_k_head[i]), freq, sgn, batch=batch, seq=seq, tl=INPROJ_TILE)
        o_attn = _attention(q, k, v, batch=batch, seq=seq, tq=ATTN_Q_TILE)

        wst, a_re, a_im, toep, vmat = _s5_params(lam_re[i], lam_im[i], log_dt[i], b_re[i], b_im[i],
                                                 c_re[i], c_im[i], d_skip[i])
        o_ssm = _s5(u_slab, wst, a_re, a_im, toep, vmat,
                    w_glu[i].astype(BF16), row(b_glu[i]), row(g_out_ssm[i]),
                    batch=batch, seq=seq, chunks=S5_CHUNKS_PER_STEP)

        wo = w_o[i].astype(BF16)
        x2d = _oproj(x2d, o_attn, o_ssm, row(g_out_attn[i]), wo[:ATTN_WIDTH], wo[ATTN_WIDTH:],
                     batch=batch, seq=seq, tl=OPROJ_TILE)
        x2d = _ffn(x2d, row(g_ffn_norm[i]), w_gate[i].astype(BF16), w_up[i].astype(BF16),
                   w_down[i].astype(BF16), tm=FFN_TOKEN_TILE, tf=FFN_FF_TILE)
        x2d = _ple(x2d, p[i].reshape(n_tok, PLE_DIM), row(g_ple_norm[i]),
                   w_ple_gate[i].astype(BF16), w_ple_proj[i].astype(BF16), tm=PLE_TILE)
    return x2d.reshape(batch, seq, D_MODEL)
```

```python
import functools

import jax
import jax.numpy as jnp
from jax import lax
from jax.experimental import pallas as pl
from jax.experimental.pallas import tpu as pltpu

D_MODEL = 2048
PLE_DIM = 256
N_HEADS = 8
QK_NOPE_DIM = 128
QK_ROPE_DIM = 64
V_HEAD_DIM = 128
QK_HEAD_DIM = QK_NOPE_DIM + QK_ROPE_DIM
Q_LORA = 512
KV_LORA = 256
ATTN_WIDTH = N_HEADS * V_HEAD_DIM
ROPE_THETA = 10000.0
SSM_WIDTH = 1024
SSM_GROUP = 16
SSM_GROUPS = SSM_WIDTH // SSM_GROUP
SSM_STATE = 64
D_FF = 5632
EPS = 1e-6

LANE = 128
SUBLANE = 8
HEAD_PAD = 2 * LANE
ROPE_HALF = QK_ROPE_DIM // 2
CHUNK = SUBLANE
N_LANE_TILES = SSM_WIDTH // LANE
PAIR_BLOCK = 2 * SSM_GROUP
PAIRS_PER_TILE = LANE // PAIR_BLOCK
PAIR_W = 2 * LANE
N_PAIRS = SSM_GROUPS // 2
VMEM_LIMIT = 56 * 1024 * 1024
LOG2_E = 1.4426950408889634
INPROJ_TILE = 512
INPROJ_SUBTILES = 2
ATTN_Q_TILE = 256
ATTN_HEADS_PER_STEP = 2
QK_AHEAD = 4
S5_CHUNKS_PER_STEP = 16
OPROJ_TILE = 1024
OPROJ_SUBTILES = 4
FFN_TOKEN_TILE = 1024
FFN_FF_TILE = 512
FFN_SUBTILES = 2
PLE_TILE = 1024
PLE_SUBTILES = 2

BF16 = jnp.bfloat16
F32 = jnp.float32


def _rms(t, g, width=None):
    n = t.shape[-1] if width is None else width
    ss = jnp.sum(t * t, axis=-1, keepdims=True) * (1.0 / n)
    return t * lax.rsqrt(ss + EPS) * g


def _dot(a, b):
    return jnp.dot(a, b, preferred_element_type=F32)


def _sigmoid(t):
    return 1.0 / (1.0 + jnp.exp(-t))


def _params(*sem):
    return pltpu.CompilerParams(dimension_semantics=sem, vmem_limit_bytes=VMEM_LIMIT)


def _inproj_kernel(x_ref, pos_ref, gmix_ref, win_ref, gql_ref, wuq_ref, gkvl_ref, wukv_ref,
                   gq_ref, gk_ref, freq_ref, sgn_ref,
                   q_ref, k_ref, v_ref, u_ref):
    sub = x_ref.shape[0] // INPROJ_SUBTILES
    o1, o2 = Q_LORA, Q_LORA + KV_LORA
    kr_tile = o2 // LANE
    gq = gq_ref[...]
    gk = gk_ref[...]
    scale = QK_HEAD_DIM ** -0.5 * LOG2_E
    lane = lax.broadcasted_iota(jnp.int32, (sub, LANE), 1)
    lo_half = lane < LANE // 2

    def project(s):
        rs = slice(s * sub, (s + 1) * sub)
        h = _rms(x_ref[rs, :], gmix_ref[...]).astype(BF16)
        z = _dot(h, win_ref[...])
        tiles = [z[:, (kr_tile + m) * LANE:(kr_tile + m + 1) * LANE] for m in range(N_LANE_TILES)]
        last = z[:, (kr_tile + N_LANE_TILES) * LANE:]
        tiles.append(jnp.concatenate([last, jnp.zeros_like(last)], axis=1))
        sw = [pltpu.roll(t, LANE // 2, 1) for t in tiles]
        cs = slice(s * sub // CHUNK, (s + 1) * sub // CHUNK)
        for kt in range(N_LANE_TILES):
            u_ref[cs, kt, :, :] = jnp.where(lo_half, sw[kt], sw[kt + 1]).reshape(sub // CHUNK, CHUNK, LANE)
        zk = z[:, kr_tile * LANE:(kr_tile + 1) * LANE]
        kr = (jnp.where(lane < ROPE_HALF, zk, 0.0)
              + jnp.where((lane >= LANE // 2) & (lane < LANE // 2 + ROPE_HALF),
                          pltpu.roll(zk, ROPE_HALF, 1), 0.0))
        c_q = _rms(z[:, :o1], gql_ref[...]).astype(BF16)
        q = _dot(c_q, wuq_ref[...])
        c_kv = _rms(z[:, o1:o2], gkvl_ref[...]).astype(BF16)
        kv = _dot(c_kv, wukv_ref[...])
        return q, kv, kr

    def finish_heads(s, q, kv, kr):
        rs = slice(s * sub, (s + 1) * sub)
        v_ref[rs, :] = kv[:, ATTN_WIDTH:].astype(BF16)
        ang = pos_ref[rs, :].astype(F32) * freq_ref[...]
        cos_t = jnp.cos(ang)
        sin_t = jnp.sin(ang) * sgn_ref[...]

        def rope(t):
            return t * cos_t + pltpu.roll(t, LANE // 2, 1) * sin_t

        kr_ss = jnp.sum(kr * kr, axis=-1, keepdims=True)
        kr_rope = rope(kr * gk[:, LANE:])
        for hd in range(N_HEADS):
            qh = q[:, hd * HEAD_PAD:(hd + 1) * HEAD_PAD]
            qn = _rms(qh, gq, width=QK_HEAD_DIM) * scale
            q_ref[rs, hd * HEAD_PAD:hd * HEAD_PAD + LANE] = qn[:, :LANE].astype(BF16)
            q_ref[rs, hd * HEAD_PAD + LANE:(hd + 1) * HEAD_PAD] = rope(qn[:, LANE:]).astype(BF16)
            kn = kv[:, hd * QK_NOPE_DIM:(hd + 1) * QK_NOPE_DIM]
            ss = (jnp.sum(kn * kn, axis=-1, keepdims=True) + kr_ss) * (1.0 / QK_HEAD_DIM)
            rinv = lax.rsqrt(ss + EPS)
            k_ref[rs, hd * HEAD_PAD:hd * HEAD_PAD + LANE] = (kn * rinv * gk[:, :LANE]).astype(BF16)
            k_ref[rs, hd * HEAD_PAD + LANE:(hd + 1) * HEAD_PAD] = (kr_rope * rinv).astype(BF16)

    for s in range(INPROJ_SUBTILES):
        finish_heads(s, *project(s))


def _inproj(x2d, pos2d, gmix, win, gql, wuq, gkvl, wukv, gq, gk, freq, sgn, *, batch, seq, tl):
    nl = seq // tl
    tok = lambda w: pl.BlockSpec((tl, w), lambda b, i: (b * nl + i, 0))
    full = lambda a: pl.BlockSpec(a.shape, lambda b, i: (0,) * a.ndim)
    n_tok = batch * seq
    return pl.pallas_call(
        _inproj_kernel,
        grid=(batch, nl),
        in_specs=[tok(D_MODEL), tok(1), full(gmix), full(win), full(gql), full(wuq), full(gkvl),
                  full(wukv), full(gq), full(gk), full(freq), full(sgn)],
        out_specs=[tok(N_HEADS * HEAD_PAD), tok(N_HEADS * HEAD_PAD), tok(ATTN_WIDTH),
                   pl.BlockSpec((tl // CHUNK, N_LANE_TILES, CHUNK, LANE), lambda b, i: (i, 0, b, 0))],
        out_shape=[jax.ShapeDtypeStruct((n_tok, N_HEADS * HEAD_PAD), BF16),
                   jax.ShapeDtypeStruct((n_tok, N_HEADS * HEAD_PAD), BF16),
                   jax.ShapeDtypeStruct((n_tok, ATTN_WIDTH), BF16),
                   jax.ShapeDtypeStruct((seq // CHUNK, N_LANE_TILES, batch * CHUNK, LANE), F32)],
        compiler_params=_params("arbitrary", "arbitrary"),
    )(x2d, pos2d, gmix, win, gql, wuq, gkvl, wukv, gq, gk, freq, sgn)


def _attn_kernel(q_ref, k_ref, v_ref, o_ref, *, tq):
    seq = q_ref.shape[0]
    n = seq // tq
    vts = [v_ref[:, hd * V_HEAD_DIM:(hd + 1) * V_HEAD_DIM].T for hd in range(ATTN_HEADS_PER_STEP)]
    diag_mask = (lax.broadcasted_iota(jnp.int32, (tq, tq), 0)
                 <= lax.broadcasted_iota(jnp.int32, (tq, tq), 1))

    def scores_t(item):
        hd, i = item
        hs = slice(hd * HEAD_PAD, (hd + 1) * HEAD_PAD)
        return lax.dot_general(k_ref[:(i + 1) * tq, hs], q_ref[i * tq:(i + 1) * tq, hs],
                               (((1,), (1,)), ((), ())), preferred_element_type=F32)

    order = [(hd, i) for i in range(n - 1, -1, -1) for hd in range(ATTN_HEADS_PER_STEP)]
    sts = {item: scores_t(item) for item in order[:QK_AHEAD]}
    for pos, item in enumerate(order):
        hd, i = item
        kv_len = (i + 1) * tq
        st = sts.pop(item)
        if pos + QK_AHEAD < len(order):
            ahead = order[pos + QK_AHEAD]
            sts[ahead] = scores_t(ahead)
        sd = jnp.where(diag_mask, st[kv_len - tq:], -jnp.inf)
        m = jnp.max(sd, axis=0, keepdims=True)
        if i:
            m = jnp.maximum(m, jnp.max(st[:kv_len - tq], axis=0, keepdims=True))
        pd = jnp.exp2(sd - m)
        l = jnp.sum(pd, axis=0, keepdims=True)
        if i:
            pt = jnp.exp2(st[:kv_len - tq] - m)
            l = l + jnp.sum(pt, axis=0, keepdims=True)
            p = jnp.concatenate([pt.astype(BF16), pd.astype(BF16)], axis=0)
        else:
            p = pd.astype(BF16)
        ot = _dot(vts[hd][:, :kv_len], p)
        o_ref[i * tq:(i + 1) * tq, hd * V_HEAD_DIM:(hd + 1) * V_HEAD_DIM] = (ot / l).T.astype(o_ref.dtype)


def _attention(q, k, v, *, batch, seq, tq):
    hps = ATTN_HEADS_PER_STEP
    return pl.pallas_call(
        functools.partial(_attn_kernel, tq=tq),
        grid=(batch, N_HEADS // hps),
        in_specs=[pl.BlockSpec((seq, hps * HEAD_PAD), lambda b, h: (b, h)),
                  pl.BlockSpec((seq, hps * HEAD_PAD), lambda b, h: (b, h)),
                  pl.BlockSpec((seq, hps * V_HEAD_DIM), lambda b, h: (b, h))],
        out_specs=pl.BlockSpec((seq, hps * V_HEAD_DIM), lambda b, h: (b, h)),
        out_shape=jax.ShapeDtypeStruct((batch * seq, ATTN_WIDTH), BF16),
        compiler_params=_params("arbitrary", "arbitrary"),
    )(q, k, v)


def _block_transpose(tiles):
    lane = lax.broadcasted_iota(jnp.int32, tiles[0].shape, 1)
    for d in (2, 1):
        hi = (lane & (PAIR_BLOCK * d)) != 0
        new = list(tiles)
        for i in range(PAIRS_PER_TILE):
            if i & d == 0:
                a, b = tiles[i], tiles[i + d]
                new[i] = jnp.where(hi, pltpu.roll(b, PAIR_BLOCK * d, 1), a)
                new[i + d] = jnp.where(hi, b, pltpu.roll(a, LANE - PAIR_BLOCK * d, 1))
        tiles = new
    return tiles


def _s5_kernel(u_ref, wst_ref, are_ref, aim_ref, toep_ref, vmat_ref, wglu_ref, bglu_ref,
               gout_ref, o_ref, state_ref, x_ref, xs_ref, ys_ref, yg_ref, *, chunks, batch):
    rows_x = chunks * batch
    rows = rows_x * CHUNK

    @pl.when(pl.program_id(0) == 0)
    def _():
        state_ref[...] = jnp.zeros_like(state_ref)

    for kt in range(N_LANE_TILES):
        tiles = [u_ref[:, pl.ds(kt, 1), pl.ds(t, batch, stride=CHUNK), :].reshape(rows_x, LANE)
                 for t in range(CHUNK)]
        for half in range(CHUNK // PAIRS_PER_TILE):
            outs = _block_transpose(tiles[half * PAIRS_PER_TILE:(half + 1) * PAIRS_PER_TILE])
            for j in range(PAIRS_PER_TILE):
                c0 = (kt * PAIRS_PER_TILE + j) * PAIR_W + half * LANE
                x_ref[:, c0:c0 + LANE] = outs[j].astype(BF16)

    for q in range(N_PAIRS):
        cs = slice(q * PAIR_W, (q + 1) * PAIR_W)
        xs_ref[:, cs] = _dot(x_ref[:, cs], wst_ref[q])

    def step(c, carry):
        r0 = pl.multiple_of(c * batch, batch)
        for q in range(N_PAIRS):
            re = slice(q * PAIR_W, q * PAIR_W + LANE)
            im = slice(q * PAIR_W + LANE, (q + 1) * PAIR_W)
            al = slice(q * LANE, (q + 1) * LANE)
            s_re = state_ref[:, re]
            s_im = state_ref[:, im]
            a_re = are_ref[:, al]
            a_im = aim_ref[:, al]
            x_re = xs_ref[pl.ds(r0, batch), re]
            x_im = xs_ref[pl.ds(r0, batch), im]
            xs_ref[pl.ds(r0, batch), re] = s_re
            xs_ref[pl.ds(r0, batch), im] = s_im
            state_ref[:, re] = a_re * s_re - a_im * s_im + x_re
            state_ref[:, im] = a_re * s_im + a_im * s_re + x_im
        return carry

    lax.fori_loop(0, chunks, step, 0, unroll=True)

    for q in range(N_PAIRS):
        cs = slice(q * PAIR_W, (q + 1) * PAIR_W)
        xs_ref[:, cs] = (_dot(x_ref[:, cs], toep_ref[q])
                         + _dot(xs_ref[:, cs].astype(BF16), vmat_ref[q]))

    for kt in range(N_LANE_TILES):
        for half in range(CHUNK // PAIRS_PER_TILE):
            tiles = [xs_ref[:, (kt * PAIRS_PER_TILE + j) * PAIR_W + half * LANE:
                            (kt * PAIRS_PER_TILE + j) * PAIR_W + (half + 1) * LANE]
                     for j in range(PAIRS_PER_TILE)]
            outs = _block_transpose(tiles)
            for t4 in range(PAIRS_PER_TILE):
                t = half * PAIRS_PER_TILE + t4
                ys_ref[:, pl.ds(kt, 1), pl.ds(t, batch, stride=CHUNK), :] = (
                    outs[t4].reshape(chunks, 1, batch, LANE))

    for kt in range(N_LANE_TILES):
        yg_ref[:, kt * LANE:(kt + 1) * LANE] = jax.nn.gelu(ys_ref[:, kt].reshape(rows, LANE), approximate=True)
    y = yg_ref[...]
    gate = _sigmoid(_dot(y.astype(BF16), wglu_ref[...]) + bglu_ref[...])
    o_ref[...] = _rms(y * gate, gout_ref[...]).reshape(chunks, batch, CHUNK, SSM_WIDTH)


def _s5(u_slab, wst, a_re, a_im, toep, vmat, wglu, bglu, gout, *, batch, seq, chunks):
    n_chunks = seq // CHUNK
    rows_x = chunks * batch
    full = lambda a: pl.BlockSpec(a.shape, lambda i: (0,) * a.ndim, pipeline_mode=pl.Buffered(1))
    return pl.pallas_call(
        functools.partial(_s5_kernel, chunks=chunks, batch=batch),
        grid=(n_chunks // chunks,),
        in_specs=[pl.BlockSpec((chunks, N_LANE_TILES, batch * CHUNK, LANE), lambda i: (i, 0, 0, 0)),
                  full(wst), full(a_re), full(a_im), full(toep), full(vmat), full(wglu),
                  full(bglu), full(gout)],
        out_specs=pl.BlockSpec((chunks, batch, CHUNK, SSM_WIDTH), lambda i: (i, 0, 0, 0)),
        out_shape=jax.ShapeDtypeStruct((n_chunks, batch, CHUNK, SSM_WIDTH), F32),
        scratch_shapes=[pltpu.VMEM((batch, N_PAIRS * PAIR_W), F32),
                        pltpu.VMEM((rows_x, N_PAIRS * PAIR_W), BF16),
                        pltpu.VMEM((rows_x, N_PAIRS * PAIR_W), F32),
                        pltpu.VMEM((chunks, N_LANE_TILES, batch * CHUNK, LANE), F32),
                        pltpu.VMEM((rows_x * CHUNK, SSM_WIDTH), F32)],
        compiler_params=_params("arbitrary"),
    )(u_slab, wst, a_re, a_im, toep, vmat, wglu, bglu, gout)


def _oproj_kernel(x_ref, oa_ref, os_ref, ga_ref, woa_ref, wos_ref, out_ref):
    sub = x_ref.shape[0] // OPROJ_SUBTILES
    for s in range(OPROJ_SUBTILES):
        rs = slice(s * sub, (s + 1) * sub)
        oa = _rms(oa_ref[rs, :].astype(F32), ga_ref[...]).astype(BF16)
        os_ = os_ref[s * sub // CHUNK:(s + 1) * sub // CHUNK].reshape(sub, SSM_WIDTH).astype(BF16)
        out_ref[rs, :] = x_ref[rs, :] + _dot(oa, woa_ref[...]) + _dot(os_, wos_ref[...])


def _oproj(x2d, o_attn, o_ssm, g_attn, wo_a, wo_s, *, batch, seq, tl):
    nl = seq // tl
    tok = lambda w: pl.BlockSpec((tl, w), lambda b, i: (b * nl + i, 0))
    full = lambda a: pl.BlockSpec(a.shape, lambda b, i: (0,) * a.ndim)
    return pl.pallas_call(
        _oproj_kernel,
        grid=(batch, nl),
        in_specs=[tok(D_MODEL), tok(ATTN_WIDTH),
                  pl.BlockSpec((tl // CHUNK, None, CHUNK, SSM_WIDTH), lambda b, i: (i, b, 0, 0)),
                  full(g_attn), full(wo_a), full(wo_s)],
        out_specs=tok(D_MODEL),
        out_shape=jax.ShapeDtypeStruct((batch * seq, D_MODEL), F32),
        compiler_params=_params("arbitrary", "arbitrary"),
    )(x2d, o_attn, o_ssm, g_attn, wo_a, wo_s)


def _ffn_kernel(x_ref, g_ref, wg_ref, wu_ref, wd_ref, out_ref, h_ref):
    j = pl.program_id(1)

    def ff_block(h):
        gate = _dot(h, wg_ref[...])
        up = _dot(h, wu_ref[...])
        act = (gate * _sigmoid(gate) * up).astype(BF16)
        return _dot(act, wd_ref[...])

    @pl.when(j == 0)
    def _():
        sub = x_ref.shape[0] // FFN_SUBTILES
        for s in range(FFN_SUBTILES):
            rs = slice(s * sub, (s + 1) * sub)
            x = x_ref[rs, :]
            h = _rms(x, g_ref[...]).astype(BF16)
            h_ref[rs, :] = h
            out_ref[rs, :] = x + ff_block(h)

    @pl.when(j > 0)
    def _():
        out_ref[...] += ff_block(h_ref[...])


def _ffn(x2d, g, wg, wu, wd, *, tm, tf):
    n_tok = x2d.shape[0]
    return pl.pallas_call(
        _ffn_kernel,
        grid=(n_tok // tm, D_FF // tf),
        in_specs=[pl.BlockSpec((tm, D_MODEL), lambda i, j: (i, 0)),
                  pl.BlockSpec((1, D_MODEL), lambda i, j: (0, 0)),
                  pl.BlockSpec((D_MODEL, tf), lambda i, j: (0, j)),
                  pl.BlockSpec((D_MODEL, tf), lambda i, j: (0, j)),
                  pl.BlockSpec((tf, D_MODEL), lambda i, j: (j, 0))],
        out_specs=pl.BlockSpec((tm, D_MODEL), lambda i, j: (i, 0)),
        out_shape=jax.ShapeDtypeStruct((n_tok, D_MODEL), F32),
        scratch_shapes=[pltpu.VMEM((tm, D_MODEL), BF16)],
        compiler_params=_params("arbitrary", "arbitrary"),
    )(x2d, g, wg, wu, wd)


def _ple_kernel(x_ref, p_ref, g_ref, wpg_ref, wpp_ref, out_ref):
    sub = x_ref.shape[0] // PLE_SUBTILES
    for s in range(PLE_SUBTILES):
        rs = slice(s * sub, (s + 1) * sub)
        x = x_ref[rs, :]
        h = _rms(x, g_ref[...]).astype(BF16)
        gate = _sigmoid(_dot(h, wpg_ref[...]))
        out_ref[rs, :] = x + gate * _dot(p_ref[rs, :].astype(BF16), wpp_ref[...])


def _ple(x2d, p2d, g, wpg, wpp, *, tm):
    n_tok = x2d.shape[0]
    full = lambda a: pl.BlockSpec(a.shape, lambda i: (0,) * a.ndim)
    return pl.pallas_call(
        _ple_kernel,
        grid=(n_tok // tm,),
        in_specs=[pl.BlockSpec((tm, D_MODEL), lambda i: (i, 0)),
                  pl.BlockSpec((tm, PLE_DIM), lambda i: (i, 0)),
                  full(g), full(wpg), full(wpp)],
        out_specs=pl.BlockSpec((tm, D_MODEL), lambda i: (i, 0)),
        out_shape=jax.ShapeDtypeStruct((n_tok, D_MODEL), F32),
        compiler_params=_params("arbitrary"),
    )(x2d, p2d, g, wpg, wpp)


def _rope_tile(t):
    z = jnp.zeros(t.shape[:-1] + (ROPE_HALF,), t.dtype)
    return jnp.concatenate([t[..., :ROPE_HALF], z, t[..., ROPE_HALF:], z], axis=-1)


def _head_gain(g):
    return jnp.concatenate([g[:QK_NOPE_DIM], _rope_tile(g[QK_NOPE_DIM:])])[None, :].astype(F32)


def _s5_params(lam_re, lam_im, log_dt, b_re, b_im, c_re, c_im, d_skip):
    G, P, H, T, NQ = SSM_GROUPS, SSM_STATE, SSM_GROUP, CHUNK, N_PAIRS
    TH = T * H
    lr = jnp.minimum(lam_re.astype(F32), -1e-4)
    li = lam_im.astype(F32)
    dt = jnp.exp(log_dt.astype(F32))[:, None]
    mag = jnp.exp(lr * dt)
    abar_re = mag * jnp.cos(li * dt)
    abar_im = mag * jnp.sin(li * dt)
    den = lr * lr + li * li
    num_re = abar_re - 1.0
    num_im = abar_im
    coef_re = ((num_re * lr + num_im * li) / den)[:, None, :]
    coef_im = ((num_im * lr - num_re * li) / den)[:, None, :]
    br = b_re.astype(F32).transpose(0, 2, 1)
    bim = b_im.astype(F32).transpose(0, 2, 1)
    bb_re = coef_re * br - coef_im * bim
    bb_im = coef_re * bim + coef_im * br
    pw_re, pw_im = [jnp.ones_like(abar_re)], [jnp.zeros_like(abar_im)]
    for _ in range(T):
        r, i = pw_re[-1], pw_im[-1]
        pw_re.append(r * abar_re - i * abar_im)
        pw_im.append(r * abar_im + i * abar_re)
    pw_re = jnp.stack(pw_re, axis=1)[:, :, None, :]
    pw_im = jnp.stack(pw_im, axis=1)[:, :, None, :]
    cr = c_re.astype(F32)[:, None]
    ci = c_im.astype(F32)[:, None]
    ca_re = cr * pw_re - ci * pw_im
    ca_im = cr * pw_im + ci * pw_re
    cat = lambda re, im: jnp.concatenate([re.reshape(G, -1, P), im.reshape(G, -1, P)], axis=-1)
    m1 = jnp.einsum('gik,gjk->gij', cat(bb_re, bb_im), cat(ca_re[:, :T], -ca_im[:, :T]),
                    precision=lax.Precision.HIGHEST)
    m1 = m1.at[:, :, :H].add(jnp.eye(H, dtype=F32)[None] * d_skip.astype(F32)[:, None, :])
    toep = jnp.stack([jnp.pad(m1[:, :, :TH - H * t], ((0, 0), (0, 0), (H * t, 0))) for t in range(T)],
                     axis=1).reshape(G, TH, TH)
    rev_re = pw_re[:, :T][:, ::-1]
    rev_im = pw_im[:, :T][:, ::-1]
    w_re = (rev_re * bb_re[:, None] - rev_im * bb_im[:, None]).reshape(NQ, 2, TH, P)
    w_im = (rev_re * bb_im[:, None] + rev_im * bb_re[:, None]).reshape(NQ, 2, TH, P)
    zw = jnp.zeros((NQ, TH, P), F32)
    wst_p = jnp.concatenate([
        jnp.concatenate([w_re[:, 0], zw, w_im[:, 0], zw], axis=-1),
        jnp.concatenate([zw, w_re[:, 1], zw, w_im[:, 1]], axis=-1)], axis=1)
    tp = toep.reshape(NQ, 2, TH, TH)
    zt = jnp.zeros((NQ, TH, TH), F32)
    toep_p = jnp.concatenate([jnp.concatenate([tp[:, 0], zt], axis=-1),
                              jnp.concatenate([zt, tp[:, 1]], axis=-1)], axis=1)
    v_t = cat(ca_re[:, 1:], -ca_im[:, 1:]).transpose(0, 2, 1).reshape(NQ, 2, 2, P, TH)
    zv = jnp.zeros((NQ, P, TH), F32)
    vmat_p = jnp.concatenate([
        jnp.concatenate([v_t[:, 0, 0], zv], axis=-1), jnp.concatenate([zv, v_t[:, 1, 0]], axis=-1),
        jnp.concatenate([v_t[:, 0, 1], zv], axis=-1), jnp.concatenate([zv, v_t[:, 1, 1]], axis=-1)], axis=1)
    old = jnp.arange(2 * TH).reshape(2, T, H)
    perm = jax.nn.one_hot(old.transpose(1, 0, 2).reshape(-1), 2 * TH, dtype=BF16).T
    wst_p, toep_p, vmat_p = wst_p.astype(BF16), toep_p.astype(BF16), vmat_p.astype(BF16)
    wst_p = jnp.einsum('ia,qij->qaj', perm, wst_p, preferred_element_type=F32)
    toep_p = jnp.einsum('ia,qij,jb->qab', perm, toep_p, perm, preferred_element_type=F32)
    vmat_p = jnp.einsum('qij,jb->qib', vmat_p, perm, preferred_element_type=F32)
    a_re = jnp.broadcast_to(pw_re[:, T].reshape(1, G * P), (SUBLANE, G * P))
    a_im = jnp.broadcast_to(pw_im[:, T].reshape(1, G * P), (SUBLANE, G * P))
    return wst_p.astype(BF16), a_re, a_im, toep_p.astype(BF16), vmat_p.astype(BF16)


def kernel(x, p, positions, g_mix_norm, w_in, g_q_lora, w_uq, g_kv_lora, w_ukv, g_q_head, g_k_head,
           lam_re, lam_im, log_dt, b_re, b_im, c_re, c_im, d_skip, w_glu, b_glu, g_out_attn,
           g_out_ssm, w_o, g_ffn_norm, w_gate, w_up, w_down, g_ple_norm, w_ple_gate, w_ple_proj):
    batch, seq, _ = x.shape
    assert batch == SUBLANE, "the S5 chunk recurrence keeps one batch row per sublane"
    depth = w_in.shape[0]
    n_tok = batch * seq
    row = lambda g: g[None, :].astype(F32)

    inv_freq = 1.0 / (ROPE_THETA ** (jnp.arange(0, QK_ROPE_DIM, 2, dtype=F32) / QK_ROPE_DIM))
    freq = _rope_tile(jnp.concatenate([inv_freq, inv_freq]))[None, :]
    ones = jnp.ones((ROPE_HALF,), F32)
    sgn = _rope_tile(jnp.concatenate([-ones, ones]))[None, :]
    pos2d = positions.reshape(n_tok, 1)

    x2d = x.reshape(n_tok, D_MODEL)
    for i in range(depth):
        win = w_in[i].astype(BF16)
        wq = w_uq[i].reshape(Q_LORA, N_HEADS, QK_HEAD_DIM)
        wuq = jnp.concatenate([wq[..., :QK_NOPE_DIM], _rope_tile(wq[..., QK_NOPE_DIM:])], axis=-1)
        wuq = wuq.reshape(Q_LORA, N_HEADS * HEAD_PAD).astype(BF16)
        wkv = w_ukv[i].reshape(KV_LORA, N_HEADS, QK_NOPE_DIM + V_HEAD_DIM)
        wukv = jnp.concatenate([wkv[..., :QK_NOPE_DIM].reshape(KV_LORA, -1),
                                wkv[..., QK_NOPE_DIM:].reshape(KV_LORA, -1)], axis=1).astype(BF16)

        q, k, v, u_slab = _inproj(x2d, pos2d, row(g_mix_norm[i]), win, row(g_q_lora[i]), wuq,
                                  row(g_kv_lora[i]), wukv, _head_gain(g_q_head[i]),
                                  _head_gain(g_k_head[i]), freq, sgn, batch=batch, seq=seq, tl=INPROJ_TILE)
        o_attn = _attention(q, k, v, batch=batch, seq=seq, tq=ATTN_Q_TILE)

        wst, a_re, a_im, toep, vmat = _s5_params(lam_re[i], lam_im[i], log_dt[i], b_re[i], b_im[i],
                                                 c_re[i], c_im[i], d_skip[i])
        o_ssm = _s5(u_slab, wst, a_re, a_im, toep, vmat,
                    w_glu[i].astype(BF16), row(b_glu[i]), row(g_out_ssm[i]),
                    batch=batch, seq=seq, chunks=S5_CHUNKS_PER_STEP)

        wo = w_o[i].astype(BF16)
        x2d = _oproj(x2d, o_attn, o_ssm, row(g_out_attn[i]), wo[:ATTN_WIDTH], wo[ATTN_WIDTH:],
                     batch=batch, seq=seq, tl=OPROJ_TILE)
        x2d = _ffn(x2d, row(g_ffn_norm[i]), w_gate[i].astype(BF16), w_up[i].astype(BF16),
                   w_down[i].astype(BF16), tm=FFN_TOKEN_TILE, tf=FFN_FF_TILE)
        x2d = _ple(x2d, p[i].reshape(n_tok, PLE_DIM), row(g_ple_norm[i]),
                   w_ple_gate[i].astype(BF16), w_ple_proj[i].astype(BF16), tm=PLE_TILE)
    return x2d.reshape(batch, seq, D_MODEL)
```

```python
import functools

import jax
import jax.numpy as jnp
from jax import lax
from jax.experimental import pallas as pl
from jax.experimental.pallas import tpu as pltpu

D_MODEL = 2048
PLE_DIM = 256
N_HEADS = 8
QK_NOPE_DIM = 128
QK_ROPE_DIM = 64
V_HEAD_DIM = 128
QK_HEAD_DIM = QK_NOPE_DIM + QK_ROPE_DIM
Q_LORA = 512
KV_LORA = 256
ATTN_WIDTH = N_HEADS * V_HEAD_DIM
ROPE_THETA = 10000.0
SSM_WIDTH = 1024
SSM_GROUP = 16
SSM_GROUPS = SSM_WIDTH // SSM_GROUP
SSM_STATE = 64
D_FF = 5632
EPS = 1e-6

LANE = 128
SUBLANE = 8
HEAD_PAD = 2 * LANE
ROPE_HALF = QK_ROPE_DIM // 2
CHUNK = SUBLANE
N_LANE_TILES = SSM_WIDTH // LANE
PAIR_BLOCK = 2 * SSM_GROUP
PAIRS_PER_TILE = LANE // PAIR_BLOCK
PAIR_W = 2 * LANE
N_PAIRS = SSM_GROUPS // 2
VMEM_LIMIT = 56 * 1024 * 1024
LOG2_E = 1.4426950408889634
INPROJ_TILE = 512
INPROJ_SUBTILES = 2
ATTN_Q_TILE = 256
ATTN_HEADS_PER_STEP = 2
QK_AHEAD = 4
S5_CHUNKS_PER_STEP = 16
OPROJ_TILE = 1024
OPROJ_SUBTILES = 4
FFN_TOKEN_TILE = 1024
FFN_FF_TILE = 256
FFN_SUBTILES = 2
PLE_TILE = 1024
PLE_SUBTILES = 2

BF16 = jnp.bfloat16
F32 = jnp.float32


def _rms(t, g, width=None):
    n = t.shape[-1] if width is None else width
    ss = jnp.sum(t * t, axis=-1, keepdims=True) * (1.0 / n)
    return t * lax.rsqrt(ss + EPS) * g


def _dot(a, b):
    return jnp.dot(a, b, preferred_element_type=F32)


def _sigmoid(t):
    return 1.0 / (1.0 + jnp.exp(-t))


def _params(*sem):
    return pltpu.CompilerParams(dimension_semantics=sem, vmem_limit_bytes=VMEM_LIMIT)


def _inproj_kernel(x_ref, pos_ref, gmix_ref, win_ref, gql_ref, wuq_ref, gkvl_ref, wukv_ref,
                   gq_ref, gk_ref, freq_ref, sgn_ref,
                   q_ref, k_ref, v_ref, u_ref):
    sub = x_ref.shape[0] // INPROJ_SUBTILES
    o1, o2 = Q_LORA, Q_LORA + KV_LORA
    kr_tile = o2 // LANE
    gq = gq_ref[...]
    gk = gk_ref[...]
    scale = QK_HEAD_DIM ** -0.5 * LOG2_E
    lane = lax.broadcasted_iota(jnp.int32, (sub, LANE), 1)
    lo_half = lane < LANE // 2

    def project(s):
        rs = slice(s * sub, (s + 1) * sub)
        h = _rms(x_ref[rs, :], gmix_ref[...]).astype(BF16)
        z = _dot(h, win_ref[...])
        tiles = [z[:, (kr_tile + m) * LANE:(kr_tile + m + 1) * LANE] for m in range(N_LANE_TILES)]
        last = z[:, (kr_tile + N_LANE_TILES) * LANE:]
        tiles.append(jnp.concatenate([last, jnp.zeros_like(last)], axis=1))
        sw = [pltpu.roll(t, LANE // 2, 1) for t in tiles]
        cs = slice(s * sub // CHUNK, (s + 1) * sub // CHUNK)
        for kt in range(N_LANE_TILES):
            u_ref[cs, kt, :, :] = jnp.where(lo_half, sw[kt], sw[kt + 1]).reshape(sub // CHUNK, CHUNK, LANE)
        zk = z[:, kr_tile * LANE:(kr_tile + 1) * LANE]
        kr = (jnp.where(lane < ROPE_HALF, zk, 0.0)
              + jnp.where((lane >= LANE // 2) & (lane < LANE // 2 + ROPE_HALF),
                          pltpu.roll(zk, ROPE_HALF, 1), 0.0))
        c_q = _rms(z[:, :o1], gql_ref[...]).astype(BF16)
        q = _dot(c_q, wuq_ref[...])
        c_kv = _rms(z[:, o1:o2], gkvl_ref[...]).astype(BF16)
        kv = _dot(c_kv, wukv_ref[...])
        return q, kv, kr

    def finish_heads(s, q, kv, kr):
        rs = slice(s * sub, (s + 1) * sub)
        v_ref[rs, :] = kv[:, ATTN_WIDTH:].astype(BF16)
        ang = pos_ref[rs, :].astype(F32) * freq_ref[...]
        cos_t = jnp.cos(ang)
        sin_t = jnp.sin(ang) * sgn_ref[...]

        def rope(t):
            return t * cos_t + pltpu.roll(t, LANE // 2, 1) * sin_t

        kr_ss = jnp.sum(kr * kr, axis=-1, keepdims=True)
        kr_rope = rope(kr * gk[:, LANE:])
        for hd in range(N_HEADS):
            qh = q[:, hd * HEAD_PAD:(hd + 1) * HEAD_PAD]
            qn = _rms(qh, gq, width=QK_HEAD_DIM) * scale
            q_ref[rs, hd * HEAD_PAD:hd * HEAD_PAD + LANE] = qn[:, :LANE].astype(BF16)
            q_ref[rs, hd * HEAD_PAD + LANE:(hd + 1) * HEAD_PAD] = rope(qn[:, LANE:]).astype(BF16)
            kn = kv[:, hd * QK_NOPE_DIM:(hd + 1) * QK_NOPE_DIM]
            ss = (jnp.sum(kn * kn, axis=-1, keepdims=True) + kr_ss) * (1.0 / QK_HEAD_DIM)
            rinv = lax.rsqrt(ss + EPS)
            k_ref[rs, hd * HEAD_PAD:hd * HEAD_PAD + LANE] = (kn * rinv * gk[:, :LANE]).astype(BF16)
            k_ref[rs, hd * HEAD_PAD + LANE:(hd + 1) * HEAD_PAD] = (kr_rope * rinv).astype(BF16)

    for s in range(INPROJ_SUBTILES):
        finish_heads(s, *project(s))


def _inproj(x2d, pos2d, gmix, win, gql, wuq, gkvl, wukv, gq, gk, freq, sgn, *, batch, seq, tl):
    nl = seq // tl
    tok = lambda w: pl.BlockSpec((tl, w), lambda b, i: (b * nl + i, 0))
    full = lambda a: pl.BlockSpec(a.shape, lambda b, i: (0,) * a.ndim)
    n_tok = batch * seq
    return pl.pallas_call(
        _inproj_kernel,
        grid=(batch, nl),
        in_specs=[tok(D_MODEL), tok(1), full(gmix), full(win), full(gql), full(wuq), full(gkvl),
                  full(wukv), full(gq), full(gk), full(freq), full(sgn)],
        out_specs=[tok(N_HEADS * HEAD_PAD), tok(N_HEADS * HEAD_PAD), tok(ATTN_WIDTH),
                   pl.BlockSpec((tl // CHUNK, N_LANE_TILES, CHUNK, LANE), lambda b, i: (i, 0, b, 0))],
        out_shape=[jax.ShapeDtypeStruct((n_tok, N_HEADS * HEAD_PAD), BF16),
                   jax.ShapeDtypeStruct((n_tok, N_HEADS * HEAD_PAD), BF16),
                   jax.ShapeDtypeStruct((n_tok, ATTN_WIDTH), BF16),
                   jax.ShapeDtypeStruct((seq // CHUNK, N_LANE_TILES, batch * CHUNK, LANE), F32)],
        compiler_params=_params("arbitrary", "arbitrary"),
    )(x2d, pos2d, gmix, win, gql, wuq, gkvl, wukv, gq, gk, freq, sgn)


def _attn_kernel(q_ref, k_ref, v_ref, o_ref, *, tq):
    seq = q_ref.shape[0]
    n = seq // tq
    vts = [v_ref[:, hd * V_HEAD_DIM:(hd + 1) * V_HEAD_DIM].T for hd in range(ATTN_HEADS_PER_STEP)]
    diag_mask = (lax.broadcasted_iota(jnp.int32, (tq, tq), 0)
                 <= lax.broadcasted_iota(jnp.int32, (tq, tq), 1))

    def scores_t(item):
        hd, i = item
        hs = slice(hd * HEAD_PAD, (hd + 1) * HEAD_PAD)
        return lax.dot_general(k_ref[:(i + 1) * tq, hs], q_ref[i * tq:(i + 1) * tq, hs],
                               (((1,), (1,)), ((), ())), preferred_element_type=F32)

    order = [(hd, i) for i in range(n - 1, -1, -1) for hd in range(ATTN_HEADS_PER_STEP)]
    sts = {item: scores_t(item) for item in order[:QK_AHEAD]}
    for pos, item in enumerate(order):
        hd, i = item
        kv_len = (i + 1) * tq
        st = sts.pop(item)
        if pos + QK_AHEAD < len(order):
            ahead = order[pos + QK_AHEAD]
            sts[ahead] = scores_t(ahead)
        sd = jnp.where(diag_mask, st[kv_len - tq:], -jnp.inf)
        m = jnp.max(sd, axis=0, keepdims=True)
        if i:
            m = jnp.maximum(m, jnp.max(st[:kv_len - tq], axis=0, keepdims=True))
        pd = jnp.exp2(sd - m)
        l = jnp.sum(pd, axis=0, keepdims=True)
        if i:
            pt = jnp.exp2(st[:kv_len - tq] - m)
            l = l + jnp.sum(pt, axis=0, keepdims=True)
            p = jnp.concatenate([pt.astype(BF16), pd.astype(BF16)], axis=0)
        else:
            p = pd.astype(BF16)
        ot = _dot(vts[hd][:, :kv_len], p)
        o_ref[i * tq:(i + 1) * tq, hd * V_HEAD_DIM:(hd + 1) * V_HEAD_DIM] = (ot / l).T.astype(o_ref.dtype)


def _attention(q, k, v, *, batch, seq, tq):
    hps = ATTN_HEADS_PER_STEP
    return pl.pallas_call(
        functools.partial(_attn_kernel, tq=tq),
        grid=(batch, N_HEADS // hps),
        in_specs=[pl.BlockSpec((seq, hps * HEAD_PAD), lambda b, h: (b, h)),
                  pl.BlockSpec((seq, hps * HEAD_PAD), lambda b, h: (b, h)),
                  pl.BlockSpec((seq, hps * V_HEAD_DIM), lambda b, h: (b, h))],
        out_specs=pl.BlockSpec((seq, hps * V_HEAD_DIM), lambda b, h: (b, h)),
        out_shape=jax.ShapeDtypeStruct((batch * seq, ATTN_WIDTH), BF16),
        compiler_params=_params("arbitrary", "arbitrary"),
    )(q, k, v)


def _block_transpose(tiles):
    lane = lax.broadcasted_iota(jnp.int32, tiles[0].shape, 1)
    for d in (2, 1):
        hi = (lane & (PAIR_BLOCK * d)) != 0
        new = list(tiles)
        for i in range(PAIRS_PER_TILE):
            if i & d == 0:
                a, b = tiles[i], tiles[i + d]
                new[i] = jnp.where(hi, pltpu.roll(b, PAIR_BLOCK * d, 1), a)
                new[i + d] = jnp.where(hi, b, pltpu.roll(a, LANE - PAIR_BLOCK * d, 1))
        tiles = new
    return tiles


def _s5_kernel(u_ref, wst_ref, are_ref, aim_ref, toep_ref, vmat_ref, wglu_ref, bglu_ref,
               gout_ref, o_ref, state_ref, x_ref, xs_ref, ys_ref, yg_ref, *, chunks, batch):
    rows_x = chunks * batch
    rows = rows_x * CHUNK

    @pl.when(pl.program_id(0) == 0)
    def _():
        state_ref[...] = jnp.zeros_like(state_ref)

    for kt in range(N_LANE_TILES):
        tiles = [u_ref[:, pl.ds(kt, 1), pl.ds(t, batch, stride=CHUNK), :].reshape(rows_x, LANE)
                 for t in range(CHUNK)]
        for half in range(CHUNK // PAIRS_PER_TILE):
            outs = _block_transpose(tiles[half * PAIRS_PER_TILE:(half + 1) * PAIRS_PER_TILE])
            for j in range(PAIRS_PER_TILE):
                c0 = (kt * PAIRS_PER_TILE + j) * PAIR_W + half * LANE
                x_ref[:, c0:c0 + LANE] = outs[j].astype(BF16)

    for q in range(N_PAIRS):
        cs = slice(q * PAIR_W, (q + 1) * PAIR_W)
        xs_ref[:, cs] = _dot(x_ref[:, cs], wst_ref[q])

    def step(c, carry):
        r0 = pl.multiple_of(c * batch, batch)
        for q in range(N_PAIRS):
            re = slice(q * PAIR_W, q * PAIR_W + LANE)
            im = slice(q * PAIR_W + LANE, (q + 1) * PAIR_W)
            al = slice(q * LANE, (q + 1) * LANE)
            s_re = state_ref[:, re]
            s_im = state_ref[:, im]
            a_re = are_ref[:, al]
            a_im = aim_ref[:, al]
            x_re = xs_ref[pl.ds(r0, batch), re]
            x_im = xs_ref[pl.ds(r0, batch), im]
            xs_ref[pl.ds(r0, batch), re] = s_re
            xs_ref[pl.ds(r0, batch), im] = s_im
            state_ref[:, re] = a_re * s_re - a_im * s_im + x_re
            state_ref[:, im] = a_re * s_im + a_im * s_re + x_im
        return carry

    lax.fori_loop(0, chunks, step, 0, unroll=True)

    for q in range(N_PAIRS):
        cs = slice(q * PAIR_W, (q + 1) * PAIR_W)
        xs_ref[:, cs] = (_dot(x_ref[:, cs], toep_ref[q])
                         + _dot(xs_ref[:, cs].astype(BF16), vmat_ref[q]))

    for kt in range(N_LANE_TILES):
        for half in range(CHUNK // PAIRS_PER_TILE):
            tiles = [xs_ref[:, (kt * PAIRS_PER_TILE + j) * PAIR_W + half * LANE:
                            (kt * PAIRS_PER_TILE + j) * PAIR_W + (half + 1) * LANE]
                     for j in range(PAIRS_PER_TILE)]
            outs = _block_transpose(tiles)
            for t4 in range(PAIRS_PER_TILE):
                t = half * PAIRS_PER_TILE + t4
                ys_ref[:, pl.ds(kt, 1), pl.ds(t, batch, stride=CHUNK), :] = (
                    outs[t4].reshape(chunks, 1, batch, LANE))

    for kt in range(N_LANE_TILES):
        yg_ref[:, kt * LANE:(kt + 1) * LANE] = jax.nn.gelu(ys_ref[:, kt].reshape(rows, LANE), approximate=True)
    y = yg_ref[...]
    gate = _sigmoid(_dot(y.astype(BF16), wglu_ref[...]) + bglu_ref[...])
    o_ref[...] = _rms(y * gate, gout_ref[...]).reshape(chunks, batch, CHUNK, SSM_WIDTH)


def _s5(u_slab, wst, a_re, a_im, toep, vmat, wglu, bglu, gout, *, batch, seq, chunks):
    n_chunks = seq // CHUNK
    rows_x = chunks * batch
    full = lambda a: pl.BlockSpec(a.shape, lambda i: (0,) * a.ndim, pipeline_mode=pl.Buffered(1))
    return pl.pallas_call(
        functools.partial(_s5_kernel, chunks=chunks, batch=batch),
        grid=(n_chunks // chunks,),
        in_specs=[pl.BlockSpec((chunks, N_LANE_TILES, batch * CHUNK, LANE), lambda i: (i, 0, 0, 0)),
                  full(wst), full(a_re), full(a_im), full(toep), full(vmat), full(wglu),
                  full(bglu), full(gout)],
        out_specs=pl.BlockSpec((chunks, batch, CHUNK, SSM_WIDTH), lambda i: (i, 0, 0, 0)),
        out_shape=jax.ShapeDtypeStruct((n_chunks, batch, CHUNK, SSM_WIDTH), F32),
        scratch_shapes=[pltpu.VMEM((batch, N_PAIRS * PAIR_W), F32),
                        pltpu.VMEM((rows_x, N_PAIRS * PAIR_W), BF16),
                        pltpu.VMEM((rows_x, N_PAIRS * PAIR_W), F32),
                        pltpu.VMEM((chunks, N_LANE_TILES, batch * CHUNK, LANE), F32),
                        pltpu.VMEM((rows_x * CHUNK, SSM_WIDTH), F32)],
        compiler_params=_params("arbitrary"),
    )(u_slab, wst, a_re, a_im, toep, vmat, wglu, bglu, gout)


def _oproj_kernel(x_ref, oa_ref, os_ref, ga_ref, woa_ref, wos_ref, out_ref):
    sub = x_ref.shape[0] // OPROJ_SUBTILES
    for s in range(OPROJ_SUBTILES):
        rs = slice(s * sub, (s + 1) * sub)
        oa = _rms(oa_ref[rs, :].astype(F32), ga_ref[...]).astype(BF16)
        os_ = os_ref[s * sub // CHUNK:(s + 1) * sub // CHUNK].reshape(sub, SSM_WIDTH).astype(BF16)
        out_ref[rs, :] = x_ref[rs, :] + _dot(oa, woa_ref[...]) + _dot(os_, wos_ref[...])


def _oproj(x2d, o_attn, o_ssm, g_attn, wo_a, wo_s, *, batch, seq, tl):
    nl = seq // tl
    tok = lambda w: pl.BlockSpec((tl, w), lambda b, i: (b * nl + i, 0))
    full = lambda a: pl.BlockSpec(a.shape, lambda b, i: (0,) * a.ndim)
    return pl.pallas_call(
        _oproj_kernel,
        grid=(batch, nl),
        in_specs=[tok(D_MODEL), tok(ATTN_WIDTH),
                  pl.BlockSpec((tl // CHUNK, None, CHUNK, SSM_WIDTH), lambda b, i: (i, b, 0, 0)),
                  full(g_attn), full(wo_a), full(wo_s)],
        out_specs=tok(D_MODEL),
        out_shape=jax.ShapeDtypeStruct((batch * seq, D_MODEL), F32),
        compiler_params=_params("arbitrary", "arbitrary"),
    )(x2d, o_attn, o_ssm, g_attn, wo_a, wo_s)


def _ffn_kernel(x_ref, g_ref, wg_ref, wu_ref, wd_ref, out_ref, h_ref):
    j = pl.program_id(1)

    def ff_block(h):
        gate = _dot(h, wg_ref[...].astype(BF16))
        up = _dot(h, wu_ref[...].astype(BF16))
        act = (gate * _sigmoid(gate) * up).astype(BF16)
        return _dot(act, wd_ref[...].astype(BF16))

    @pl.when(j == 0)
    def _():
        sub = x_ref.shape[0] // FFN_SUBTILES
        for s in range(FFN_SUBTILES):
            rs = slice(s * sub, (s + 1) * sub)
            x = x_ref[rs, :]
            h = _rms(x, g_ref[...]).astype(BF16)
            h_ref[rs, :] = h
            out_ref[rs, :] = x + ff_block(h)

    @pl.when(j > 0)
    def _():
        out_ref[...] += ff_block(h_ref[...])


def _ffn(x2d, g, wg, wu, wd, *, layer, tm, tf):
    n_tok = x2d.shape[0]
    return pl.pallas_call(
        _ffn_kernel,
        grid=(n_tok // tm, D_FF // tf),
        in_specs=[pl.BlockSpec((tm, D_MODEL), lambda i, j: (i, 0)),
                  pl.BlockSpec((1, D_MODEL), lambda i, j: (0, 0)),
                  pl.BlockSpec((None, D_MODEL, tf), lambda i, j: (layer, 0, j)),
                  pl.BlockSpec((None, D_MODEL, tf), lambda i, j: (layer, 0, j)),
                  pl.BlockSpec((None, tf, D_MODEL), lambda i, j: (layer, j, 0))],
        out_specs=pl.BlockSpec((tm, D_MODEL), lambda i, j: (i, 0)),
        out_shape=jax.ShapeDtypeStruct((n_tok, D_MODEL), F32),
        scratch_shapes=[pltpu.VMEM((tm, D_MODEL), BF16)],
        compiler_params=_params("arbitrary", "arbitrary"),
    )(x2d, g, wg, wu, wd)


def _ple_kernel(x_ref, p_ref, g_ref, wpg_ref, wpp_ref, out_ref):
    sub = x_ref.shape[0] // PLE_SUBTILES
    for s in range(PLE_SUBTILES):
        rs = slice(s * sub, (s + 1) * sub)
        x = x_ref[rs, :]
        h = _rms(x, g_ref[...]).astype(BF16)
        gate = _sigmoid(_dot(h, wpg_ref[...]))
        out_ref[rs, :] = x + gate * _dot(p_ref[rs, :].astype(BF16), wpp_ref[...])


def _ple(x2d, p2d, g, wpg, wpp, *, tm):
    n_tok = x2d.shape[0]
    full = lambda a: pl.BlockSpec(a.shape, lambda i: (0,) * a.ndim)
    return pl.pallas_call(
        _ple_kernel,
        grid=(n_tok // tm,),
        in_specs=[pl.BlockSpec((tm, D_MODEL), lambda i: (i, 0)),
                  pl.BlockSpec((tm, PLE_DIM), lambda i: (i, 0)),
                  full(g), full(wpg), full(wpp)],
        out_specs=pl.BlockSpec((tm, D_MODEL), lambda i: (i, 0)),
        out_shape=jax.ShapeDtypeStruct((n_tok, D_MODEL), F32),
        compiler_params=_params("arbitrary"),
    )(x2d, p2d, g, wpg, wpp)


def _rope_tile(t):
    z = jnp.zeros(t.shape[:-1] + (ROPE_HALF,), t.dtype)
    return jnp.concatenate([t[..., :ROPE_HALF], z, t[..., ROPE_HALF:], z], axis=-1)


def _head_gain(g):
    return jnp.concatenate([g[:QK_NOPE_DIM], _rope_tile(g[QK_NOPE_DIM:])])[None, :].astype(F32)


def _s5_params(lam_re, lam_im, log_dt, b_re, b_im, c_re, c_im, d_skip):
    G, P, H, T, NQ = SSM_GROUPS, SSM_STATE, SSM_GROUP, CHUNK, N_PAIRS
    TH = T * H
    lr = jnp.minimum(lam_re.astype(F32), -1e-4)
    li = lam_im.astype(F32)
    dt = jnp.exp(log_dt.astype(F32))[:, None]
    mag = jnp.exp(lr * dt)
    abar_re = mag * jnp.cos(li * dt)
    abar_im = mag * jnp.sin(li * dt)
    den = lr * lr + li * li
    num_re = abar_re - 1.0
    num_im = abar_im
    coef_re = ((num_re * lr + num_im * li) / den)[:, None, :]
    coef_im = ((num_im * lr - num_re * li) / den)[:, None, :]
    br = b_re.astype(F32).transpose(0, 2, 1)
    bim = b_im.astype(F32).transpose(0, 2, 1)
    bb_re = coef_re * br - coef_im * bim
    bb_im = coef_re * bim + coef_im * br
    pw_re, pw_im = [jnp.ones_like(abar_re)], [jnp.zeros_like(abar_im)]
    for _ in range(T):
        r, i = pw_re[-1], pw_im[-1]
        pw_re.append(r * abar_re - i * abar_im)
        pw_im.append(r * abar_im + i * abar_re)
    pw_re = jnp.stack(pw_re, axis=1)[:, :, None, :]
    pw_im = jnp.stack(pw_im, axis=1)[:, :, None, :]
    cr = c_re.astype(F32)[:, None]
    ci = c_im.astype(F32)[:, None]
    ca_re = cr * pw_re - ci * pw_im
    ca_im = cr * pw_im + ci * pw_re
    cat = lambda re, im: jnp.concatenate([re.reshape(G, -1, P), im.reshape(G, -1, P)], axis=-1)
    m1 = jnp.einsum('gik,gjk->gij', cat(bb_re, bb_im), cat(ca_re[:, :T], -ca_im[:, :T]),
                    precision=lax.Precision.HIGHEST)
    m1 = m1.at[:, :, :H].add(jnp.eye(H, dtype=F32)[None] * d_skip.astype(F32)[:, None, :])
    toep = jnp.stack([jnp.pad(m1[:, :, :TH - H * t], ((0, 0), (0, 0), (H * t, 0))) for t in range(T)],
                     axis=1).reshape(G, TH, TH)
    rev_re = pw_re[:, :T][:, ::-1]
    rev_im = pw_im[:, :T][:, ::-1]
    w_re = (rev_re * bb_re[:, None] - rev_im * bb_im[:, None]).reshape(NQ, 2, TH, P)
    w_im = (rev_re * bb_im[:, None] + rev_im * bb_re[:, None]).reshape(NQ, 2, TH, P)
    zw = jnp.zeros((NQ, TH, P), F32)
    wst_p = jnp.concatenate([
        jnp.concatenate([w_re[:, 0], zw, w_im[:, 0], zw], axis=-1),
        jnp.concatenate([zw, w_re[:, 1], zw, w_im[:, 1]], axis=-1)], axis=1)
    tp = toep.reshape(NQ, 2, TH, TH)
    zt = jnp.zeros((NQ, TH, TH), F32)
    toep_p = jnp.concatenate([jnp.concatenate([tp[:, 0], zt], axis=-1),
                              jnp.concatenate([zt, tp[:, 1]], axis=-1)], axis=1)
    v_t = cat(ca_re[:, 1:], -ca_im[:, 1:]).transpose(0, 2, 1).reshape(NQ, 2, 2, P, TH)
    zv = jnp.zeros((NQ, P, TH), F32)
    vmat_p = jnp.concatenate([
        jnp.concatenate([v_t[:, 0, 0], zv], axis=-1), jnp.concatenate([zv, v_t[:, 1, 0]], axis=-1),
        jnp.concatenate([v_t[:, 0, 1], zv], axis=-1), jnp.concatenate([zv, v_t[:, 1, 1]], axis=-1)], axis=1)
    old = jnp.arange(2 * TH).reshape(2, T, H)
    perm = jax.nn.one_hot(old.transpose(1, 0, 2).reshape(-1), 2 * TH, dtype=BF16).T
    wst_p, toep_p, vmat_p = wst_p.astype(BF16), toep_p.astype(BF16), vmat_p.astype(BF16)
    wst_p = jnp.einsum('ia,qij->qaj', perm, wst_p, preferred_element_type=F32)
    toep_p = jnp.einsum('ia,qij,jb->qab', perm, toep_p, perm, preferred_element_type=F32)
    vmat_p = jnp.einsum('qij,jb->qib', vmat_p, perm, preferred_element_type=F32)
    a_re = jnp.broadcast_to(pw_re[:, T].reshape(1, G * P), (SUBLANE, G * P))
    a_im = jnp.broadcast_to(pw_im[:, T].reshape(1, G * P), (SUBLANE, G * P))
    return wst_p.astype(BF16), a_re, a_im, toep_p.astype(BF16), vmat_p.astype(BF16)


def kernel(x, p, positions, g_mix_norm, w_in, g_q_lora, w_uq, g_kv_lora, w_ukv, g_q_head, g_k_head,
           lam_re, lam_im, log_dt, b_re, b_im, c_re, c_im, d_skip, w_glu, b_glu, g_out_attn,
           g_out_ssm, w_o, g_ffn_norm, w_gate, w_up, w_down, g_ple_norm, w_ple_gate, w_ple_proj):
    batch, seq, _ = x.shape
    assert batch == SUBLANE, "the S5 chunk recurrence keeps one batch row per sublane"
    depth = w_in.shape[0]
    n_tok = batch * seq
    row = lambda g: g[None, :].astype(F32)

    inv_freq = 1.0 / (ROPE_THETA ** (jnp.arange(0, QK_ROPE_DIM, 2, dtype=F32) / QK_ROPE_DIM))
    freq = _rope_tile(jnp.concatenate([inv_freq, inv_freq]))[None, :]
    ones = jnp.ones((ROPE_HALF,), F32)
    sgn = _rope_tile(jnp.concatenate([-ones, ones]))[None, :]
    pos2d = positions.reshape(n_tok, 1)

    x2d = x.reshape(n_tok, D_MODEL)
    for i in range(depth):
        win = w_in[i].astype(BF16)
        wq = w_uq[i].reshape(Q_LORA, N_HEADS, QK_HEAD_DIM)
        wuq = jnp.concatenate([wq[..., :QK_NOPE_DIM], _rope_tile(wq[..., QK_NOPE_DIM:])], axis=-1)
        wuq = wuq.reshape(Q_LORA, N_HEADS * HEAD_PAD).astype(BF16)
        wkv = w_ukv[i].reshape(KV_LORA, N_HEADS, QK_NOPE_DIM + V_HEAD_DIM)
        wukv = jnp.concatenate([wkv[..., :QK_NOPE_DIM].reshape(KV_LORA, -1),
                                wkv[..., QK_NOPE_DIM:].reshape(KV_LORA, -1)], axis=1).astype(BF16)

        q, k, v, u_slab = _inproj(x2d, pos2d, row(g_mix_norm[i]), win, row(g_q_lora[i]), wuq,
                                  row(g_kv_lora[i]), wukv, _head_gain(g_q_head[i]),
                                  _head_gain(g_k_head[i]), freq, sgn, batch=batch, seq=seq, tl=INPROJ_TILE)
        o_attn = _attention(q, k, v, batch=batch, seq=seq, tq=ATTN_Q_TILE)

        wst, a_re, a_im, toep, vmat = _s5_params(lam_re[i], lam_im[i], log_dt[i], b_re[i], b_im[i],
                                                 c_re[i], c_im[i], d_skip[i])
        o_ssm = _s5(u_slab, wst, a_re, a_im, toep, vmat,
                    w_glu[i].astype(BF16), row(b_glu[i]), row(g_out_ssm[i]),
                    batch=batch, seq=seq, chunks=S5_CHUNKS_PER_STEP)

        wo = w_o[i].astype(BF16)
        x2d = _oproj(x2d, o_attn, o_ssm, row(g_out_attn[i]), wo[:ATTN_WIDTH], wo[ATTN_WIDTH:],
                     batch=batch, seq=seq, tl=OPROJ_TILE)
        x2d = _ffn(x2d, row(g_ffn_norm[i]), w_gate, w_up, w_down,
                   layer=i, tm=FFN_TOKEN_TILE, tf=FFN_FF_TILE)
        x2d = _ple(x2d, p[i].reshape(n_tok, PLE_DIM), row(g_ple_norm[i]),
                   w_ple_gate[i].astype(BF16), w_ple_proj[i].astype(BF16), tm=PLE_TILE)
    return x2d.reshape(batch, seq, D_MODEL)
```

```python
import functools

import jax
import jax.numpy as jnp
from jax import lax
from jax.experimental import pallas as pl
from jax.experimental.pallas import tpu as pltpu

D_MODEL = 2048
PLE_DIM = 256
N_HEADS = 8
QK_NOPE_DIM = 128
QK_ROPE_DIM = 64
V_HEAD_DIM = 128
QK_HEAD_DIM = QK_NOPE_DIM + QK_ROPE_DIM
Q_LORA = 512
KV_LORA = 256
ATTN_WIDTH = N_HEADS * V_HEAD_DIM
ROPE_THETA = 10000.0
SSM_WIDTH = 1024
SSM_GROUP = 16
SSM_GROUPS = SSM_WIDTH // SSM_GROUP
SSM_STATE = 64
D_FF = 5632
EPS = 1e-6

LANE = 128
SUBLANE = 8
HEAD_PAD = 2 * LANE
ROPE_HALF = QK_ROPE_DIM // 2
CHUNK = SUBLANE
N_LANE_TILES = SSM_WIDTH // LANE
PAIR_BLOCK = 2 * SSM_GROUP
PAIRS_PER_TILE = LANE // PAIR_BLOCK
PAIR_W = 2 * LANE
N_PAIRS = SSM_GROUPS // 2
VMEM_LIMIT = 56 * 1024 * 1024
LOG2_E = 1.4426950408889634
INPROJ_TILE = 512
INPROJ_SUBTILES = 2
ATTN_Q_TILE = 256
ATTN_HEADS_PER_STEP = 2
QK_AHEAD = 4
S5_CHUNKS_PER_STEP = 16
OPROJ_TILE = 1024
OPROJ_SUBTILES = 4
FFN_TOKEN_TILE = 1024
FFN_FF_TILE = 512
FFN_SUBTILES = 2
PLE_TILE = 1024
PLE_SUBTILES = 2

BF16 = jnp.bfloat16
F32 = jnp.float32


def _rms(t, g, width=None):
    n = t.shape[-1] if width is None else width
    ss = jnp.sum(t * t, axis=-1, keepdims=True) * (1.0 / n)
    return t * lax.rsqrt(ss + EPS) * g


def _dot(a, b):
    return jnp.dot(a, b, preferred_element_type=F32)


def _sigmoid(t):
    return 0.5 * jnp.tanh(0.5 * t) + 0.5


def _params(*sem):
    return pltpu.CompilerParams(dimension_semantics=sem, vmem_limit_bytes=VMEM_LIMIT)


def _inproj_kernel(x_ref, pos_ref, gmix_ref, win_ref, gql_ref, wuq_ref, gkvl_ref, wukv_ref,
                   gq_ref, gk_ref, freq_ref, sgn_ref,
                   q_ref, k_ref, v_ref, u_ref):
    sub = x_ref.shape[0] // INPROJ_SUBTILES
    o1, o2 = Q_LORA, Q_LORA + KV_LORA
    kr_tile = o2 // LANE
    gq = gq_ref[...]
    gk = gk_ref[...]
    scale = QK_HEAD_DIM ** -0.5 * LOG2_E
    lane = lax.broadcasted_iota(jnp.int32, (sub, LANE), 1)
    lo_half = lane < LANE // 2

    def project(s):
        rs = slice(s * sub, (s + 1) * sub)
        h = _rms(x_ref[rs, :], gmix_ref[...]).astype(BF16)
        z = _dot(h, win_ref[...])
        tiles = [z[:, (kr_tile + m) * LANE:(kr_tile + m + 1) * LANE] for m in range(N_LANE_TILES)]
        last = z[:, (kr_tile + N_LANE_TILES) * LANE:]
        tiles.append(jnp.concatenate([last, jnp.zeros_like(last)], axis=1))
        sw = [pltpu.roll(t, LANE // 2, 1) for t in tiles]
        cs = slice(s * sub // CHUNK, (s + 1) * sub // CHUNK)
        for kt in range(N_LANE_TILES):
            u_ref[cs, kt, :, :] = jnp.where(lo_half, sw[kt], sw[kt + 1]).reshape(sub // CHUNK, CHUNK, LANE)
        zk = z[:, kr_tile * LANE:(kr_tile + 1) * LANE]
        kr = (jnp.where(lane < ROPE_HALF, zk, 0.0)
              + jnp.where((lane >= LANE // 2) & (lane < LANE // 2 + ROPE_HALF),
                          pltpu.roll(zk, ROPE_HALF, 1), 0.0))
        c_q = _rms(z[:, :o1], gql_ref[...]).astype(BF16)
        q = _dot(c_q, wuq_ref[...])
        c_kv = _rms(z[:, o1:o2], gkvl_ref[...]).astype(BF16)
        kv = _dot(c_kv, wukv_ref[...])
        return q, kv, kr

    def finish_heads(s, q, kv, kr):
        rs = slice(s * sub, (s + 1) * sub)
        v_ref[rs, :] = kv[:, ATTN_WIDTH:].astype(BF16)
        ang = pos_ref[rs, :].astype(F32) * freq_ref[...]
        cos_t = jnp.cos(ang)
        sin_t = jnp.sin(ang) * sgn_ref[...]

        def rope(t):
            return t * cos_t + pltpu.roll(t, LANE // 2, 1) * sin_t

        kr_ss = jnp.sum(kr * kr, axis=-1, keepdims=True)
        kr_rope = rope(kr * gk[:, LANE:])
        for hd in range(N_HEADS):
            qh = q[:, hd * HEAD_PAD:(hd + 1) * HEAD_PAD]
            qn = _rms(qh, gq, width=QK_HEAD_DIM) * scale
            q_ref[rs, hd * HEAD_PAD:hd * HEAD_PAD + LANE] = qn[:, :LANE].astype(BF16)
            q_ref[rs, hd * HEAD_PAD + LANE:(hd + 1) * HEAD_PAD] = rope(qn[:, LANE:]).astype(BF16)
            kn = kv[:, hd * QK_NOPE_DIM:(hd + 1) * QK_NOPE_DIM]
            ss = (jnp.sum(kn * kn, axis=-1, keepdims=True) + kr_ss) * (1.0 / QK_HEAD_DIM)
            rinv = lax.rsqrt(ss + EPS)
            k_ref[rs, hd * HEAD_PAD:hd * HEAD_PAD + LANE] = (kn * rinv * gk[:, :LANE]).astype(BF16)
            k_ref[rs, hd * HEAD_PAD + LANE:(hd + 1) * HEAD_PAD] = (kr_rope * rinv).astype(BF16)

    for s in range(INPROJ_SUBTILES):
        finish_heads(s, *project(s))


def _inproj(x2d, pos2d, gmix, win, gql, wuq, gkvl, wukv, gq, gk, freq, sgn, *, batch, seq, tl):
    nl = seq // tl
    tok = lambda w: pl.BlockSpec((tl, w), lambda b, i: (b * nl + i, 0))
    full = lambda a: pl.BlockSpec(a.shape, lambda b, i: (0,) * a.ndim)
    n_tok = batch * seq
    return pl.pallas_call(
        _inproj_kernel,
        grid=(batch, nl),
        in_specs=[tok(D_MODEL), tok(1), full(gmix), full(win), full(gql), full(wuq), full(gkvl),
                  full(wukv), full(gq), full(gk), full(freq), full(sgn)],
        out_specs=[tok(N_HEADS * HEAD_PAD), tok(N_HEADS * HEAD_PAD), tok(ATTN_WIDTH),
                   pl.BlockSpec((tl // CHUNK, N_LANE_TILES, CHUNK, LANE), lambda b, i: (i, 0, b, 0))],
        out_shape=[jax.ShapeDtypeStruct((n_tok, N_HEADS * HEAD_PAD), BF16),
                   jax.ShapeDtypeStruct((n_tok, N_HEADS * HEAD_PAD), BF16),
                   jax.ShapeDtypeStruct((n_tok, ATTN_WIDTH), BF16),
                   jax.ShapeDtypeStruct((seq // CHUNK, N_LANE_TILES, batch * CHUNK, LANE), F32)],
        compiler_params=_params("arbitrary", "arbitrary"),
    )(x2d, pos2d, gmix, win, gql, wuq, gkvl, wukv, gq, gk, freq, sgn)


def _attn_kernel(q_ref, k_ref, v_ref, o_ref, *, tq):
    seq = q_ref.shape[0]
    n = seq // tq
    vts = [v_ref[:, hd * V_HEAD_DIM:(hd + 1) * V_HEAD_DIM].T for hd in range(ATTN_HEADS_PER_STEP)]
    diag_mask = (lax.broadcasted_iota(jnp.int32, (tq, tq), 0)
                 <= lax.broadcasted_iota(jnp.int32, (tq, tq), 1))

    def scores_t(item):
        hd, i = item
        hs = slice(hd * HEAD_PAD, (hd + 1) * HEAD_PAD)
        return lax.dot_general(k_ref[:(i + 1) * tq, hs], q_ref[i * tq:(i + 1) * tq, hs],
                               (((1,), (1,)), ((), ())), preferred_element_type=F32)

    order = [(hd, i) for i in range(n - 1, -1, -1) for hd in range(ATTN_HEADS_PER_STEP)]
    sts = {item: scores_t(item) for item in order[:QK_AHEAD]}
    for pos, item in enumerate(order):
        hd, i = item
        kv_len = (i + 1) * tq
        st = sts.pop(item)
        if pos + QK_AHEAD < len(order):
            ahead = order[pos + QK_AHEAD]
            sts[ahead] = scores_t(ahead)
        sd = jnp.where(diag_mask, st[kv_len - tq:], -jnp.inf)
        m = jnp.max(sd, axis=0, keepdims=True)
        if i:
            m = jnp.maximum(m, jnp.max(st[:kv_len - tq], axis=0, keepdims=True))
        pd = jnp.exp2(sd - m)
        l = jnp.sum(pd, axis=0, keepdims=True)
        if i:
            pt = jnp.exp2(st[:kv_len - tq] - m)
            l = l + jnp.sum(pt, axis=0, keepdims=True)
            p = jnp.concatenate([pt.astype(BF16), pd.astype(BF16)], axis=0)
        else:
            p = pd.astype(BF16)
        ot = _dot(vts[hd][:, :kv_len], p)
        o_ref[i * tq:(i + 1) * tq, hd * V_HEAD_DIM:(hd + 1) * V_HEAD_DIM] = (ot / l).T.astype(o_ref.dtype)


def _attention(q, k, v, *, batch, seq, tq):
    hps = ATTN_HEADS_PER_STEP
    return pl.pallas_call(
        functools.partial(_attn_kernel, tq=tq),
        grid=(batch, N_HEADS // hps),
        in_specs=[pl.BlockSpec((seq, hps * HEAD_PAD), lambda b, h: (b, h)),
                  pl.BlockSpec((seq, hps * HEAD_PAD), lambda b, h: (b, h)),
                  pl.BlockSpec((seq, hps * V_HEAD_DIM), lambda b, h: (b, h))],
        out_specs=pl.BlockSpec((seq, hps * V_HEAD_DIM), lambda b, h: (b, h)),
        out_shape=jax.ShapeDtypeStruct((batch * seq, ATTN_WIDTH), BF16),
        compiler_params=_params("arbitrary", "arbitrary"),
    )(q, k, v)


def _block_transpose(tiles):
    lane = lax.broadcasted_iota(jnp.int32, tiles[0].shape, 1)
    for d in (2, 1):
        hi = (lane & (PAIR_BLOCK * d)) != 0
        new = list(tiles)
        for i in range(PAIRS_PER_TILE):
            if i & d == 0:
                a, b = tiles[i], tiles[i + d]
                new[i] = jnp.where(hi, pltpu.roll(b, PAIR_BLOCK * d, 1), a)
                new[i + d] = jnp.where(hi, b, pltpu.roll(a, LANE - PAIR_BLOCK * d, 1))
        tiles = new
    return tiles


def _s5_kernel(u_ref, wst_ref, are_ref, aim_ref, toep_ref, vmat_ref, wglu_ref, bglu_ref,
               gout_ref, o_ref, state_ref, x_ref, xs_ref, ys_ref, yg_ref, *, chunks, batch):
    rows_x = chunks * batch
    rows = rows_x * CHUNK

    @pl.when(pl.program_id(0) == 0)
    def _():
        state_ref[...] = jnp.zeros_like(state_ref)

    for kt in range(N_LANE_TILES):
        tiles = [u_ref[:, pl.ds(kt, 1), pl.ds(t, batch, stride=CHUNK), :].reshape(rows_x, LANE)
                 for t in range(CHUNK)]
        for half in range(CHUNK // PAIRS_PER_TILE):
            outs = _block_transpose(tiles[half * PAIRS_PER_TILE:(half + 1) * PAIRS_PER_TILE])
            for j in range(PAIRS_PER_TILE):
                c0 = (kt * PAIRS_PER_TILE + j) * PAIR_W + half * LANE
                x_ref[:, c0:c0 + LANE] = outs[j].astype(BF16)

    for q in range(N_PAIRS):
        cs = slice(q * PAIR_W, (q + 1) * PAIR_W)
        xs_ref[:, cs] = _dot(x_ref[:, cs], wst_ref[q])

    def step(c, carry):
        r0 = pl.multiple_of(c * batch, batch)
        for q in range(N_PAIRS):
            re = slice(q * PAIR_W, q * PAIR_W + LANE)
            im = slice(q * PAIR_W + LANE, (q + 1) * PAIR_W)
            al = slice(q * LANE, (q + 1) * LANE)
            s_re = state_ref[:, re]
            s_im = state_ref[:, im]
            a_re = are_ref[:, al]
            a_im = aim_ref[:, al]
            x_re = xs_ref[pl.ds(r0, batch), re]
            x_im = xs_ref[pl.ds(r0, batch), im]
            xs_ref[pl.ds(r0, batch), re] = s_re
            xs_ref[pl.ds(r0, batch), im] = s_im
            state_ref[:, re] = a_re * s_re - a_im * s_im + x_re
            state_ref[:, im] = a_re * s_im + a_im * s_re + x_im
        return carry

    lax.fori_loop(0, chunks, step, 0, unroll=True)

    for q in range(N_PAIRS):
        cs = slice(q * PAIR_W, (q + 1) * PAIR_W)
        xs_ref[:, cs] = (_dot(x_ref[:, cs], toep_ref[q])
                         + _dot(xs_ref[:, cs].astype(BF16), vmat_ref[q]))

    for kt in range(N_LANE_TILES):
        for half in range(CHUNK // PAIRS_PER_TILE):
            tiles = [xs_ref[:, (kt * PAIRS_PER_TILE + j) * PAIR_W + half * LANE:
                            (kt * PAIRS_PER_TILE + j) * PAIR_W + (half + 1) * LANE]
                     for j in range(PAIRS_PER_TILE)]
            outs = _block_transpose(tiles)
            for t4 in range(PAIRS_PER_TILE):
                t = half * PAIRS_PER_TILE + t4
                ys_ref[:, pl.ds(kt, 1), pl.ds(t, batch, stride=CHUNK), :] = (
                    outs[t4].reshape(chunks, 1, batch, LANE))

    for kt in range(N_LANE_TILES):
        yg_ref[:, kt * LANE:(kt + 1) * LANE] = jax.nn.gelu(ys_ref[:, kt].reshape(rows, LANE), approximate=True)
    y = yg_ref[...]
    gate = _sigmoid(_dot(y.astype(BF16), wglu_ref[...]) + bglu_ref[...])
    o_ref[...] = _rms(y * gate, gout_ref[...]).reshape(chunks, batch, CHUNK, SSM_WIDTH)


def _s5(u_slab, wst, a_re, a_im, toep, vmat, wglu, bglu, gout, *, batch, seq, chunks):
    n_chunks = seq // CHUNK
    rows_x = chunks * batch
    full = lambda a: pl.BlockSpec(a.shape, lambda i: (0,) * a.ndim, pipeline_mode=pl.Buffered(1))
    return pl.pallas_call(
        functools.partial(_s5_kernel, chunks=chunks, batch=batch),
        grid=(n_chunks // chunks,),
        in_specs=[pl.BlockSpec((chunks, N_LANE_TILES, batch * CHUNK, LANE), lambda i: (i, 0, 0, 0)),
                  full(wst), full(a_re), full(a_im), full(toep), full(vmat), full(wglu),
                  full(bglu), full(gout)],
        out_specs=pl.BlockSpec((chunks, batch, CHUNK, SSM_WIDTH), lambda i: (i, 0, 0, 0)),
        out_shape=jax.ShapeDtypeStruct((n_chunks, batch, CHUNK, SSM_WIDTH), F32),
        scratch_shapes=[pltpu.VMEM((batch, N_PAIRS * PAIR_W), F32),
                        pltpu.VMEM((rows_x, N_PAIRS * PAIR_W), BF16),
                        pltpu.VMEM((rows_x, N_PAIRS * PAIR_W), F32),
                        pltpu.VMEM((chunks, N_LANE_TILES, batch * CHUNK, LANE), F32),
                        pltpu.VMEM((rows_x * CHUNK, SSM_WIDTH), F32)],
        compiler_params=_params("arbitrary"),
    )(u_slab, wst, a_re, a_im, toep, vmat, wglu, bglu, gout)


def _oproj_kernel(x_ref, oa_ref, os_ref, ga_ref, woa_ref, wos_ref, out_ref):
    sub = x_ref.shape[0] // OPROJ_SUBTILES
    for s in range(OPROJ_SUBTILES):
        rs = slice(s * sub, (s + 1) * sub)
        oa = _rms(oa_ref[rs, :].astype(F32), ga_ref[...]).astype(BF16)
        os_ = os_ref[s * sub // CHUNK:(s + 1) * sub // CHUNK].reshape(sub, SSM_WIDTH).astype(BF16)
        out_ref[rs, :] = x_ref[rs, :] + _dot(oa, woa_ref[...]) + _dot(os_, wos_ref[...])


def _oproj(x2d, o_attn, o_ssm, g_attn, wo_a, wo_s, *, batch, seq, tl):
    nl = seq // tl
    tok = lambda w: pl.BlockSpec((tl, w), lambda b, i: (b * nl + i, 0))
    full = lambda a: pl.BlockSpec(a.shape, lambda b, i: (0,) * a.ndim)
    return pl.pallas_call(
        _oproj_kernel,
        grid=(batch, nl),
        in_specs=[tok(D_MODEL), tok(ATTN_WIDTH),
                  pl.BlockSpec((tl // CHUNK, None, CHUNK, SSM_WIDTH), lambda b, i: (i, b, 0, 0)),
                  full(g_attn), full(wo_a), full(wo_s)],
        out_specs=tok(D_MODEL),
        out_shape=jax.ShapeDtypeStruct((batch * seq, D_MODEL), F32),
        compiler_params=_params("arbitrary", "arbitrary"),
    )(x2d, o_attn, o_ssm, g_attn, wo_a, wo_s)


def _ffn_kernel(x_ref, g_ref, wg_ref, wu_ref, wd_ref, out_ref, h_ref):
    j = pl.program_id(1)

    def ff_block(h):
        gate = _dot(h, wg_ref[...])
        up = _dot(h, wu_ref[...])
        act = (gate * _sigmoid(gate) * up).astype(BF16)
        return _dot(act, wd_ref[...])

    @pl.when(j == 0)
    def _():
        sub = x_ref.shape[0] // FFN_SUBTILES
        for s in range(FFN_SUBTILES):
            rs = slice(s * sub, (s + 1) * sub)
            x = x_ref[rs, :]
            h = _rms(x, g_ref[...]).astype(BF16)
            h_ref[rs, :] = h
            out_ref[rs, :] = x + ff_block(h)

    @pl.when(j > 0)
    def _():
        out_ref[...] += ff_block(h_ref[...])


def _ffn(x2d, g, wg, wu, wd, *, tm, tf):
    n_tok = x2d.shape[0]
    return pl.pallas_call(
        _ffn_kernel,
        grid=(n_tok // tm, D_FF // tf),
        in_specs=[pl.BlockSpec((tm, D_MODEL), lambda i, j: (i, 0)),
                  pl.BlockSpec((1, D_MODEL), lambda i, j: (0, 0)),
                  pl.BlockSpec((D_MODEL, tf), lambda i, j: (0, j)),
                  pl.BlockSpec((D_MODEL, tf), lambda i, j: (0, j)),
                  pl.BlockSpec((tf, D_MODEL), lambda i, j: (j, 0))],
        out_specs=pl.BlockSpec((tm, D_MODEL), lambda i, j: (i, 0)),
        out_shape=jax.ShapeDtypeStruct((n_tok, D_MODEL), F32),
        scratch_shapes=[pltpu.VMEM((tm, D_MODEL), BF16)],
        compiler_params=_params("arbitrary", "arbitrary"),
    )(x2d, g, wg, wu, wd)


def _ple_kernel(x_ref, p_ref, g_ref, wpg_ref, wpp_ref, out_ref):
    sub = x_ref.shape[0] // PLE_SUBTILES
    for s in range(PLE_SUBTILES):
        rs = slice(s * sub, (s + 1) * sub)
        x = x_ref[rs, :]
        h = _rms(x, g_ref[...]).astype(BF16)
        gate = _sigmoid(_dot(h, wpg_ref[...]))
        out_ref[rs, :] = x + gate * _dot(p_ref[rs, :].astype(BF16), wpp_ref[...])


def _ple(x2d, p2d, g, wpg, wpp, *, tm):
    n_tok = x2d.shape[0]
    full = lambda a: pl.BlockSpec(a.shape, lambda i: (0,) * a.ndim)
    return pl.pallas_call(
        _ple_kernel,
        grid=(n_tok // tm,),
        in_specs=[pl.BlockSpec((tm, D_MODEL), lambda i: (i, 0)),
                  pl.BlockSpec((tm, PLE_DIM), lambda i: (i, 0)),
                  full(g), full(wpg), full(wpp)],
        out_specs=pl.BlockSpec((tm, D_MODEL), lambda i: (i, 0)),
        out_shape=jax.ShapeDtypeStruct((n_tok, D_MODEL), F32),
        compiler_params=_params("arbitrary"),
    )(x2d, p2d, g, wpg, wpp)


def _rope_tile(t):
    z = jnp.zeros(t.shape[:-1] + (ROPE_HALF,), t.dtype)
    return jnp.concatenate([t[..., :ROPE_HALF], z, t[..., ROPE_HALF:], z], axis=-1)


def _head_gain(g):
    return jnp.concatenate([g[:QK_NOPE_DIM], _rope_tile(g[QK_NOPE_DIM:])])[None, :].astype(F32)


def _s5_params(lam_re, lam_im, log_dt, b_re, b_im, c_re, c_im, d_skip):
    G, P, H, T, NQ = SSM_GROUPS, SSM_STATE, SSM_GROUP, CHUNK, N_PAIRS
    TH = T * H
    lr = jnp.minimum(lam_re.astype(F32), -1e-4)
    li = lam_im.astype(F32)
    dt = jnp.exp(log_dt.astype(F32))[:, None]
    mag = jnp.exp(lr * dt)
    abar_re = mag * jnp.cos(li * dt)
    abar_im = mag * jnp.sin(li * dt)
    den = lr * lr + li * li
    num_re = abar_re - 1.0
    num_im = abar_im
    coef_re = ((num_re * lr + num_im * li) / den)[:, None, :]
    coef_im = ((num_im * lr - num_re * li) / den)[:, None, :]
    br = b_re.astype(F32).transpose(0, 2, 1)
    bim = b_im.astype(F32).transpose(0, 2, 1)
    bb_re = coef_re * br - coef_im * bim
    bb_im = coef_re * bim + coef_im * br
    pw_re, pw_im = [jnp.ones_like(abar_re)], [jnp.zeros_like(abar_im)]
    for _ in range(T):
        r, i = pw_re[-1], pw_im[-1]
        pw_re.append(r * abar_re - i * abar_im)
        pw_im.append(r * abar_im + i * abar_re)
    pw_re = jnp.stack(pw_re, axis=1)[:, :, None, :]
    pw_im = jnp.stack(pw_im, axis=1)[:, :, None, :]
    cr = c_re.astype(F32)[:, None]
    ci = c_im.astype(F32)[:, None]
    ca_re = cr * pw_re - ci * pw_im
    ca_im = cr * pw_im + ci * pw_re
    cat = lambda re, im: jnp.concatenate([re.reshape(G, -1, P), im.reshape(G, -1, P)], axis=-1)
    m1 = jnp.einsum('gik,gjk->gij', cat(bb_re, bb_im), cat(ca_re[:, :T], -ca_im[:, :T]),
                    precision=lax.Precision.HIGHEST)
    m1 = m1.at[:, :, :H].add(jnp.eye(H, dtype=F32)[None] * d_skip.astype(F32)[:, None, :])
    toep = jnp.stack([jnp.pad(m1[:, :, :TH - H * t], ((0, 0), (0, 0), (H * t, 0))) for t in range(T)],
                     axis=1).reshape(G, TH, TH)
    rev_re = pw_re[:, :T][:, ::-1]
    rev_im = pw_im[:, :T][:, ::-1]
    w_re = (rev_re * bb_re[:, None] - rev_im * bb_im[:, None]).reshape(NQ, 2, TH, P)
    w_im = (rev_re * bb_im[:, None] + rev_im * bb_re[:, None]).reshape(NQ, 2, TH, P)
    zw = jnp.zeros((NQ, TH, P), F32)
    wst_p = jnp.concatenate([
        jnp.concatenate([w_re[:, 0], zw, w_im[:, 0], zw], axis=-1),
        jnp.concatenate([zw, w_re[:, 1], zw, w_im[:, 1]], axis=-1)], axis=1)
    tp = toep.reshape(NQ, 2, TH, TH)
    zt = jnp.zeros((NQ, TH, TH), F32)
    toep_p = jnp.concatenate([jnp.concatenate([tp[:, 0], zt], axis=-1),
                              jnp.concatenate([zt, tp[:, 1]], axis=-1)], axis=1)
    v_t = cat(ca_re[:, 1:], -ca_im[:, 1:]).transpose(0, 2, 1).reshape(NQ, 2, 2, P, TH)
    zv = jnp.zeros((NQ, P, TH), F32)
    vmat_p = jnp.concatenate([
        jnp.concatenate([v_t[:, 0, 0], zv], axis=-1), jnp.concatenate([zv, v_t[:, 1, 0]], axis=-1),
        jnp.concatenate([v_t[:, 0, 1], zv], axis=-1), jnp.concatenate([zv, v_t[:, 1, 1]], axis=-1)], axis=1)
    old = jnp.arange(2 * TH).reshape(2, T, H)
    perm = jax.nn.one_hot(old.transpose(1, 0, 2).reshape(-1), 2 * TH, dtype=BF16).T
    wst_p, toep_p, vmat_p = wst_p.astype(BF16), toep_p.astype(BF16), vmat_p.astype(BF16)
    wst_p = jnp.einsum('ia,qij->qaj', perm, wst_p, preferred_element_type=F32)
    toep_p = jnp.einsum('ia,qij,jb->qab', perm, toep_p, perm, preferred_element_type=F32)
    vmat_p = jnp.einsum('qij,jb->qib', vmat_p, perm, preferred_element_type=F32)
    a_re = jnp.broadcast_to(pw_re[:, T].reshape(1, G * P), (SUBLANE, G * P))
    a_im = jnp.broadcast_to(pw_im[:, T].reshape(1, G * P), (SUBLANE, G * P))
    return wst_p.astype(BF16), a_re, a_im, toep_p.astype(BF16), vmat_p.astype(BF16)


def kernel(x, p, positions, g_mix_norm, w_in, g_q_lora, w_uq, g_kv_lora, w_ukv, g_q_head, g_k_head,
           lam_re, lam_im, log_dt, b_re, b_im, c_re, c_im, d_skip, w_glu, b_glu, g_out_attn,
           g_out_ssm, w_o, g_ffn_norm, w_gate, w_up, w_down, g_ple_norm, w_ple_gate, w_ple_proj):
    batch, seq, _ = x.shape
    assert batch == SUBLANE, "the S5 chunk recurrence keeps one batch row per sublane"
    depth = w_in.shape[0]
    n_tok = batch * seq
    row = lambda g: g[None, :].astype(F32)

    inv_freq = 1.0 / (ROPE_THETA ** (jnp.arange(0, QK_ROPE_DIM, 2, dtype=F32) / QK_ROPE_DIM))
    freq = _rope_tile(jnp.concatenate([inv_freq, inv_freq]))[None, :]
    ones = jnp.ones((ROPE_HALF,), F32)
    sgn = _rope_tile(jnp.concatenate([-ones, ones]))[None, :]
    pos2d = positions.reshape(n_tok, 1)

    x2d = x.reshape(n_tok, D_MODEL)
    for i in range(depth):
        win = w_in[i].astype(BF16)
        wq = w_uq[i].reshape(Q_LORA, N_HEADS, QK_HEAD_DIM)
        wuq = jnp.concatenate([wq[..., :QK_NOPE_DIM], _rope_tile(wq[..., QK_NOPE_DIM:])], axis=-1)
        wuq = wuq.reshape(Q_LORA, N_HEADS * HEAD_PAD).astype(BF16)
        wkv = w_ukv[i].reshape(KV_LORA, N_HEADS, QK_NOPE_DIM + V_HEAD_DIM)
        wukv = jnp.concatenate([wkv[..., :QK_NOPE_DIM].reshape(KV_LORA, -1),
                                wkv[..., QK_NOPE_DIM:].reshape(KV_LORA, -1)], axis=1).astype(BF16)

        q, k, v, u_slab = _inproj(x2d, pos2d, row(g_mix_norm[i]), win, row(g_q_lora[i]), wuq,
                                  row(g_kv_lora[i]), wukv, _head_gain(g_q_head[i]),
                                  _head_gain(g_k_head[i]), freq, sgn, batch=batch, seq=seq, tl=INPROJ_TILE)
        o_attn = _attention(q, k, v, batch=batch, seq=seq, tq=ATTN_Q_TILE)

        wst, a_re, a_im, toep, vmat = _s5_params(lam_re[i], lam_im[i], log_dt[i], b_re[i], b_im[i],
                                                 c_re[i], c_im[i], d_skip[i])
        o_ssm = _s5(u_slab, wst, a_re, a_im, toep, vmat,
                    w_glu[i].astype(BF16), row(b_glu[i]), row(g_out_ssm[i]),
                    batch=batch, seq=seq, chunks=S5_CHUNKS_PER_STEP)

        wo = w_o[i].astype(BF16)
        x2d = _oproj(x2d, o_attn, o_ssm, row(g_out_attn[i]), wo[:ATTN_WIDTH], wo[ATTN_WIDTH:],
                     batch=batch, seq=seq, tl=OPROJ_TILE)
        x2d = _ffn(x2d, row(g_ffn_norm[i]), w_gate[i].astype(BF16), w_up[i].astype(BF16),
                   w_down[i].astype(BF16), tm=FFN_TOKEN_TILE, tf=FFN_FF_TILE)
        x2d = _ple(x2d, p[i].reshape(n_tok, PLE_DIM), row(g_ple_norm[i]),
                   w_ple_gate[i].astype(BF16), w_ple_proj[i].astype(BF16), tm=PLE_TILE)
    return x2d.reshape(batch, seq, D_MODEL)
```

```python
import functools

import jax
import jax.numpy as jnp
from jax import lax
from jax.experimental import pallas as pl
from jax.experimental.pallas import tpu as pltpu

D_MODEL = 2048
PLE_DIM = 256
N_HEADS = 8
QK_NOPE_DIM = 128
QK_ROPE_DIM = 64
V_HEAD_DIM = 128
QK_HEAD_DIM = QK_NOPE_DIM + QK_ROPE_DIM
Q_LORA = 512
KV_LORA = 256
ATTN_WIDTH = N_HEADS * V_HEAD_DIM
ROPE_THETA = 10000.0
SSM_WIDTH = 1024
SSM_GROUP = 16
SSM_GROUPS = SSM_WIDTH // SSM_GROUP
SSM_STATE = 64
D_FF = 5632
EPS = 1e-6

LANE = 128
SUBLANE = 8
HEAD_PAD = 2 * LANE
ROPE_HALF = QK_ROPE_DIM // 2
CHUNK = SUBLANE
N_LANE_TILES = SSM_WIDTH // LANE
PAIR_BLOCK = 2 * SSM_GROUP
PAIRS_PER_TILE = LANE // PAIR_BLOCK
PAIR_W = 2 * LANE
N_PAIRS = SSM_GROUPS // 2
VMEM_LIMIT = 56 * 1024 * 1024
LOG2_E = 1.4426950408889634
INPROJ_TILE = 512
INPROJ_SUBTILES = 2
ATTN_Q_TILE = 256
ATTN_HEADS_PER_STEP = 2
QK_AHEAD = 4
S5_CHUNKS_PER_STEP = 16
OPROJ_TILE = 1024
OPROJ_SUBTILES = 4
FFN_TOKEN_TILE = 1024
FFN_FF_TILE = 512
FFN_SUBTILES = 2
PLE_TILE = 1024
PLE_SUBTILES = 2

BF16 = jnp.bfloat16
F32 = jnp.float32


def _rms(t, g, width=None):
    n = t.shape[-1] if width is None else width
    ss = jnp.sum(t * t, axis=-1, keepdims=True) * (1.0 / n)
    return t * lax.rsqrt(ss + EPS) * g


def _dot(a, b):
    return jnp.dot(a, b, preferred_element_type=F32)


def _sigmoid(t):
    return 1.0 / (1.0 + jnp.exp(-t))


def _params(*sem):
    return pltpu.CompilerParams(dimension_semantics=sem, vmem_limit_bytes=VMEM_LIMIT)


def _inproj_kernel(x_ref, pos_ref, gmix_ref, win_ref, gql_ref, wuq_ref, gkvl_ref, wukv_ref,
                   gq_ref, gk_ref, freq_ref, sgn_ref,
                   q_ref, k_ref, v_ref, u_ref):
    sub = x_ref.shape[0] // INPROJ_SUBTILES
    o1, o2 = Q_LORA, Q_LORA + KV_LORA
    kr_tile = o2 // LANE
    gq = gq_ref[...]
    gk = gk_ref[...]
    scale = QK_HEAD_DIM ** -0.5 * LOG2_E
    lane = lax.broadcasted_iota(jnp.int32, (sub, LANE), 1)
    lo_half = lane < LANE // 2

    def project(s):
        rs = slice(s * sub, (s + 1) * sub)
        h = _rms(x_ref[rs, :], gmix_ref[...]).astype(BF16)
        z = _dot(h, win_ref[...])
        tiles = [z[:, (kr_tile + m) * LANE:(kr_tile + m + 1) * LANE] for m in range(N_LANE_TILES)]
        last = z[:, (kr_tile + N_LANE_TILES) * LANE:]
        tiles.append(jnp.concatenate([last, jnp.zeros_like(last)], axis=1))
        sw = [pltpu.roll(t, LANE // 2, 1) for t in tiles]
        cs = slice(s * sub // CHUNK, (s + 1) * sub // CHUNK)
        for kt in range(N_LANE_TILES):
            u_ref[cs, kt, :, :] = jnp.where(lo_half, sw[kt], sw[kt + 1]).reshape(sub // CHUNK, CHUNK, LANE)
        zk = z[:, kr_tile * LANE:(kr_tile + 1) * LANE]
        kr = (jnp.where(lane < ROPE_HALF, zk, 0.0)
              + jnp.where((lane >= LANE // 2) & (lane < LANE // 2 + ROPE_HALF),
                          pltpu.roll(zk, ROPE_HALF, 1), 0.0))
        c_q = _rms(z[:, :o1], gql_ref[...]).astype(BF16)
        q = _dot(c_q, wuq_ref[...])
        c_kv = _rms(z[:, o1:o2], gkvl_ref[...]).astype(BF16)
        kv = _dot(c_kv, wukv_ref[...])
        return q, kv, kr

    def finish_heads(s, q, kv, kr):
        rs = slice(s * sub, (s + 1) * sub)
        v_ref[rs, :] = kv[:, ATTN_WIDTH:].astype(BF16)
        ang = pos_ref[rs, :].astype(F32) * freq_ref[...]
        cos_t = jnp.cos(ang)
        sin_t = jnp.sin(ang) * sgn_ref[...]

        def rope(t):
            return t * cos_t + pltpu.roll(t, LANE // 2, 1) * sin_t

        kr_ss = jnp.sum(kr * kr, axis=-1, keepdims=True)
        kr_rope = rope(kr * gk[:, LANE:])
        for hd in range(N_HEADS):
            qh = q[:, hd * HEAD_PAD:(hd + 1) * HEAD_PAD]
            qn = _rms(qh, gq, width=QK_HEAD_DIM) * scale
            q_ref[rs, hd * HEAD_PAD:hd * HEAD_PAD + LANE] = qn[:, :LANE].astype(BF16)
            q_ref[rs, hd * HEAD_PAD + LANE:(hd + 1) * HEAD_PAD] = rope(qn[:, LANE:]).astype(BF16)
            kn = kv[:, hd * QK_NOPE_DIM:(hd + 1) * QK_NOPE_DIM]
            ss = (jnp.sum(kn * kn, axis=-1, keepdims=True) + kr_ss) * (1.0 / QK_HEAD_DIM)
            rinv = lax.rsqrt(ss + EPS)
            k_ref[rs, hd * HEAD_PAD:hd * HEAD_PAD + LANE] = (kn * rinv * gk[:, :LANE]).astype(BF16)
            k_ref[rs, hd * HEAD_PAD + LANE:(hd + 1) * HEAD_PAD] = (kr_rope * rinv).astype(BF16)

    for s in range(INPROJ_SUBTILES):
        finish_heads(s, *project(s))


def _inproj(x2d, pos2d, gmix, win, gql, wuq, gkvl, wukv, gq, gk, freq, sgn, *, batch, seq, tl):
    nl = seq // tl
    tok = lambda w: pl.BlockSpec((tl, w), lambda b, i: (b * nl + i, 0))
    full = lambda a: pl.BlockSpec(a.shape, lambda b, i: (0,) * a.ndim)
    n_tok = batch * seq
    return pl.pallas_call(
        _inproj_kernel,
        grid=(batch, nl),
        in_specs=[tok(D_MODEL), tok(1), full(gmix), full(win), full(gql), full(wuq), full(gkvl),
                  full(wukv), full(gq), full(gk), full(freq), full(sgn)],
        out_specs=[tok(N_HEADS * HEAD_PAD), tok(N_HEADS * HEAD_PAD), tok(ATTN_WIDTH),
                   pl.BlockSpec((tl // CHUNK, N_LANE_TILES, CHUNK, LANE), lambda b, i: (i, 0, b, 0))],
        out_shape=[jax.ShapeDtypeStruct((n_tok, N_HEADS * HEAD_PAD), BF16),
                   jax.ShapeDtypeStruct((n_tok, N_HEADS * HEAD_PAD), BF16),
                   jax.ShapeDtypeStruct((n_tok, ATTN_WIDTH), BF16),
                   jax.ShapeDtypeStruct((seq // CHUNK, N_LANE_TILES, batch * CHUNK, LANE), F32)],
        compiler_params=_params("arbitrary", "arbitrary"),
    )(x2d, pos2d, gmix, win, gql, wuq, gkvl, wukv, gq, gk, freq, sgn)


def _attn_kernel(q_ref, k_ref, v_ref, o_ref, *, tq):
    seq = q_ref.shape[0]
    n = seq // tq
    vts = [v_ref[:, hd * V_HEAD_DIM:(hd + 1) * V_HEAD_DIM].T for hd in range(ATTN_HEADS_PER_STEP)]
    diag_mask = (lax.broadcasted_iota(jnp.int32, (tq, tq), 0)
                 <= lax.broadcasted_iota(jnp.int32, (tq, tq), 1))

    def scores_t(item):
        hd, i = item
        hs = slice(hd * HEAD_PAD, (hd + 1) * HEAD_PAD)
        return lax.dot_general(k_ref[:(i + 1) * tq, hs], q_ref[i * tq:(i + 1) * tq, hs],
                               (((1,), (1,)), ((), ())), preferred_element_type=F32)

    order = [(hd, i) for i in range(n - 1, -1, -1) for hd in range(ATTN_HEADS_PER_STEP)]
    sts = {item: scores_t(item) for item in order[:QK_AHEAD]}
    for pos, item in enumerate(order):
        hd, i = item
        kv_len = (i + 1) * tq
        st = sts.pop(item)
        if pos + QK_AHEAD < len(order):
            ahead = order[pos + QK_AHEAD]
            sts[ahead] = scores_t(ahead)
        sd = jnp.where(diag_mask, st[kv_len - tq:], -jnp.inf)
        m = jnp.max(sd, axis=0, keepdims=True)
        if i:
            m = jnp.maximum(m, jnp.max(st[:kv_len - tq], axis=0, keepdims=True))
        pd = jnp.exp2(sd - m)
        l = jnp.sum(pd, axis=0, keepdims=True)
        if i:
            pt = jnp.exp2(st[:kv_len - tq] - m)
            l = l + jnp.sum(pt, axis=0, keepdims=True)
            p = jnp.concatenate([pt.astype(BF16), pd.astype(BF16)], axis=0)
        else:
            p = pd.astype(BF16)
        ot = _dot(vts[hd][:, :kv_len], p)
        o_ref[i * tq:(i + 1) * tq, hd * V_HEAD_DIM:(hd + 1) * V_HEAD_DIM] = (ot / l).T.astype(o_ref.dtype)


def _attention(q, k, v, *, batch, seq, tq):
    hps = ATTN_HEADS_PER_STEP
    return pl.pallas_call(
        functools.partial(_attn_kernel, tq=tq),
        grid=(batch, N_HEADS // hps),
        in_specs=[pl.BlockSpec((seq, hps * HEAD_PAD), lambda b, h: (b, h)),
                  pl.BlockSpec((seq, hps * HEAD_PAD), lambda b, h: (b, h)),
                  pl.BlockSpec((seq, hps * V_HEAD_DIM), lambda b, h: (b, h))],
        out_specs=pl.BlockSpec((seq, hps * V_HEAD_DIM), lambda b, h: (b, h)),
        out_shape=jax.ShapeDtypeStruct((batch * seq, ATTN_WIDTH), BF16),
        compiler_params=_params("arbitrary", "arbitrary"),
    )(q, k, v)


def _block_transpose(tiles):
    lane = lax.broadcasted_iota(jnp.int32, tiles[0].shape, 1)
    for d in (2, 1):
        hi = (lane & (PAIR_BLOCK * d)) != 0
        new = list(tiles)
        for i in range(PAIRS_PER_TILE):
            if i & d == 0:
                a, b = tiles[i], tiles[i + d]
                new[i] = jnp.where(hi, pltpu.roll(b, PAIR_BLOCK * d, 1), a)
                new[i + d] = jnp.where(hi, b, pltpu.roll(a, LANE - PAIR_BLOCK * d, 1))
        tiles = new
    return tiles


def _s5_kernel(u_ref, wst_ref, are_ref, aim_ref, toep_ref, vmat_ref, wglu_ref, bglu_ref,
               gout_ref, o_ref, state_ref, x_ref, xs_ref, ys_ref, yg_ref, *, chunks, batch):
    rows_x = chunks * batch
    rows = rows_x * CHUNK

    @pl.when(pl.program_id(0) == 0)
    def _():
        state_ref[...] = jnp.zeros_like(state_ref)

    for kt in range(N_LANE_TILES):
        tiles = [u_ref[:, pl.ds(kt, 1), pl.ds(t, batch, stride=CHUNK), :].reshape(rows_x, LANE)
                 for t in range(CHUNK)]
        for half in range(CHUNK // PAIRS_PER_TILE):
            outs = _block_transpose(tiles[half * PAIRS_PER_TILE:(half + 1) * PAIRS_PER_TILE])
            for j in range(PAIRS_PER_TILE):
                c0 = (kt * PAIRS_PER_TILE + j) * PAIR_W + half * LANE
                x_ref[:, c0:c0 + LANE] = outs[j].astype(BF16)

    for q in range(N_PAIRS):
        cs = slice(q * PAIR_W, (q + 1) * PAIR_W)
        xs_ref[:, cs] = _dot(x_ref[:, cs], wst_ref[q])

    def step(c, carry):
        r0 = pl.multiple_of(c * batch, batch)
        for q in range(N_PAIRS):
            re = slice(q * PAIR_W, q * PAIR_W + LANE)
            im = slice(q * PAIR_W + LANE, (q + 1) * PAIR_W)
            al = slice(q * LANE, (q + 1) * LANE)
            s_re = state_ref[:, re]
            s_im = state_ref[:, im]
            a_re = are_ref[:, al]
            a_im = aim_ref[:, al]
            x_re = xs_ref[pl.ds(r0, batch), re]
            x_im = xs_ref[pl.ds(r0, batch), im]
            xs_ref[pl.ds(r0, batch), re] = s_re
            xs_ref[pl.ds(r0, batch), im] = s_im
            state_ref[:, re] = a_re * s_re - a_im * s_im + x_re
            state_ref[:, im] = a_re * s_im + a_im * s_re + x_im
        return carry

    lax.fori_loop(0, chunks, step, 0, unroll=True)

    for q in range(N_PAIRS):
        cs = slice(q * PAIR_W, (q + 1) * PAIR_W)
        xs_ref[:, cs] = (_dot(x_ref[:, cs], toep_ref[q])
                         + _dot(xs_ref[:, cs].astype(BF16), vmat_ref[q]))

    for kt in range(N_LANE_TILES):
        for half in range(CHUNK // PAIRS_PER_TILE):
            tiles = [xs_ref[:, (kt * PAIRS_PER_TILE + j) * PAIR_W + half * LANE:
                            (kt * PAIRS_PER_TILE + j) * PAIR_W + (half + 1) * LANE]
                     for j in range(PAIRS_PER_TILE)]
            outs = _block_transpose(tiles)
            for t4 in range(PAIRS_PER_TILE):
                t = half * PAIRS_PER_TILE + t4
                ys_ref[:, pl.ds(kt, 1), pl.ds(t, batch, stride=CHUNK), :] = (
                    outs[t4].reshape(chunks, 1, batch, LANE))

    for kt in range(N_LANE_TILES):
        yg_ref[:, kt * LANE:(kt + 1) * LANE] = jax.nn.gelu(ys_ref[:, kt].reshape(rows, LANE), approximate=True)
    y = yg_ref[...]
    gate = _sigmoid(_dot(y.astype(BF16), wglu_ref[...]) + bglu_ref[...])
    o_ref[...] = _rms(y * gate, gout_ref[...]).reshape(chunks, batch, CHUNK, SSM_WIDTH)


def _s5(u_slab, wst, a_re, a_im, toep, vmat, wglu, bglu, gout, *, batch, seq, chunks):
    n_chunks = seq // CHUNK
    rows_x = chunks * batch
    full = lambda a: pl.BlockSpec(a.shape, lambda i: (0,) * a.ndim, pipeline_mode=pl.Buffered(1))
    return pl.pallas_call(
        functools.partial(_s5_kernel, chunks=chunks, batch=batch),
        grid=(n_chunks // chunks,),
        in_specs=[pl.BlockSpec((chunks, N_LANE_TILES, batch * CHUNK, LANE), lambda i: (i, 0, 0, 0)),
                  full(wst), full(a_re), full(a_im), full(toep), full(vmat), full(wglu),
                  full(bglu), full(gout)],
        out_specs=pl.BlockSpec((chunks, batch, CHUNK, SSM_WIDTH), lambda i: (i, 0, 0, 0)),
        out_shape=jax.ShapeDtypeStruct((n_chunks, batch, CHUNK, SSM_WIDTH), F32),
        scratch_shapes=[pltpu.VMEM((batch, N_PAIRS * PAIR_W), F32),
                        pltpu.VMEM((rows_x, N_PAIRS * PAIR_W), BF16),
                        pltpu.VMEM((rows_x, N_PAIRS * PAIR_W), F32),
                        pltpu.VMEM((chunks, N_LANE_TILES, batch * CHUNK, LANE), F32),
                        pltpu.VMEM((rows_x * CHUNK, SSM_WIDTH), F32)],
        compiler_params=_params("arbitrary"),
    )(u_slab, wst, a_re, a_im, toep, vmat, wglu, bglu, gout)


def _oproj_kernel(x_ref, oa_ref, os_ref, ga_ref, woa_ref, wos_ref, out_ref):
    sub = x_ref.shape[0] // OPROJ_SUBTILES
    for s in range(OPROJ_SUBTILES):
        rs = slice(s * sub, (s + 1) * sub)
        os_ = os_ref[s * sub // CHUNK:(s + 1) * sub // CHUNK].reshape(sub, SSM_WIDTH).astype(BF16)
        acc = x_ref[rs, :] + _dot(os_, wos_ref[...])
        oa = _rms(oa_ref[rs, :].astype(F32), ga_ref[...]).astype(BF16)
        out_ref[rs, :] = acc + _dot(oa, woa_ref[...])


def _oproj(x2d, o_attn, o_ssm, g_attn, wo_a, wo_s, *, batch, seq, tl):
    nl = seq // tl
    tok = lambda w: pl.BlockSpec((tl, w), lambda b, i: (b * nl + i, 0))
    full = lambda a: pl.BlockSpec(a.shape, lambda b, i: (0,) * a.ndim)
    return pl.pallas_call(
        _oproj_kernel,
        grid=(batch, nl),
        in_specs=[tok(D_MODEL), tok(ATTN_WIDTH),
                  pl.BlockSpec((tl // CHUNK, None, CHUNK, SSM_WIDTH), lambda b, i: (i, b, 0, 0)),
                  full(g_attn), full(wo_a), full(wo_s)],
        out_specs=tok(D_MODEL),
        out_shape=jax.ShapeDtypeStruct((batch * seq, D_MODEL), F32),
        compiler_params=_params("arbitrary", "arbitrary"),
    )(x2d, o_attn, o_ssm, g_attn, wo_a, wo_s)


def _ffn_kernel(x_ref, g_ref, wg_ref, wu_ref, wd_ref, out_ref, h_ref):
    j = pl.program_id(1)

    def ff_block(h):
        gate = _dot(h, wg_ref[...])
        up = _dot(h, wu_ref[...])
        act = (gate * _sigmoid(gate) * up).astype(BF16)
        return _dot(act, wd_ref[...])

    @pl.when(j == 0)
    def _():
        sub = x_ref.shape[0] // FFN_SUBTILES
        for s in range(FFN_SUBTILES):
            rs = slice(s * sub, (s + 1) * sub)
            x = x_ref[rs, :]
            h = _rms(x, g_ref[...]).astype(BF16)
            h_ref[rs, :] = h
            out_ref[rs, :] = x + ff_block(h)

    @pl.when(j > 0)
    def _():
        out_ref[...] += ff_block(h_ref[...])


def _ffn(x2d, g, wg, wu, wd, *, tm, tf):
    n_tok = x2d.shape[0]
    return pl.pallas_call(
        _ffn_kernel,
        grid=(n_tok // tm, D_FF // tf),
        in_specs=[pl.BlockSpec((tm, D_MODEL), lambda i, j: (i, 0)),
                  pl.BlockSpec((1, D_MODEL), lambda i, j: (0, 0)),
                  pl.BlockSpec((D_MODEL, tf), lambda i, j: (0, j)),
                  pl.BlockSpec((D_MODEL, tf), lambda i, j: (0, j)),
                  pl.BlockSpec((tf, D_MODEL), lambda i, j: (j, 0))],
        out_specs=pl.BlockSpec((tm, D_MODEL), lambda i, j: (i, 0)),
        out_shape=jax.ShapeDtypeStruct((n_tok, D_MODEL), F32),
        scratch_shapes=[pltpu.VMEM((tm, D_MODEL), BF16)],
        compiler_params=_params("arbitrary", "arbitrary"),
    )(x2d, g, wg, wu, wd)


def _ple_kernel(x_ref, p_ref, g_ref, wpg_ref, wpp_ref, out_ref):
    sub = x_ref.shape[0] // PLE_SUBTILES
    for s in range(PLE_SUBTILES):
        rs = slice(s * sub, (s + 1) * sub)
        x = x_ref[rs, :]
        h = _rms(x, g_ref[...]).astype(BF16)
        gate = _sigmoid(_dot(h, wpg_ref[...]))
        out_ref[rs, :] = x + gate * _dot(p_ref[rs, :].astype(BF16), wpp_ref[...])


def _ple(x2d, p2d, g, wpg, wpp, *, tm):
    n_tok = x2d.shape[0]
    full = lambda a: pl.BlockSpec(a.shape, lambda i: (0,) * a.ndim)
    return pl.pallas_call(
        _ple_kernel,
        grid=(n_tok // tm,),
        in_specs=[pl.BlockSpec((tm, D_MODEL), lambda i: (i, 0)),
                  pl.BlockSpec((tm, PLE_DIM), lambda i: (i, 0)),
                  full(g), full(wpg), full(wpp)],
        out_specs=pl.BlockSpec((tm, D_MODEL), lambda i: (i, 0)),
        out_shape=jax.ShapeDtypeStruct((n_tok, D_MODEL), F32),
        compiler_params=_params("arbitrary"),
    )(x2d, p2d, g, wpg, wpp)


def _rope_tile(t):
    z = jnp.zeros(t.shape[:-1] + (ROPE_HALF,), t.dtype)
    return jnp.concatenate([t[..., :ROPE_HALF], z, t[..., ROPE_HALF:], z], axis=-1)


def _head_gain(g):
    return jnp.concatenate([g[:QK_NOPE_DIM], _rope_tile(g[QK_NOPE_DIM:])])[None, :].astype(F32)


def _s5_params(lam_re, lam_im, log_dt, b_re, b_im, c_re, c_im, d_skip):
    G, P, H, T, NQ = SSM_GROUPS, SSM_STATE, SSM_GROUP, CHUNK, N_PAIRS
    TH = T * H
    lr = jnp.minimum(lam_re.astype(F32), -1e-4)
    li = lam_im.astype(F32)
    dt = jnp.exp(log_dt.astype(F32))[:, None]
    mag = jnp.exp(lr * dt)
    abar_re = mag * jnp.cos(li * dt)
    abar_im = mag * jnp.sin(li * dt)
    den = lr * lr + li * li
    num_re = abar_re - 1.0
    num_im = abar_im
    coef_re = ((num_re * lr + num_im * li) / den)[:, None, :]
    coef_im = ((num_im * lr - num_re * li) / den)[:, None, :]
    br = b_re.astype(F32).transpose(0, 2, 1)
    bim = b_im.astype(F32).transpose(0, 2, 1)
    bb_re = coef_re * br - coef_im * bim
    bb_im = coef_re * bim + coef_im * br
    pw_re, pw_im = [jnp.ones_like(abar_re)], [jnp.zeros_like(abar_im)]
    for _ in range(T):
        r, i = pw_re[-1], pw_im[-1]
        pw_re.append(r * abar_re - i * abar_im)
        pw_im.append(r * abar_im + i * abar_re)
    pw_re = jnp.stack(pw_re, axis=1)[:, :, None, :]
    pw_im = jnp.stack(pw_im, axis=1)[:, :, None, :]
    cr = c_re.astype(F32)[:, None]
    ci = c_im.astype(F32)[:, None]
    ca_re = cr * pw_re - ci * pw_im
    ca_im = cr * pw_im + ci * pw_re
    cat = lambda re, im: jnp.concatenate([re.reshape(G, -1, P), im.reshape(G, -1, P)], axis=-1)
    m1 = jnp.einsum('gik,gjk->gij', cat(bb_re, bb_im), cat(ca_re[:, :T], -ca_im[:, :T]),
                    precision=lax.Precision.HIGHEST)
    m1 = m1.at[:, :, :H].add(jnp.eye(H, dtype=F32)[None] * d_skip.astype(F32)[:, None, :])
    toep = jnp.stack([jnp.pad(m1[:, :, :TH - H * t], ((0, 0), (0, 0), (H * t, 0))) for t in range(T)],
                     axis=1).reshape(G, TH, TH)
    rev_re = pw_re[:, :T][:, ::-1]
    rev_im = pw_im[:, :T][:, ::-1]
    w_re = (rev_re * bb_re[:, None] - rev_im * bb_im[:, None]).reshape(NQ, 2, TH, P)
    w_im = (rev_re * bb_im[:, None] + rev_im * bb_re[:, None]).reshape(NQ, 2, TH, P)
    zw = jnp.zeros((NQ, TH, P), F32)
    wst_p = jnp.concatenate([
        jnp.concatenate([w_re[:, 0], zw, w_im[:, 0], zw], axis=-1),
        jnp.concatenate([zw, w_re[:, 1], zw, w_im[:, 1]], axis=-1)], axis=1)
    tp = toep.reshape(NQ, 2, TH, TH)
    zt = jnp.zeros((NQ, TH, TH), F32)
    toep_p = jnp.concatenate([jnp.concatenate([tp[:, 0], zt], axis=-1),
                              jnp.concatenate([zt, tp[:, 1]], axis=-1)], axis=1)
    v_t = cat(ca_re[:, 1:], -ca_im[:, 1:]).transpose(0, 2, 1).reshape(NQ, 2, 2, P, TH)
    zv = jnp.zeros((NQ, P, TH), F32)
    vmat_p = jnp.concatenate([
        jnp.concatenate([v_t[:, 0, 0], zv], axis=-1), jnp.concatenate([zv, v_t[:, 1, 0]], axis=-1),
        jnp.concatenate([v_t[:, 0, 1], zv], axis=-1), jnp.concatenate([zv, v_t[:, 1, 1]], axis=-1)], axis=1)
    old = jnp.arange(2 * TH).reshape(2, T, H)
    perm = jax.nn.one_hot(old.transpose(1, 0, 2).reshape(-1), 2 * TH, dtype=BF16).T
    wst_p, toep_p, vmat_p = wst_p.astype(BF16), toep_p.astype(BF16), vmat_p.astype(BF16)
    wst_p = jnp.einsum('ia,qij->qaj', perm, wst_p, preferred_element_type=F32)
    toep_p = jnp.einsum('ia,qij,jb->qab', perm, toep_p, perm, preferred_element_type=F32)
    vmat_p = jnp.einsum('qij,jb->qib', vmat_p, perm, preferred_element_type=F32)
    a_re = jnp.broadcast_to(pw_re[:, T].reshape(1, G * P), (SUBLANE, G * P))
    a_im = jnp.broadcast_to(pw_im[:, T].reshape(1, G * P), (SUBLANE, G * P))
    return wst_p.astype(BF16), a_re, a_im, toep_p.astype(BF16), vmat_p.astype(BF16)


def kernel(x, p, positions, g_mix_norm, w_in, g_q_lora, w_uq, g_kv_lora, w_ukv, g_q_head, g_k_head,
           lam_re, lam_im, log_dt, b_re, b_im, c_re, c_im, d_skip, w_glu, b_glu, g_out_attn,
           g_out_ssm, w_o, g_ffn_norm, w_gate, w_up, w_down, g_ple_norm, w_ple_gate, w_ple_proj):
    batch, seq, _ = x.shape
    assert batch == SUBLANE, "the S5 chunk recurrence keeps one batch row per sublane"
    depth = w_in.shape[0]
    n_tok = batch * seq
    row = lambda g: g[None, :].astype(F32)

    inv_freq = 1.0 / (ROPE_THETA ** (jnp.arange(0, QK_ROPE_DIM, 2, dtype=F32) / QK_ROPE_DIM))
    freq = _rope_tile(jnp.concatenate([inv_freq, inv_freq]))[None, :]
    ones = jnp.ones((ROPE_HALF,), F32)
    sgn = _rope_tile(jnp.concatenate([-ones, ones]))[None, :]
    pos2d = positions.reshape(n_tok, 1)

    x2d = x.reshape(n_tok, D_MODEL)
    for i in range(depth):
        win = w_in[i].astype(BF16)
        wq = w_uq[i].reshape(Q_LORA, N_HEADS, QK_HEAD_DIM)
        wuq = jnp.concatenate([wq[..., :QK_NOPE_DIM], _rope_tile(wq[..., QK_NOPE_DIM:])], axis=-1)
        wuq = wuq.reshape(Q_LORA, N_HEADS * HEAD_PAD).astype(BF16)
        wkv = w_ukv[i].reshape(KV_LORA, N_HEADS, QK_NOPE_DIM + V_HEAD_DIM)
        wukv = jnp.concatenate([wkv[..., :QK_NOPE_DIM].reshape(KV_LORA, -1),
                                wkv[..., QK_NOPE_DIM:].reshape(KV_LORA, -1)], axis=1).astype(BF16)

        q, k, v, u_slab = _inproj(x2d, pos2d, row(g_mix_norm[i]), win, row(g_q_lora[i]), wuq,
                                  row(g_kv_lora[i]), wukv, _head_gain(g_q_head[i]),
                                  _head_gain(g_k_head[i]), freq, sgn, batch=batch, seq=seq, tl=INPROJ_TILE)
        o_attn = _attention(q, k, v, batch=batch, seq=seq, tq=ATTN_Q_TILE)

        wst, a_re, a_im, toep, vmat = _s5_params(lam_re[i], lam_im[i], log_dt[i], b_re[i], b_im[i],
                                                 c_re[i], c_im[i], d_skip[i])
        o_ssm = _s5(u_slab, wst, a_re, a_im, toep, vmat,
                    w_glu[i].astype(BF16), row(b_glu[i]), row(g_out_ssm[i]),
                    batch=batch, seq=seq, chunks=S5_CHUNKS_PER_STEP)

        wo = w_o[i].astype(BF16)
        x2d = _oproj(x2d, o_attn, o_ssm, row(g_out_attn[i]), wo[:ATTN_WIDTH], wo[ATTN_WIDTH:],
                     batch=batch, seq=seq, tl=OPROJ_TILE)
        x2d = _ffn(x2d, row(g_ffn_norm[i]), w_gate[i].astype(BF16), w_up[i].astype(BF16),
                   w_down[i].astype(BF16), tm=FFN_TOKEN_TILE, tf=FFN_FF_TILE)
        x2d = _ple(x2d, p[i].reshape(n_tok, PLE_DIM), row(g_ple_norm[i]),
                   w_ple_gate[i].astype(BF16), w_ple_proj[i].astype(BF16), tm=PLE_TILE)
    return x2d.reshape(batch, seq, D_MODEL)
```
